```python
import math
import jax
import jax.numpy as jnp
from jax import lax
import numpy as np

D_MODEL = 1024
BATCH = 8
SEQ = 2048
DEPTH = 2
DEC_BATCH = 128
DEC_SEQ = 4
PAST_LEN = 2048
PAGE_SIZE = 128

HEAD_DIM = 64
H_A = 8
H_B = 8
KV_B = 2
G_B = H_B // KV_B
H_IDX = 8
D_IDX = 64
MOBA_BLOCK = 256
MOBA_TOPK = 3
DSA_TOPK = 256
NUM_BUCKETS = 32
MAX_DISTANCE = 128
D_CONV = D_MODEL
CONV_W = 3
N_GROUPS = 4
EXPERTS_PER_GROUP = 8
N_EXPERTS = N_GROUPS * EXPERTS_PER_GROUP
EXPERT_TOPK = 2
D_EXPERT = D_MODEL // 2
EXPERT_ROWS = 128
MOBA_Q_CHUNK = 32
DSA_Q_CHUNK = 128
LN_EPS = 1e-5
N_ATTN_LAYERS = (DEPTH + 1) // 2
N_CONV_LAYERS = DEPTH // 2
ATTN_COLS = (H_A * HEAD_DIM, H_A * HEAD_DIM, H_A * HEAD_DIM,
             H_B * HEAD_DIM, KV_B * HEAD_DIM, KV_B * HEAD_DIM,
             H_IDX * D_IDX, D_IDX, H_IDX)
ATTN_SPLITS = tuple(int(s) for s in np.cumsum(ATTN_COLS)[:-1])
ATTN_IN = sum(ATTN_COLS)
ATTN_OUT = (H_A + H_B) * HEAD_DIM

kernel_name = 'hybrid_moba_dsa_shortconv_hmoe_step'

F32 = jnp.float32


def layer_norm(x, g, b):
    xf = x.astype(F32)
    mu = jnp.mean(xf, axis=-1, keepdims=True)
    var = jnp.mean(jnp.square(xf - mu), axis=-1, keepdims=True)
    return ((xf - mu) * lax.rsqrt(var + LN_EPS) * g + b).astype(x.dtype)


def rel_bucket(dist):
    n = jnp.maximum(dist, 0)
    max_exact = NUM_BUCKETS // 2
    nf = jnp.maximum(n, 1).astype(F32)
    large = max_exact + (jnp.log(nf / max_exact) / math.log(MAX_DISTANCE / max_exact)
                         * (NUM_BUCKETS - max_exact)).astype(jnp.int32)
    large = jnp.minimum(large, NUM_BUCKETS - 1)
    return jnp.where(n < max_exact, n, large)


def to_chunks(a, c):
    b, t = a.shape[:2]
    return jnp.moveaxis(a.reshape((b, t // c, c) + a.shape[2:]), 1, 0)


def from_chunks(a):
    n, b, c, f = a.shape
    return jnp.moveaxis(a, 0, 1).reshape(b, n * c, f)


def gather_pages(pool, page_table):
    rows = pool[page_table]
    return rows.reshape((page_table.shape[0], -1) + pool.shape[2:])


def moba_attend(q, k, v, q_pos, bias_a, q_chunk):
    b, t, h, dh = q.shape
    l = k.shape[1]
    nb = -(-l // MOBA_BLOCK)
    pad = ((0, 0), (0, nb * MOBA_BLOCK - l), (0, 0), (0, 0))
    kb = jnp.pad(k, pad).reshape(b, nb, MOBA_BLOCK, h, dh).transpose(0, 3, 1, 2, 4)
    vb = jnp.pad(v, pad).reshape(b, nb, MOBA_BLOCK, h, dh).transpose(0, 3, 1, 2, 4)
    k_mean = jnp.mean(kb.astype(F32), axis=3)
    top = min(MOBA_TOPK, nb)
    blk = jnp.arange(MOBA_BLOCK, dtype=jnp.int32)
    b_ix = jnp.arange(b)[:, None, None, None]
    h_ix = jnp.arange(h)[None, None, :, None]
    h_ix5 = h_ix[..., None]
    scale = 1.0 / math.sqrt(dh)

    def chunk(args):
        qc, pc = args
        c = qc.shape[1]
        own = pc // MOBA_BLOCK
        gate = jnp.einsum('bchd,bhnd->bchn', qc.astype(F32), k_mean)
        past = jnp.arange(nb)[None, :] < own[:, None]
        gate = jnp.where(past[None, :, None, :], gate, -jnp.inf)
        _, sel = lax.top_k(gate, top)
        sel_ok = sel < own[None, :, None, None]
        ks = kb[b_ix, h_ix, sel]
        vs = vb[b_ix, h_ix, sel]
        ko = kb[:, :, own].transpose(0, 2, 1, 3, 4)
        vo = vb[:, :, own].transpose(0, 2, 1, 3, 4)
        dist_sel = pc[None, :, None, None, None] - (sel[..., None] * MOBA_BLOCK + blk)
        s_sel = (jnp.einsum('bchd,bchkld->bchkl', qc, ks).astype(F32) * scale
                 + bias_a[rel_bucket(dist_sel), h_ix5].astype(F32))
        s_sel = jnp.where(sel_ok[..., None], s_sel, -jnp.inf)
        dist_own = pc[:, None] - (own[:, None] * MOBA_BLOCK + blk[None, :])
        s_own = (jnp.einsum('bchd,bchld->bchl', qc, ko).astype(F32) * scale
                 + bias_a[rel_bucket(dist_own)].transpose(0, 2, 1)[None].astype(F32))
        s_own = jnp.where((dist_own >= 0)[None, :, None, :], s_own, -jnp.inf)
        s = jnp.concatenate([s_sel.reshape(b, c, h, top * MOBA_BLOCK), s_own], axis=-1)
        p = jax.nn.softmax(s, axis=-1).astype(v.dtype)
        p_sel = p[..., :top * MOBA_BLOCK].reshape(s_sel.shape)
        p_own = p[..., top * MOBA_BLOCK:]
        o = (jnp.einsum('bchkl,bchkld->bchd', p_sel, vs)
             + jnp.einsum('bchl,bchld->bchd', p_own, vo))
        return o.reshape(b, c, h * dh)

    out = lax.map(chunk, (to_chunks(q, q_chunk), q_pos.reshape(-1, q_chunk)))
    return from_chunks(out)


def dsa_attend(q, k, v, q_idx, w_idx, k_idx, q_pos, bias_b, k_sel, q_chunk):
    b, t, h, dh = q.shape
    l = k.shape[1]
    k_pos = jnp.arange(l, dtype=jnp.int32)
    kif = k_idx.astype(F32)
    b_ix = jnp.arange(b)[:, None, None]
    scale = 1.0 / math.sqrt(dh)
    idx_scale = 1.0 / math.sqrt(D_IDX * H_IDX)

    def chunk(args):
        qc, qic, wc, pc = args
        c = qc.shape[1]
        s_idx = jnp.einsum('bchd,bld->bchl', qic.astype(F32), kif)
        score = jnp.einsum('bchl,bch->bcl', jax.nn.relu(s_idx), wc.astype(F32)) * idx_scale
        score = jnp.where((k_pos[None, :] <= pc[:, None])[None], score, -jnp.inf)
        _, idx = lax.top_k(score, k_sel)
        dist = pc[None, :, None] - idx
        ks = k[b_ix, idx]
        vs = v[b_ix, idx]
        qg = qc.reshape(b, c, KV_B, G_B, dh)
        s = jnp.einsum('bcngd,bcknd->bcngk', qg, ks).astype(F32) * scale
        bias = bias_b[rel_bucket(dist)].reshape(b, c, k_sel, KV_B, G_B).transpose(0, 1, 3, 4, 2)
        s = jnp.where((dist >= 0)[:, :, None, None, :], s + bias.astype(F32), -jnp.inf)
        p = jax.nn.softmax(s, axis=-1).astype(v.dtype)
        o = jnp.einsum('bcngk,bcknd->bcngd', p, vs)
        return o.reshape(b, c, h * dh)

    out = lax.map(chunk, (to_chunks(q, q_chunk), to_chunks(q_idx, q_chunk),
                          to_chunks(w_idx, q_chunk), q_pos.reshape(-1, q_chunk)))
    return from_chunks(out)


def attn_mixer(x, q_pos, past, w_in, w_out, rel_bias, moba_chunk, dsa_chunk):
    b, t, _ = x.shape
    qa, ka, va, qb, kb, vb, qi, ki, wi = jnp.split(x @ w_in, ATTN_SPLITS, axis=-1)
    qa = qa.reshape(b, t, H_A, HEAD_DIM)
    ka = ka.reshape(b, t, H_A, HEAD_DIM)
    va = va.reshape(b, t, H_A, HEAD_DIM)
    qb = qb.reshape(b, t, H_B, HEAD_DIM)
    kb = kb.reshape(b, t, KV_B, HEAD_DIM)
    vb = vb.reshape(b, t, KV_B, HEAD_DIM)
    qi = qi.reshape(b, t, H_IDX, D_IDX)
    new_rows = (ka, va, kb, vb, ki)
    if past is None:
        full = new_rows
    else:
        full = tuple(jnp.concatenate([p.astype(n.dtype), n], axis=1) for p, n in zip(past, new_rows))
    ka_f, va_f, kb_f, vb_f, ki_f = full
    k_sel = min(DSA_TOPK, ka_f.shape[1] // 4)
    ya = moba_attend(qa, ka_f, va_f, q_pos, rel_bias[:, :H_A], moba_chunk)
    yb = dsa_attend(qb, kb_f, vb_f, qi, wi, ki_f, q_pos, rel_bias[:, H_A:], k_sel, dsa_chunk)
    y = jnp.concatenate([ya, yb], axis=-1) @ w_out
    return y, new_rows


def conv_mixer(x, past_u, w_in, conv_w, w_out):
    b, t, _ = x.shape
    h, gate_b, gate_c = jnp.split(x @ w_in, 3, axis=-1)
    u = gate_c * h
    if past_u is None:
        past_u = jnp.zeros((b, CONV_W - 1, D_CONV), u.dtype)
    up = jnp.concatenate([past_u.astype(u.dtype), u], axis=1)
    c = sum(up[:, j:j + t] * conv_w[j] for j in range(CONV_W))
    y = (gate_b * c) @ w_out
    return y, up[:, t:]


def routed_experts(x2, e_idx, gates, w_gate, w_up, w_down):
    n, d = x2.shape
    a = n * EXPERT_TOPK
    e_flat = e_idx.reshape(a)
    tok_flat = jnp.repeat(jnp.arange(n, dtype=jnp.int32), EXPERT_TOPK)
    g_flat = gates.reshape(a)
    order = jnp.argsort(e_flat).astype(jnp.int32)
    e_sorted = e_flat[order]
    counts = jnp.bincount(e_flat, length=N_EXPERTS)
    padded = (counts + EXPERT_ROWS - 1) // EXPERT_ROWS * EXPERT_ROWS
    end_pad = jnp.cumsum(padded)
    start_pad = end_pad - padded
    start = jnp.cumsum(counts) - counts
    dest = start_pad[e_sorted] + jnp.arange(a, dtype=jnp.int32) - start[e_sorted]
    n_blocks = -(-a // EXPERT_ROWS) + N_EXPERTS
    row_src = jnp.full((n_blocks * EXPERT_ROWS,), a, jnp.int32).at[dest].set(order)
    valid = row_src < a
    src = jnp.minimum(row_src, a - 1)
    tok_rows = tok_flat[src]
    g_rows = jnp.where(valid, g_flat[src], 0).astype(x2.dtype)
    block_expert = jnp.minimum(
        jnp.searchsorted(end_pad, jnp.arange(n_blocks, dtype=jnp.int32) * EXPERT_ROWS, side='right'),
        N_EXPERTS - 1)
    xr = x2[tok_rows].reshape(n_blocks, EXPERT_ROWS, d)

    def block_ffn(args):
        xb, e = args
        hdn = jax.nn.silu(xb @ w_gate[e]) * (xb @ w_up[e])
        return hdn @ w_down[e]

    yr = lax.map(block_ffn, (xr, block_expert)).reshape(-1, d)
    return jnp.zeros_like(x2).at[tok_rows].add(yr * g_rows[:, None])


def hier_moe(x, w_rg, b_rg, w_re, b_re, w_gate, w_up, w_down):
    b, t, d = x.shape
    x2 = x.reshape(b * t, d)
    p_group = jax.nn.softmax((x2 @ w_rg + b_rg).astype(F32), axis=-1)
    g_val, g_idx = lax.top_k(p_group, 1)
    logit_e = (x2 @ w_re + b_re).astype(F32).reshape(-1, N_GROUPS, EXPERTS_PER_GROUP)
    logit_e = jnp.take_along_axis(logit_e, g_idx[:, :, None], axis=1)[:, 0]
    p_exp = jax.nn.softmax(logit_e, axis=-1)
    e_val, e_loc = lax.top_k(p_exp, EXPERT_TOPK)
    gates = g_val * e_val / jnp.sum(e_val, axis=-1, keepdims=True)
    e_idx = g_idx * EXPERTS_PER_GROUP + e_loc
    y = routed_experts(x2, e_idx, gates.astype(x.dtype), w_gate, w_up, w_down)
    return y.reshape(b, t, d)


def stack_rows(rows, j):
    return jnp.stack([r[j] for r in rows])


def setup_inputs(seed: int = 0) -> dict:
    key = jax.random.key(seed)
    ks = jax.random.split(key, 24)
    n_pages = PAST_LEN // PAGE_SIZE
    n_phys = (DEC_BATCH * n_pages * 5) // 4
    beta = (8 * DEPTH) ** -0.25

    def nrm(k, shape, s):
        return jax.random.normal(k, shape, F32) * s

    col_scale = np.concatenate([np.full((c,), beta if i in (2, 5) else 1.0, np.float32)
                                for i, c in enumerate(ATTN_COLS)])
    page_table = jax.random.permutation(ks[8], n_phys)[:DEC_BATCH * n_pages]
    page_table = page_table.reshape(DEC_BATCH, n_pages).astype(jnp.int32)
    return {
        'x_prompt': nrm(ks[0], (BATCH, SEQ, D_MODEL), 1.0),
        'x_sample': nrm(ks[1], (DEC_BATCH, DEC_SEQ, D_MODEL), 1.0),
        'cache_k_a': nrm(ks[2], (N_ATTN_LAYERS, n_phys, PAGE_SIZE, H_A, HEAD_DIM), 1.0),
        'cache_v_a': nrm(ks[3], (N_ATTN_LAYERS, n_phys, PAGE_SIZE, H_A, HEAD_DIM), 1.0),
        'cache_k_b': nrm(ks[4], (N_ATTN_LAYERS, n_phys, PAGE_SIZE, KV_B, HEAD_DIM), 1.0),
        'cache_v_b': nrm(ks[5], (N_ATTN_LAYERS, n_phys, PAGE_SIZE, KV_B, HEAD_DIM), 1.0),
        'cache_k_idx': nrm(ks[6], (N_ATTN_LAYERS, n_phys, PAGE_SIZE, D_IDX), 1.0),
        'state_conv': nrm(ks[7], (N_CONV_LAYERS, DEC_BATCH, CONV_W - 1, D_CONV), 1.0),
        'page_table': page_table,
        'rel_bias': nrm(ks[9], (NUM_BUCKETS, H_A + H_B), 0.5),
        'w_attn_in': nrm(ks[10], (N_ATTN_LAYERS, D_MODEL, ATTN_IN), D_MODEL ** -0.5) * col_scale,
        'w_attn_out': nrm(ks[11], (N_ATTN_LAYERS, ATTN_OUT, D_MODEL), ATTN_OUT ** -0.5 * beta),
        'w_conv_in': nrm(ks[12], (N_CONV_LAYERS, D_MODEL, 3 * D_CONV), D_MODEL ** -0.5),
        'conv_w': nrm(ks[13], (N_CONV_LAYERS, CONV_W, D_CONV), CONV_W ** -0.5),
        'w_conv_out': nrm(ks[14], (N_CONV_LAYERS, D_CONV, D_MODEL), D_CONV ** -0.5 * beta),
        'w_router_group': nrm(ks[15], (DEPTH, D_MODEL, N_GROUPS), D_MODEL ** -0.5),
        'b_router_group': nrm(ks[16], (DEPTH, N_GROUPS), 0.01),
        'w_router_expert': nrm(ks[17], (DEPTH, D_MODEL, N_EXPERTS), D_MODEL ** -0.5),
        'b_router_expert': nrm(ks[18], (DEPTH, N_EXPERTS), 0.01),
        'w_exp_gate': nrm(ks[19], (DEPTH, N_EXPERTS, D_MODEL, D_EXPERT), D_MODEL ** -0.5),
        'w_exp_up': nrm(ks[20], (DEPTH, N_EXPERTS, D_MODEL, D_EXPERT), D_MODEL ** -0.5 * beta),
        'w_exp_down': nrm(ks[21], (DEPTH, N_EXPERTS, D_EXPERT, D_MODEL), D_EXPERT ** -0.5 * beta),
        'ln_g': 1.0 + nrm(ks[22], (DEPTH, 2, D_MODEL), 0.02),
        'ln_b': nrm(ks[23], (DEPTH, 2, D_MODEL), 0.02),
    }


def reference(x_prompt, x_sample, cache_k_a, cache_v_a, cache_k_b, cache_v_b, cache_k_idx,
              state_conv, page_table, rel_bias, w_attn_in, w_attn_out, w_conv_in, conv_w,
              w_conv_out, w_router_group, b_router_group, w_router_expert, b_router_expert,
              w_exp_gate, w_exp_up, w_exp_down, ln_g, ln_b):
    alpha = (2 * DEPTH) ** 0.25
    t_p = x_prompt.shape[1]
    t_s = x_sample.shape[1]
    past_len = page_table.shape[1] * cache_k_a.shape[2]
    pos_p = jnp.arange(t_p, dtype=jnp.int32)
    pos_s = past_len + jnp.arange(t_s, dtype=jnp.int32)
    xp, xs = x_prompt, x_sample
    attn_p, attn_s, conv_p, conv_s = [], [], [], []
    for layer in range(DEPTH):
        i = layer // 2
        if layer % 2 == 0:
            past = tuple(gather_pages(c[i], page_table)
                         for c in (cache_k_a, cache_v_a, cache_k_b, cache_v_b, cache_k_idx))
            hp, rows_p = attn_mixer(xp, pos_p, None, w_attn_in[i], w_attn_out[i], rel_bias,
                                    min(MOBA_Q_CHUNK, t_p), min(DSA_Q_CHUNK, t_p))
            hs, rows_s = attn_mixer(xs, pos_s, past, w_attn_in[i], w_attn_out[i], rel_bias, 1, t_s)
            attn_p.append(rows_p)
            attn_s.append(rows_s)
        else:
            hp, st_p = conv_mixer(xp, None, w_conv_in[i], conv_w[i], w_conv_out[i])
            hs, st_s = conv_mixer(xs, state_conv[i], w_conv_in[i], conv_w[i], w_conv_out[i])
            conv_p.append(st_p)
            conv_s.append(st_s)
        xp = layer_norm(alpha * xp + hp, ln_g[layer, 0], ln_b[layer, 0])
        xs = layer_norm(alpha * xs + hs, ln_g[layer, 0], ln_b[layer, 0])
        moe_w = (w_router_group[layer], b_router_group[layer], w_router_expert[layer],
                 b_router_expert[layer], w_exp_gate[layer], w_exp_up[layer], w_exp_down[layer])
        xp = layer_norm(alpha * xp + hier_moe(xp, *moe_w), ln_g[layer, 1], ln_b[layer, 1])
        xs = layer_norm(alpha * xs + hier_moe(xs, *moe_w), ln_g[layer, 1], ln_b[layer, 1])
    return (xp, xs,
            stack_rows(attn_p, 0), stack_rows(attn_p, 1), stack_rows(attn_p, 2),
            stack_rows(attn_p, 3), stack_rows(attn_p, 4), jnp.stack(conv_p),
            stack_rows(attn_s, 0), stack_rows(attn_s, 1), stack_rows(attn_s, 2),
            stack_rows(attn_s, 3), stack_rows(attn_s, 4), jnp.stack(conv_s))
```

```python
import functools
import math

import numpy as np
import jax
import jax.numpy as jnp
from jax import lax
from jax.experimental import pallas as pl
from jax.experimental.pallas import tpu as pltpu

F32 = jnp.float32
BF16 = jnp.bfloat16
I32 = jnp.int32

HEAD_DIM = 64
H_A = 8
H_B = 8
KV_B = 2
G_B = H_B // KV_B
H_IDX = 8
D_IDX = 64
MOBA_BLOCK = 256
MOBA_TOPK = 3
DSA_TOPK = 256
NUM_BUCKETS = 32
MAX_DISTANCE = 128
CONV_W = 3
N_GROUPS = 4
EXPERTS_PER_GROUP = 8
N_EXPERTS = N_GROUPS * EXPERTS_PER_GROUP
LN_EPS = 1e-5
ATTN_COLS = (H_A * HEAD_DIM, H_A * HEAD_DIM, H_A * HEAD_DIM,
             H_B * HEAD_DIM, KV_B * HEAD_DIM, KV_B * HEAD_DIM,
             H_IDX * D_IDX, D_IDX, H_IDX)

LANES = 128
SUBLANES = 8
VMEM_LIMIT = 56 * 1024 * 1024

ATT_TILE = MOBA_BLOCK
ROW_TILE = 256
FFN_ROWS = 256
NEG_INF = float("-inf")


def _bucket_lows():
    n = np.arange(0, 4 * MAX_DISTANCE)
    max_exact = NUM_BUCKETS // 2
    nf = np.maximum(n, 1).astype(np.float32)
    large = max_exact + (np.log(nf / max_exact) / math.log(MAX_DISTANCE / max_exact)
                         * (NUM_BUCKETS - max_exact)).astype(np.int32)
    bucket = np.where(n < max_exact, n, np.minimum(large, NUM_BUCKETS - 1))
    assert np.all(np.diff(bucket) >= 0) and bucket[-1] == NUM_BUCKETS - 1
    lows = [int(np.argmax(bucket == k)) for k in range(NUM_BUCKETS)]
    assert all(bucket[lo] == k for k, lo in enumerate(lows))
    return tuple(lows)


BUCKET_LOWS = _bucket_lows()
FAR_DIST = BUCKET_LOWS[-1]


def _params(sem):
    return pltpu.CompilerParams(dimension_semantics=sem, vmem_limit_bytes=VMEM_LIMIT)


def _dot(a, b):
    return jnp.dot(a, b, preferred_element_type=F32)


def _dot_nt(a, b):
    return lax.dot_general(a, b, (((1,), (1,)), ((), ())), preferred_element_type=F32)


def _split_bf16(a):
    hi = a.astype(BF16)
    lo = (a - hi.astype(F32)).astype(BF16)
    return hi, lo


def _dot3(a, b, nt=False):
    ah, al = _split_bf16(a)
    bh, bl = _split_bf16(b)
    d = _dot_nt if nt else _dot
    return d(ah, bh) + (d(ah, bl) + d(al, bh))


def _iota(shape, dim):
    return lax.broadcasted_iota(I32, shape, dim)


def _layer_norm(z, g, b):
    mu = jnp.mean(z, axis=-1, keepdims=True)
    zc = z - mu
    var = jnp.mean(zc * zc, axis=-1, keepdims=True)
    return zc * lax.rsqrt(var + LN_EPS) * g + b


def _bias_by_distance(dist, bias_of_bucket):
    acc = jnp.zeros(dist.shape, F32) + bias_of_bucket(0)
    for k in range(1, NUM_BUCKETS):
        acc = jnp.where(dist >= BUCKET_LOWS[k], bias_of_bucket(k), acc)
    return acc


def _sort_key(x):
    bits = lax.bitcast_convert_type(x + 0.0, I32)
    return bits ^ (lax.shift_right_arithmetic(bits, 31) & 0x7FFFFFFF)


_ATTN_OFFS = tuple(int(v) for v in np.cumsum((0,) + ATTN_COLS[:-1]))
_ATTN_IN = sum(ATTN_COLS)
_ATTN_IN_PAD = -(-_ATTN_IN // LANES) * LANES
_TAIL_OFF = _ATTN_OFFS[7]


def _attn_inproj_kernel(x_ref, w_ref, qa, ka, va, qb, kb, vb, qi, ki, wi):
    xb = x_ref[...].astype(BF16)
    for o, off, wd in zip((qa, ka, va, qb, kb, vb, qi), _ATTN_OFFS, ATTN_COLS):
        o[...] = _dot(xb, w_ref[:, off:off + wd])
    tail = _dot(xb, w_ref[:, _TAIL_OFF:_ATTN_IN_PAD])
    ki[...] = tail[:, :D_IDX]
    wi[...] = tail[:, D_IDX:D_IDX + H_IDX]


def _attn_inproj(x2, w_pad):
    n, d = x2.shape
    tm = min(ROW_TILE, n)
    outs = tuple(jax.ShapeDtypeStruct((n, c), F32) for c in ATTN_COLS)
    return pl.pallas_call(
        _attn_inproj_kernel,
        grid=(n // tm,),
        in_specs=[pl.BlockSpec((tm, d), lambda i: (i, 0)),
                  pl.BlockSpec((d, _ATTN_IN_PAD), lambda i: (0, 0))],
        out_specs=tuple(pl.BlockSpec((tm, c), lambda i: (i, 0)) for c in ATTN_COLS),
        out_shape=outs,
        compiler_params=_params(("arbitrary",)),
        name="attn_inproj",
    )(x2, w_pad)


def _build_bias_tiles(bias_ref, scr, head0, n_heads):
    row = _iota((ATT_TILE, ATT_TILE), 0)
    col = _iota((ATT_TILE, ATT_TILE), 1)
    for t in range(2):
        dist = row - col + (ATT_TILE if t == 0 else 0)
        for h in range(n_heads):
            scr[h, t] = _bias_by_distance(dist, lambda k, h=h: bias_ref[k, head0 + h])


def _online_update(s, v, m, l, acc):
    m_new = jnp.maximum(m, jnp.max(s, axis=1, keepdims=True))
    m_safe = jnp.where(m_new == NEG_INF, 0.0, m_new)
    alpha = jnp.exp(m - m_safe)
    p = jnp.exp(s - m_safe)
    l = alpha * l + jnp.sum(p, axis=1, keepdims=True)
    acc = alpha * acc + _dot(p.astype(BF16), v)
    return m_new, l, acc


def _moba_prompt_kernel(bias_ref, q_ref, k_ref, v_ref, o_ref, kmean_scr, bias_scr, *, nblk):
    b, hp, i = pl.program_id(0), pl.program_id(1), pl.program_id(2)
    heads_per_step = LANES // HEAD_DIM

    @pl.when((b == 0) & (hp == 0) & (i == 0))
    def _():
        _build_bias_tiles(bias_ref, bias_scr, 0, H_A)

    @pl.when(i == 0)
    def _():
        for n in range(nblk):
            kmean_scr[n:n + 1, :] = jnp.mean(
                k_ref[n * MOBA_BLOCK:(n + 1) * MOBA_BLOCK, :], axis=0, keepdims=True)

    q = q_ref[...] * (1.0 / math.sqrt(HEAD_DIM))
    row = _iota((ATT_TILE, ATT_TILE), 0)
    col = _iota((ATT_TILE, ATT_TILE), 1)
    blk_lane = _iota((ATT_TILE, nblk), 1).astype(F32)
    outs = []
    for hh in range(heads_per_step):
        h = hp * heads_per_step + hh
        cs = slice(hh * HEAD_DIM, (hh + 1) * HEAD_DIM)
        qh = q[:, cs]
        gate = _dot3(qh, kmean_scr[:, cs], nt=True)
        gate = jnp.where(blk_lane < i.astype(F32), gate, NEG_INF)
        selmask = jnp.zeros((ATT_TILE, nblk), F32)
        for _ in range(min(MOBA_TOPK, nblk)):
            mx = jnp.max(gate, axis=1, keepdims=True)
            first = jnp.min(jnp.where(gate == mx, blk_lane, float(nblk)), axis=1, keepdims=True)
            hit = blk_lane == first
            selmask = jnp.where(hit & (mx > NEG_INF), 1.0, selmask)
            gate = jnp.where(hit, NEG_INF, gate)

        qb = qh.astype(BF16)
        far_bias = bias_ref[NUM_BUCKETS - 1, h]
        start = pl.multiple_of(i * ATT_TILE, ATT_TILE)
        s = _dot_nt(qb, k_ref[pl.ds(start, ATT_TILE), cs].astype(BF16)) + bias_scr[h, 1]
        s = jnp.where(col <= row, s, NEG_INF)
        m0 = jnp.max(s, axis=1, keepdims=True)
        p = jnp.exp(s - m0)
        l0 = jnp.sum(p, axis=1, keepdims=True)
        acc0 = _dot(p.astype(BF16), v_ref[pl.ds(start, ATT_TILE), cs].astype(BF16))

        def body(j, carry, qb=qb, cs=cs, h=h, selmask=selmask, far_bias=far_bias):
            m, l, acc = carry
            ks = pl.multiple_of(j * ATT_TILE, ATT_TILE)
            s = _dot_nt(qb, k_ref[pl.ds(ks, ATT_TILE), cs].astype(BF16))
            s = s + jnp.where(j == i - 1, bias_scr[h, 0], far_bias)
            sel = jnp.sum(jnp.where(blk_lane == j.astype(F32), selmask, 0.0), axis=1, keepdims=True)
            s = jnp.where(sel > 0.0, s, NEG_INF)
            return _online_update(s, v_ref[pl.ds(ks, ATT_TILE), cs].astype(BF16), m, l, acc)

        _, l, acc = lax.fori_loop(0, i, body, (m0, l0, acc0))
        outs.append(acc / l)
    o_ref[...] = jnp.concatenate(outs, axis=1)


def _moba_prompt(q2, k2, v2, rel_bias, batch, seq):
    nq = seq // ATT_TILE
    nblk = seq // MOBA_BLOCK
    width = q2.shape[1]
    return pl.pallas_call(
        functools.partial(_moba_prompt_kernel, nblk=nblk),
        grid=(batch, width // LANES, nq),
        in_specs=[pl.BlockSpec(memory_space=pltpu.SMEM),
                  pl.BlockSpec((ATT_TILE, LANES), lambda b, hp, i: (b * nq + i, hp)),
                  pl.BlockSpec((seq, LANES), lambda b, hp, i: (b, hp)),
                  pl.BlockSpec((seq, LANES), lambda b, hp, i: (b, hp))],
        out_specs=pl.BlockSpec((ATT_TILE, LANES), lambda b, hp, i: (b * nq + i, hp)),
        out_shape=jax.ShapeDtypeStruct((batch * seq, width), F32),
        scratch_shapes=[pltpu.VMEM((nblk, LANES), F32),
                        pltpu.VMEM((H_A, 2, ATT_TILE, ATT_TILE), F32)],
        compiler_params=_params(("arbitrary", "arbitrary", "arbitrary")),
        name="moba_prompt",
    )(rel_bias, q2, k2, v2)


INT_MIN = -2 ** 31


def _strict_upper(n):
    return jnp.where(_iota((n, n), 0) < _iota((n, n), 1), 1.0, 0.0).astype(BF16)


def _select_tile(key, thr, need, carry, tri):
    eqf = jnp.where(key == thr, 1.0, 0.0)
    prefix = _dot(eqf.astype(BF16), tri) + carry
    self_ = jnp.where(key > thr, 1.0, jnp.where(prefix < need, eqf, 0.0))
    return self_, carry + jnp.sum(eqf, axis=1, keepdims=True)


def _kth_largest_key(count_ge, rows, k_sel):
    kk = float(k_sel)
    zero = jnp.zeros((rows, 1), I32)
    thr = jnp.where(count_ge(zero) >= kk, zero, jnp.full((rows, 1), INT_MIN, I32))

    def bit_body(t, thr):
        cand = thr | lax.shift_left(jnp.int32(1), 30 - t)
        return jnp.where(count_ge(cand) >= kk, cand, thr)

    return lax.fori_loop(0, 31, bit_body, thr)


def _dsa_prompt_kernel(bias_ref, qb_ref, qi_ref, wi_ref, kb_ref, vb_ref, ki_ref, o_ref,
                       key_scr, bias_scr, m_scr, l_scr, acc_scr, *, k_sel):
    b, i = pl.program_id(0), pl.program_id(1)

    @pl.when((b == 0) & (i == 0))
    def _():
        _build_bias_tiles(bias_ref, bias_scr, H_A, H_B)

    row = _iota((ATT_TILE, ATT_TILE), 0)
    col = _iota((ATT_TILE, ATT_TILE), 1)
    idx_scale = 1.0 / math.sqrt(D_IDX * H_IDX)

    qi = qi_ref[...]
    qi_h = [_split_bf16(qi[:, h * D_IDX:(h + 1) * D_IDX]) for h in range(H_IDX)]
    wi = wi_ref[...]
    w_h = [wi[:, h:h + 1] for h in range(H_IDX)]

    def score_body(j, _):
        ks = pl.multiple_of(j * ATT_TILE, ATT_TILE)
        kt_hi, kt_lo = _split_bf16(ki_ref[pl.ds(ks, ATT_TILE), :])
        acc = jnp.zeros((ATT_TILE, ATT_TILE), F32)
        for h in range(H_IDX):
            q_hi, q_lo = qi_h[h]
            s = _dot_nt(q_hi, kt_hi) + (_dot_nt(q_hi, kt_lo) + _dot_nt(q_lo, kt_hi))
            acc = acc + jnp.maximum(s, 0.0) * w_h[h]
        score = acc * idx_scale
        causal = (col + (j - i) * ATT_TILE) <= row
        key_scr[j] = _sort_key(jnp.where(causal, score, NEG_INF))
        return 0

    lax.fori_loop(0, i + 1, score_body, 0)

    def count_ge(cand):
        def body(j, cnt):
            return cnt + jnp.sum(jnp.where(key_scr[j] >= cand, 1.0, 0.0), axis=1, keepdims=True)
        return lax.fori_loop(0, i + 1, body, jnp.zeros((ATT_TILE, 1), F32))

    thr = _kth_largest_key(count_ge, ATT_TILE, k_sel)

    def count_gt_body(j, cnt):
        return cnt + jnp.sum(jnp.where(key_scr[j] > thr, 1.0, 0.0), axis=1, keepdims=True)

    need = float(k_sel) - lax.fori_loop(0, i + 1, count_gt_body, jnp.zeros((ATT_TILE, 1), F32))

    q = qb_ref[...] * (1.0 / math.sqrt(HEAD_DIM))
    q_h = [q[:, h * HEAD_DIM:(h + 1) * HEAD_DIM].astype(BF16) for h in range(H_B)]
    tri = _strict_upper(ATT_TILE)
    m_scr[...] = jnp.full(m_scr.shape, NEG_INF, F32)
    l_scr[...] = jnp.zeros(l_scr.shape, F32)
    acc_scr[...] = jnp.zeros(acc_scr.shape, F32)

    def attend_body(j, carry):
        ks = pl.multiple_of(j * ATT_TILE, ATT_TILE)
        self_, carry = _select_tile(key_scr[j], thr, need, carry, tri)
        causal = (col + (j - i) * ATT_TILE) <= row
        keep = jnp.where(causal, self_, 0.0) > 0.0
        near = jnp.maximum(j - (i - 1), 0)
        for n in range(KV_B):
            cs = slice(n * HEAD_DIM, (n + 1) * HEAD_DIM)
            kt = kb_ref[pl.ds(ks, ATT_TILE), cs].astype(BF16)
            vt = vb_ref[pl.ds(ks, ATT_TILE), cs].astype(BF16)
            for g in range(G_B):
                h = n * G_B + g
                bias = jnp.where(j >= i - 1, bias_scr[h, near], bias_ref[NUM_BUCKETS - 1, H_A + h])
                s = jnp.where(keep, _dot_nt(q_h[h], kt) + bias, NEG_INF)
                m, l, acc = _online_update(s, vt, m_scr[h], l_scr[h], acc_scr[h])
                m_scr[h] = m
                l_scr[h] = l
                acc_scr[h] = acc
        return carry

    lax.fori_loop(0, i + 1, attend_body, jnp.zeros((ATT_TILE, 1), F32))
    o_ref[...] = jnp.concatenate([acc_scr[h] / l_scr[h] for h in range(H_B)], axis=1)


def _dsa_prompt(qb2, qi2, wi2, kb2, vb2, ki2, rel_bias, batch, seq):
    nq = seq // ATT_TILE
    k_sel = min(DSA_TOPK, seq // 4)
    qmap = lambda b, i: (b * nq + i, 0)
    kmap = lambda b, i: (b, 0)
    return pl.pallas_call(
        functools.partial(_dsa_prompt_kernel, k_sel=k_sel),
        grid=(batch, nq),
        in_specs=[pl.BlockSpec(memory_space=pltpu.SMEM),
                  pl.BlockSpec((ATT_TILE, H_B * HEAD_DIM), qmap),
                  pl.BlockSpec((ATT_TILE, H_IDX * D_IDX), qmap),
                  pl.BlockSpec((ATT_TILE, H_IDX), qmap),
                  pl.BlockSpec((seq, KV_B * HEAD_DIM), kmap),
                  pl.BlockSpec((seq, KV_B * HEAD_DIM), kmap),
                  pl.BlockSpec((seq, D_IDX), kmap)],
        out_specs=pl.BlockSpec((ATT_TILE, H_B * HEAD_DIM), qmap),
        out_shape=jax.ShapeDtypeStruct((batch * seq, H_B * HEAD_DIM), F32),
        scratch_shapes=[pltpu.VMEM((nq, ATT_TILE, ATT_TILE), I32),
                        pltpu.VMEM((H_B, 2, ATT_TILE, ATT_TILE), F32),
                        pltpu.VMEM((H_B, ATT_TILE, 1), F32),
                        pltpu.VMEM((H_B, ATT_TILE, 1), F32),
                        pltpu.VMEM((H_B, ATT_TILE, HEAD_DIM), F32)],
        compiler_params=_params(("arbitrary", "arbitrary")),
        name="dsa_prompt",
    )(rel_bias, qb2, qi2, wi2, kb2, vb2, ki2)


N_CACHES = 5
TAIL_ROWS = LANES


def _softmax_rows(s):
    m = jnp.max(s, axis=1, keepdims=True)
    p = jnp.exp(s - m)
    return p / jnp.sum(p, axis=1, keepdims=True)


def _stack_heads(x, n_heads, width):
    return jnp.concatenate([x[:, h * width:(h + 1) * width] for h in range(n_heads)], axis=0)


def _sample_attn_kernel(pt_ref, bias_a_ref, bias_b_ref, qa_ref, ka_ref, va_ref, qb_ref, kb_ref,
                        vb_ref, qi_ref, ki_ref, wi_ref, cka, cva, ckb, cvb, cki,
                        ya_ref, yb_ref, ka_buf, va_buf, kb_buf, vb_buf, ki_buf, sem,
                        bias_a_scr, bias_b_scr, *, n_seq, n_pages, page, ts):
    b = pl.program_id(0)
    slot = b % 2
    past = n_pages * page
    lp = past + TAIL_ROWS
    rows = H_A * ts
    caches = (cka, cva, ckb, cvb, cki)
    bufs = (ka_buf, va_buf, kb_buf, vb_buf, ki_buf)

    def page_copies(seq, slot_):
        cps = []
        for p in range(n_pages):
            pg = pt_ref[seq, p]
            for c in range(N_CACHES):
                cps.append(pltpu.make_async_copy(
                    caches[c].at[pg], bufs[c].at[slot_, pl.ds(p * page, page)], sem.at[slot_, c]))
        return cps

    @pl.when(b == 0)
    def _():
        for cp in page_copies(0, 0):
            cp.start()
        dist = past + _iota((rows, lp), 0) % ts - _iota((rows, lp), 1)
        bias_a_scr[...] = _bias_by_distance(dist, lambda k: bias_a_ref[:, k:k + 1])
        bias_b_scr[...] = _bias_by_distance(dist, lambda k: bias_b_ref[:, k:k + 1])

    @pl.when(b + 1 < n_seq)
    def _():
        for cp in page_copies(b + 1, 1 - slot):
            cp.start()

    for buf, new in zip(bufs, (ka_ref, va_ref, kb_ref, vb_ref, ki_ref)):
        buf[slot, past:lp, :] = jnp.zeros((TAIL_ROWS, buf.shape[2]), F32)
        buf[slot, past:past + ts, :] = new[0]

    for cp in page_copies(b, slot):
        cp.wait()

    colpos = _iota((ts, lp), 1)
    qpos = _iota((ts, lp), 0)
    valid = colpos <= past + qpos
    valid_rows = jnp.where(_iota((rows, lp), 1) <= past + _iota((rows, lp), 0) % ts, 1.0, 0.0)
    row_head = _iota((rows, 1), 0) // ts

    n_past_blk = past // MOBA_BLOCK
    qa = qa_ref[0] * (1.0 / math.sqrt(HEAD_DIM))
    width_a = H_A * HEAD_DIM
    q_rows = jnp.concatenate([qa] * H_A, axis=0)
    own_head_a = (_iota((rows, width_a), 1) // HEAD_DIM) == row_head
    q_bd = jnp.where(own_head_a, q_rows, 0.0)
    kmean = jnp.concatenate(
        [jnp.mean(ka_buf[slot, n * MOBA_BLOCK:(n + 1) * MOBA_BLOCK, :], axis=0, keepdims=True)
         for n in range(n_past_blk)], axis=0)
    gate = _dot3(q_bd, kmean, nt=True)
    blk_lane = _iota((rows, n_past_blk), 1).astype(F32)
    selmask = jnp.zeros((rows, n_past_blk), F32)
    for _ in range(min(MOBA_TOPK, n_past_blk)):
        mx = jnp.max(gate, axis=1, keepdims=True)
        first = jnp.min(jnp.where(gate == mx, blk_lane, float(n_past_blk)), axis=1, keepdims=True)
        hit = blk_lane == first
        selmask = jnp.where(hit, 1.0, selmask)
        gate = jnp.where(hit, NEG_INF, gate)
    expand = jnp.where(_iota((n_past_blk, lp), 1) // MOBA_BLOCK == _iota((n_past_blk, lp), 0),
                       1.0, 0.0).astype(BF16)
    in_sel_blk = _dot(selmask.astype(BF16), expand)
    keep_a = jnp.where(_iota((rows, lp), 1) >= past, valid_rows, in_sel_blk) > 0.0
    s = _dot_nt(q_bd.astype(BF16), ka_buf[slot].astype(BF16)) + bias_a_scr[...]
    p = _softmax_rows(jnp.where(keep_a, s, NEG_INF))
    o = jnp.where(own_head_a, _dot(p.astype(BF16), va_buf[slot].astype(BF16)), 0.0)
    ya = o[0:ts]
    for h in range(1, H_A):
        ya = ya + o[h * ts:(h + 1) * ts]
    ya_ref[0] = ya

    k_sel = min(DSA_TOPK, (past + ts) // 4)
    qi_rows = _stack_heads(qi_ref[0], H_IDX, D_IDX)
    wi = wi_ref[0]
    w_rows = jnp.concatenate([wi[:, h:h + 1] for h in range(H_IDX)], axis=0)
    contrib = jnp.maximum(_dot3(qi_rows, ki_buf[slot], nt=True), 0.0) * w_rows
    score = contrib[0:ts]
    for h in range(1, H_IDX):
        score = score + contrib[h * ts:(h + 1) * ts]
    score = jnp.where(valid, score * (1.0 / math.sqrt(D_IDX * H_IDX)), NEG_INF)
    key = _sort_key(score)

    def count_ge(cand):
        return jnp.sum(jnp.where(key >= cand, 1.0, 0.0), axis=1, keepdims=True)

    thr = _kth_largest_key(count_ge, ts, k_sel)
    need = float(k_sel) - jnp.sum(jnp.where(key > thr, 1.0, 0.0), axis=1, keepdims=True)
    tri = _strict_upper(LANES)
    carry = jnp.zeros((ts, 1), F32)
    sel_chunks = []
    for c in range(lp // LANES):
        sel_c, carry = _select_tile(key[:, c * LANES:(c + 1) * LANES], thr, need, carry, tri)
        sel_chunks.append(sel_c)
    keep = jnp.where(valid, jnp.concatenate(sel_chunks, axis=1), 0.0)
    keep_b = jnp.concatenate([keep] * H_B, axis=0) > 0.0

    qb = qb_ref[0] * (1.0 / math.sqrt(HEAD_DIM))
    qb_rows = _stack_heads(qb, H_B, HEAD_DIM)
    width_b = KV_B * HEAD_DIM
    own_kv = (_iota((rows, width_b), 1) // HEAD_DIM) == row_head // G_B
    qb_bd = jnp.where(own_kv, jnp.concatenate([qb_rows] * KV_B, axis=1), 0.0)
    s = _dot_nt(qb_bd.astype(BF16), kb_buf[slot].astype(BF16)) + bias_b_scr[...]
    p = _softmax_rows(jnp.where(keep_b, s, NEG_INF))
    o = _dot(p.astype(BF16), vb_buf[slot].astype(BF16))
    pieces = []
    for h in range(H_B):
        n = h // G_B
        pieces.append(o[h * ts:(h + 1) * ts, n * HEAD_DIM:(n + 1) * HEAD_DIM])
    yb_ref[0] = jnp.concatenate(pieces, axis=1)


def _sample_attn(new_rows, caches, page_table, rel_bias, n_seq, ts):
    n_pages = page_table.shape[1]
    page = caches[0].shape[1]
    past = n_pages * page
    assert past % MOBA_BLOCK == 0 and ts <= SUBLANES
    lp = past + TAIL_ROWS
    rows = H_A * ts
    bias_a_rows = jnp.repeat(rel_bias[:, :H_A].T, ts, axis=0)
    bias_b_rows = jnp.repeat(rel_bias[:, H_A:].T, ts, axis=0)
    full = lambda a: pl.BlockSpec(a.shape, lambda b, pt: (0, 0))
    seq_spec = lambda a: pl.BlockSpec((1, ts, a.shape[2]), lambda b, pt: (b, 0, 0))
    grid_spec = pltpu.PrefetchScalarGridSpec(
        num_scalar_prefetch=1,
        grid=(n_seq,),
        in_specs=[full(bias_a_rows), full(bias_b_rows)] + [seq_spec(a) for a in new_rows]
                 + [pl.BlockSpec(memory_space=pl.ANY)] * N_CACHES,
        out_specs=[pl.BlockSpec((1, ts, H_A * HEAD_DIM), lambda b, pt: (b, 0, 0)),
                   pl.BlockSpec((1, ts, H_B * HEAD_DIM), lambda b, pt: (b, 0, 0))],
        scratch_shapes=[pltpu.VMEM((2, lp, c.shape[2]), F32) for c in caches]
                       + [pltpu.SemaphoreType.DMA((2, N_CACHES)),
                          pltpu.VMEM((rows, lp), F32), pltpu.VMEM((rows, lp), F32)],
    )
    return pl.pallas_call(
        functools.partial(_sample_attn_kernel, n_seq=n_seq, n_pages=n_pages, page=page, ts=ts),
        grid_spec=grid_spec,
        out_shape=[jax.ShapeDtypeStruct((n_seq, ts, H_A * HEAD_DIM), F32),
                   jax.ShapeDtypeStruct((n_seq, ts, H_B * HEAD_DIM), F32)],
        compiler_params=_params(("arbitrary",)),
        name="sample_attn",
    )(page_table, bias_a_rows, bias_b_rows, *new_rows, *caches)


def _attn_out_kernel(ya_ref, yb_ref, x_ref, w_ref, g_ref, b_ref, o_ref, *, alpha):
    wa = ya_ref.shape[1]
    y = (_dot(ya_ref[...].astype(BF16), w_ref[0:wa, :])
         + _dot(yb_ref[...].astype(BF16), w_ref[wa:, :]))
    o_ref[...] = _layer_norm(alpha * x_ref[...] + y, g_ref[...], b_ref[...])


def _attn_out(ya2, yb2, x2, w_out, g, b, alpha):
    n, d = x2.shape
    tm = min(ROW_TILE, n)
    row = lambda c: pl.BlockSpec((tm, c), lambda i: (i, 0))
    full = lambda a: pl.BlockSpec(a.shape, lambda i: (0, 0))
    return pl.pallas_call(
        functools.partial(_attn_out_kernel, alpha=alpha),
        grid=(n // tm,),
        in_specs=[row(ya2.shape[1]), row(yb2.shape[1]), row(d), full(w_out), full(g), full(b)],
        out_specs=row(d),
        out_shape=jax.ShapeDtypeStruct((n, d), F32),
        compiler_params=_params(("arbitrary",)),
        name="attn_out_ln",
    )(ya2, yb2, x2, w_out, g, b)


def _conv_core(x, u, u1, u2, gate_b, cw_ref, wout_ref, g_ref, b_ref, alpha):
    c = u2 * cw_ref[0:1, :] + u1 * cw_ref[1:2, :] + u * cw_ref[2:3, :]
    y = _dot((gate_b * c).astype(BF16), wout_ref[...])
    return _layer_norm(alpha * x + y, g_ref[...], b_ref[...])


def _conv_in(x, win_ref, dc):
    xb = x.astype(BF16)
    h = _dot(xb, win_ref[:, 0:dc])
    gate_b = _dot(xb, win_ref[:, dc:2 * dc])
    gate_c = _dot(xb, win_ref[:, 2 * dc:3 * dc])
    return gate_c * h, gate_b


def _conv_prompt_kernel(x_ref, win_ref, cw_ref, wout_ref, g_ref, b_ref, o_ref, st_ref, tail_scr,
                        *, alpha):
    tm, dc = x_ref.shape[0], wout_ref.shape[0]

    @pl.when(pl.program_id(1) == 0)
    def _():
        tail_scr[...] = jnp.zeros(tail_scr.shape, F32)

    x = x_ref[...]
    u, gate_b = _conv_in(x, win_ref, dc)
    ext = jnp.concatenate([tail_scr[...], u], axis=0)
    u1 = ext[SUBLANES - 1:SUBLANES - 1 + tm]
    u2 = ext[SUBLANES - 2:SUBLANES - 2 + tm]
    o_ref[...] = _conv_core(x, u, u1, u2, gate_b, cw_ref, wout_ref, g_ref, b_ref, alpha)
    tail_scr[...] = u[tm - SUBLANES:tm]
    st_ref[0] = u[tm - (CONV_W - 1):tm]


def _conv_prompt(x2, w_in, conv_w, w_out, g, b, alpha, batch, seq):
    n, d = x2.shape
    dc = w_out.shape[0]
    tm = min(ROW_TILE, seq)
    nt = seq // tm
    full = lambda a: pl.BlockSpec(a.shape, lambda bb, i: (0, 0))
    return pl.pallas_call(
        functools.partial(_conv_prompt_kernel, alpha=alpha),
        grid=(batch, nt),
        in_specs=[pl.BlockSpec((tm, d), lambda bb, i: (bb * nt + i, 0)),
                  full(w_in), full(conv_w), full(w_out), full(g), full(b)],
        out_specs=[pl.BlockSpec((tm, d), lambda bb, i: (bb * nt + i, 0)),
                   pl.BlockSpec((1, CONV_W - 1, dc), lambda bb, i: (bb, 0, 0))],
        out_shape=[jax.ShapeDtypeStruct((n, d), F32),
                   jax.ShapeDtypeStruct((batch, CONV_W - 1, dc), F32)],
        scratch_shapes=[pltpu.VMEM((SUBLANES, dc), F32)],
        compiler_params=_params(("arbitrary", "arbitrary")),
        name="conv_prompt_ln",
    )(x2, w_in, conv_w, w_out, g, b)


def _conv_sample_kernel(x_ref, p0_ref, p1_ref, win_ref, cw_ref, wout_ref, g_ref, b_ref,
                        o_ref, u_ref, *, alpha, ts):
    n, dc = x_ref.shape[0], wout_ref.shape[0]
    x = x_ref[...]
    u, gate_b = _conv_in(x, win_ref, dc)
    ext = jnp.concatenate([jnp.zeros((SUBLANES, dc), F32), u], axis=0)
    tpos = _iota((n, dc), 0) % ts
    u1 = jnp.where(tpos == 0, p1_ref[...], ext[SUBLANES - 1:SUBLANES - 1 + n])
    u2 = jnp.where(tpos == 0, p0_ref[...],
                   jnp.where(tpos == 1, p1_ref[...], ext[SUBLANES - 2:SUBLANES - 2 + n]))
    o_ref[...] = _conv_core(x, u, u1, u2, gate_b, cw_ref, wout_ref, g_ref, b_ref, alpha)
    u_ref[...] = u


def _conv_sample(x2, past0, past1, w_in, conv_w, w_out, g, b, alpha, ts):
    n, d = x2.shape
    dc = w_out.shape[0]
    tm = min(ROW_TILE, n)
    assert tm % ts == 0
    full = lambda a: pl.BlockSpec(a.shape, lambda i: (0, 0))
    row = lambda c: pl.BlockSpec((tm, c), lambda i: (i, 0))
    return pl.pallas_call(
        functools.partial(_conv_sample_kernel, alpha=alpha, ts=ts),
        grid=(n // tm,),
        in_specs=[row(d), row(dc), row(dc), full(w_in), full(conv_w), full(w_out), full(g), full(b)],
        out_specs=[row(d), row(dc)],
        out_shape=[jax.ShapeDtypeStruct((n, d), F32), jax.ShapeDtypeStruct((n, dc), F32)],
        compiler_params=_params(("arbitrary",)),
        name="conv_sample_ln",
    )(x2, past0, past1, w_in, conv_w, w_out, g, b)


GROUP_LANE0 = 0
EXPERT_LANE0 = 32


def _first_argmax(v, lane_f):
    mx = jnp.max(v, axis=1, keepdims=True)
    first = jnp.min(jnp.where(v == mx, lane_f, float(LANES)), axis=1, keepdims=True)
    return mx, first


def _router_kernel(x_ref, w_ref, bias_ref, cin_ref, e0_ref, e1_ref, g0_ref, g1_ref, r0_ref, r1_ref,
                   cnt_ref, carry_scr):
    tm = x_ref.shape[0]

    @pl.when(pl.program_id(0) == 0)
    def _():
        carry_scr[...] = cin_ref[...]

    logits = _dot3(x_ref[...], w_ref[...]) + bias_ref[...]
    lane = _iota((tm, LANES), 1)
    lane_f = lane.astype(F32)
    gl = jnp.where((lane >= GROUP_LANE0) & (lane < GROUP_LANE0 + N_GROUPS), logits, NEG_INF)
    gmax, gfirst = _first_argmax(gl, lane_f)
    pg = jnp.exp(gl - gmax)
    g_val = jnp.max(pg / jnp.sum(pg, axis=1, keepdims=True), axis=1, keepdims=True)
    g_idx = gfirst - float(GROUP_LANE0)
    grp_of_lane = ((lane - EXPERT_LANE0) // EXPERTS_PER_GROUP).astype(F32)
    in_grp = (lane >= EXPERT_LANE0) & (lane < EXPERT_LANE0 + N_EXPERTS) & (grp_of_lane == g_idx)
    el = jnp.where(in_grp, logits, NEG_INF)
    emax, first0 = _first_argmax(el, lane_f)
    pe = jnp.exp(el - emax)
    pe = pe / jnp.sum(pe, axis=1, keepdims=True)
    hit0 = lane_f == first0
    v0 = jnp.max(jnp.where(hit0, pe, 0.0), axis=1, keepdims=True)
    _, first1 = _first_argmax(jnp.where(hit0, NEG_INF, el), lane_f)
    hit1 = lane_f == first1
    v1 = jnp.max(jnp.where(hit1, pe, 0.0), axis=1, keepdims=True)
    denom = v0 + v1
    g0_ref[...] = g_val * v0 / denom
    g1_ref[...] = g_val * v1 / denom
    e0_ref[...] = first0.astype(I32) - EXPERT_LANE0
    e1_ref[...] = first1.astype(I32) - EXPERT_LANE0
    onehot = jnp.where(hit0 | hit1, 1.0, 0.0)
    lower = jnp.where(_iota((tm, tm), 1) < _iota((tm, tm), 0), 1.0, 0.0).astype(BF16)
    before = _dot(lower, onehot.astype(BF16)) + carry_scr[...]
    r0_ref[...] = jnp.sum(jnp.where(hit0, before, 0.0), axis=1, keepdims=True).astype(I32)
    r1_ref[...] = jnp.sum(jnp.where(hit1, before, 0.0), axis=1, keepdims=True).astype(I32)
    carry_scr[...] = carry_scr[...] + jnp.sum(onehot, axis=0, keepdims=True)
    cnt_ref[...] = carry_scr[...]


def _router(x2, w_slab, b_slab, counts_in):
    n, d = x2.shape
    tm = min(ROW_TILE, n)
    full = lambda a: pl.BlockSpec(a.shape, lambda i: (0, 0))
    col = pl.BlockSpec((tm, 1), lambda i: (i, 0))
    return pl.pallas_call(
        _router_kernel,
        grid=(n // tm,),
        in_specs=[pl.BlockSpec((tm, d), lambda i: (i, 0)), full(w_slab), full(b_slab), full(counts_in)],
        out_specs=[col] * 6 + [pl.BlockSpec((1, LANES), lambda i: (0, 0))],
        out_shape=[jax.ShapeDtypeStruct((n, 1), I32), jax.ShapeDtypeStruct((n, 1), I32),
                   jax.ShapeDtypeStruct((n, 1), F32), jax.ShapeDtypeStruct((n, 1), F32),
                   jax.ShapeDtypeStruct((n, 1), I32), jax.ShapeDtypeStruct((n, 1), I32),
                   jax.ShapeDtypeStruct((1, LANES), F32)],
        scratch_shapes=[pltpu.VMEM((1, LANES), F32)],
        compiler_params=_params(("arbitrary",)),
        name="moe_router",
    )(x2, w_slab, b_slab, counts_in)


def _dispatch_kernel(e0_ref, e1_ref, r0_ref, r1_ref, start_ref, x_ref, xr_in, xr_out, sem):
    del xr_in
    tm = x_ref.shape[0]
    base = pl.program_id(0) * tm

    def copies(t):
        p0 = start_ref[e0_ref[base + t]] + r0_ref[base + t]
        p1 = start_ref[e1_ref[base + t]] + r1_ref[base + t]
        src = x_ref.at[pl.ds(t, 1)]
        return (pltpu.make_async_copy(src, xr_out.at[pl.ds(p0, 1)], sem),
                pltpu.make_async_copy(src, xr_out.at[pl.ds(p1, 1)], sem))

    def start(t, _):
        for cp in copies(t):
            cp.start()
        return 0

    def wait(t, _):
        for cp in copies(t):
            cp.wait()
        return 0

    lax.fori_loop(0, tm, start, 0)
    lax.fori_loop(0, tm, wait, 0)


def _dispatch(route, start_pad, x2, xr):
    n, d = x2.shape
    tm = min(ROW_TILE, n)
    grid_spec = pltpu.PrefetchScalarGridSpec(
        num_scalar_prefetch=5,
        grid=(n // tm,),
        in_specs=[pl.BlockSpec((tm, d), lambda i, *_: (i, 0)), pl.BlockSpec(memory_space=pl.ANY)],
        out_specs=pl.BlockSpec(memory_space=pl.ANY),
        scratch_shapes=[pltpu.SemaphoreType.DMA(())],
    )
    return pl.pallas_call(
        _dispatch_kernel,
        grid_spec=grid_spec,
        out_shape=jax.ShapeDtypeStruct(xr.shape, F32),
        input_output_aliases={6: 0},
        compiler_params=_params(("arbitrary",)),
        name="moe_dispatch",
    )(*route, start_pad, x2, xr)


def _ffn_kernel(be_ref, nu_ref, x_ref, wg_ref, wu_ref, wd_ref, o_ref, wg_s, wu_s, wd_s):
    i = pl.program_id(0)

    @pl.when(i < nu_ref[0])
    def _():
        @pl.when((i == 0) | (be_ref[i] != be_ref[jnp.maximum(i - 1, 0)]))
        def _():
            wg_s[...] = wg_ref[0].astype(BF16)
            wu_s[...] = wu_ref[0].astype(BF16)
            wd_s[...] = wd_ref[0].astype(BF16)

        xb = x_ref[...].astype(BF16)
        gate = _dot(xb, wg_s[...])
        hidden = gate * jax.nn.sigmoid(gate) * _dot(xb, wu_s[...])
        o_ref[...] = _dot(hidden.astype(BF16), wd_s[...])

    @pl.when(i >= nu_ref[0])
    def _():
        o_ref[...] = jnp.zeros(o_ref.shape, F32)


def _grouped_ffn(block_expert, n_used, xr, w_gate, w_up, w_down):
    rows, d = xr.shape
    de = w_gate.shape[2]
    nblk = rows // FFN_ROWS
    grid_spec = pltpu.PrefetchScalarGridSpec(
        num_scalar_prefetch=2,
        grid=(nblk,),
        in_specs=[pl.BlockSpec((FFN_ROWS, d), lambda i, be, nu: (jnp.minimum(i, nu[0] - 1), 0)),
                  pl.BlockSpec((1, d, de), lambda i, be, nu: (be[i], 0, 0)),
                  pl.BlockSpec((1, d, de), lambda i, be, nu: (be[i], 0, 0)),
                  pl.BlockSpec((1, de, d), lambda i, be, nu: (be[i], 0, 0))],
        out_specs=pl.BlockSpec((FFN_ROWS, d), lambda i, be, nu: (i, 0)),
        scratch_shapes=[pltpu.VMEM((d, de), BF16), pltpu.VMEM((d, de), BF16), pltpu.VMEM((de, d), BF16)],
    )
    return pl.pallas_call(
        _ffn_kernel,
        grid_spec=grid_spec,
        out_shape=jax.ShapeDtypeStruct((rows, d), F32),
        compiler_params=_params(("arbitrary",)),
        name="moe_ffn",
    )(block_expert, n_used, xr, w_gate, w_up, w_down)


def _combine_kernel(e0_ref, e1_ref, r0_ref, r1_ref, start_ref, x_ref, g0_ref, g1_ref, lg_ref, lb_ref,
                    yr_ref, o_ref, ybuf, sem, *, alpha):
    tm = x_ref.shape[0]
    base = pl.program_id(0) * tm

    def copies(t):
        p0 = start_ref[e0_ref[base + t]] + r0_ref[base + t]
        p1 = start_ref[e1_ref[base + t]] + r1_ref[base + t]
        return (pltpu.make_async_copy(yr_ref.at[pl.ds(p0, 1)], ybuf.at[0, pl.ds(t, 1)], sem),
                pltpu.make_async_copy(yr_ref.at[pl.ds(p1, 1)], ybuf.at[1, pl.ds(t, 1)], sem))

    def start(t, _):
        for cp in copies(t):
            cp.start()
        return 0

    def wait(t, _):
        for cp in copies(t):
            cp.wait()
        return 0

    lax.fori_loop(0, tm, start, 0)
    lax.fori_loop(0, tm, wait, 0)
    y = ybuf[0] * g0_ref[...] + ybuf[1] * g1_ref[...]
    o_ref[...] = _layer_norm(alpha * x_ref[...] + y, lg_ref[...], lb_ref[...])


def _combine(route, start_pad, x2, g0, g1, ln_g, ln_b, yr, alpha):
    n, d = x2.shape
    tm = min(ROW_TILE, n)
    col = pl.BlockSpec((tm, 1), lambda i, *_: (i, 0))
    vec = pl.BlockSpec((1, d), lambda i, *_: (0, 0))
    grid_spec = pltpu.PrefetchScalarGridSpec(
        num_scalar_prefetch=5,
        grid=(n // tm,),
        in_specs=[pl.BlockSpec((tm, d), lambda i, *_: (i, 0)), col, col, vec, vec,
                  pl.BlockSpec(memory_space=pl.ANY)],
        out_specs=pl.BlockSpec((tm, d), lambda i, *_: (i, 0)),
        scratch_shapes=[pltpu.VMEM((2, tm, d), F32), pltpu.SemaphoreType.DMA(())],
    )
    return pl.pallas_call(
        functools.partial(_combine_kernel, alpha=alpha),
        grid_spec=grid_spec,
        out_shape=jax.ShapeDtypeStruct((n, d), F32),
        compiler_params=_params(("arbitrary",)),
        name="moe_combine_ln",
    )(*route, start_pad, x2, g0, g1, ln_g, ln_b, yr)


def _hier_moe_ln(xs, w_rg, b_rg, w_re, b_re, w_gate, w_up, w_down, ln_g, ln_b, alpha):
    d = xs[0].shape[1]
    w_slab = jnp.zeros((d, LANES), F32)
    w_slab = w_slab.at[:, GROUP_LANE0:GROUP_LANE0 + N_GROUPS].set(w_rg)
    w_slab = w_slab.at[:, EXPERT_LANE0:EXPERT_LANE0 + N_EXPERTS].set(w_re)
    b_slab = jnp.zeros((1, LANES), F32)
    b_slab = b_slab.at[0, GROUP_LANE0:GROUP_LANE0 + N_GROUPS].set(b_rg)
    b_slab = b_slab.at[0, EXPERT_LANE0:EXPERT_LANE0 + N_EXPERTS].set(b_re)

    counts = jnp.zeros((1, LANES), F32)
    routes, gates = [], []
    for x2 in xs:
        e0, e1, g0, g1, r0, r1, counts = _router(x2, w_slab, b_slab, counts)
        routes.append(tuple(a.reshape(-1) for a in (e0, e1, r0, r1)))
        gates.append((g0, g1))

    cnt = counts[0, EXPERT_LANE0:EXPERT_LANE0 + N_EXPERTS].astype(I32)
    padded = (cnt + FFN_ROWS - 1) // FFN_ROWS * FFN_ROWS
    end_pad = jnp.cumsum(padded)
    start_pad = (end_pad - padded).astype(I32)
    n_assign = 2 * sum(x2.shape[0] for x2 in xs)
    nblk = -(-n_assign // FFN_ROWS) + N_EXPERTS
    block_expert = jnp.minimum(
        jnp.searchsorted(end_pad, jnp.arange(nblk, dtype=I32) * FFN_ROWS, side='right'),
        N_EXPERTS - 1).astype(I32)
    n_used = (end_pad[-1:] // FFN_ROWS).astype(I32)

    xr = jnp.zeros((nblk * FFN_ROWS, d), F32)
    for x2, route in zip(xs, routes):
        xr = _dispatch(route, start_pad, x2, xr)
    yr = _grouped_ffn(block_expert, n_used, xr, w_gate, w_up, w_down)
    return [_combine(route, start_pad, x2, g0, g1, ln_g, ln_b, yr, alpha)
            for x2, route, (g0, g1) in zip(xs, routes, gates)]


def _stack(arrays):
    return arrays[0][None] if len(arrays) == 1 else jnp.stack(arrays)


def kernel(x_prompt, x_sample, cache_k_a, cache_v_a, cache_k_b, cache_v_b, cache_k_idx, state_conv, page_table, rel_bias, w_attn_in, w_attn_out, w_conv_in, conv_w, w_conv_out, w_router_group, b_router_group, w_router_expert, b_router_expert, w_exp_gate, w_exp_up, w_exp_down, ln_g, ln_b):
    depth = ln_g.shape[0]
    alpha = float((2 * depth) ** 0.25)
    bp, tp, d = x_prompt.shape
    bs, ts, _ = x_sample.shape
    assert ts >= CONV_W - 1
    xp = x_prompt.reshape(bp * tp, d)
    xs = x_sample.reshape(bs * ts, d)
    attn_p, attn_s, conv_p, conv_s = [], [], [], []
    kv_cols = (1, 2, 4, 5, 7)
    for layer in range(depth):
        i = layer // 2
        g_mix, b_mix = ln_g[layer, 0][None], ln_b[layer, 0][None]
        if layer % 2 == 0:
            w_in = jnp.pad(w_attn_in[i], ((0, 0), (0, _ATTN_IN_PAD - _ATTN_IN))).astype(BF16)
            w_out = w_attn_out[i].astype(BF16)
            proj_p = _attn_inproj(xp, w_in)
            proj_s = _attn_inproj(xs, w_in)
            qa, ka, va, qb, kb, vb, qi, ki, wi = proj_p
            ya_p = _moba_prompt(qa, ka, va, rel_bias, bp, tp)
            yb_p = _dsa_prompt(qb, qi, wi, kb, vb, ki, rel_bias, bp, tp)
            caches = [c[i].reshape(c.shape[1], c.shape[2], -1)
                      for c in (cache_k_a, cache_v_a, cache_k_b, cache_v_b, cache_k_idx)]
            ya_s, yb_s = _sample_attn([a.reshape(bs, ts, -1) for a in proj_s], caches,
                                      page_table, rel_bias, bs, ts)
            xp = _attn_out(ya_p, yb_p, xp, w_out, g_mix, b_mix, alpha)
            xs = _attn_out(ya_s.reshape(bs * ts, -1), yb_s.reshape(bs * ts, -1), xs, w_out,
                           g_mix, b_mix, alpha)
            attn_p.append([proj_p[c] for c in kv_cols])
            attn_s.append([proj_s[c] for c in kv_cols])
        else:
            w_in = w_conv_in[i].astype(BF16)
            w_out = w_conv_out[i].astype(BF16)
            xp, st_p = _conv_prompt(xp, w_in, conv_w[i], w_out, g_mix, b_mix, alpha, bp, tp)
            past = state_conv[i]
            xs, u_s = _conv_sample(xs, jnp.repeat(past[:, 0], ts, axis=0),
                                   jnp.repeat(past[:, 1], ts, axis=0),
                                   w_in, conv_w[i], w_out, g_mix, b_mix, alpha, ts)
            conv_p.append(st_p)
            conv_s.append(u_s.reshape(bs, ts, -1)[:, ts - (CONV_W - 1):])
        xp, xs = _hier_moe_ln([xp, xs], w_router_group[layer], b_router_group[layer],
                              w_router_expert[layer], b_router_expert[layer],
                              w_exp_gate[layer], w_exp_up[layer], w_exp_down[layer],
                              ln_g[layer, 1][None], ln_b[layer, 1][None], alpha)

    def rows(group, j, b, t, shape):
        return _stack([layer_rows[j].reshape((b, t) + shape) for layer_rows in group])

    head_shapes = ((H_A, HEAD_DIM), (H_A, HEAD_DIM), (KV_B, HEAD_DIM), (KV_B, HEAD_DIM), (D_IDX,))
    return ((xp.reshape(bp, tp, d), xs.reshape(bs, ts, d))
            + tuple(rows(attn_p, j, bp, tp, head_shapes[j]) for j in range(5))
            + (_stack(conv_p),)
            + tuple(rows(attn_s, j, bs, ts, head_shapes[j]) for j in range(5))
            + (_stack(conv_s),))
```

```python
import functools
import math

import numpy as np
import jax
import jax.numpy as jnp
from jax import lax
from jax.experimental import pallas as pl
from jax.experimental.pallas import tpu as pltpu

F32 = jnp.float32
BF16 = jnp.bfloat16
I32 = jnp.int32

HEAD_DIM = 64
H_A = 8
H_B = 8
KV_B = 2
G_B = H_B // KV_B
H_IDX = 8
D_IDX = 64
MOBA_BLOCK = 256
MOBA_TOPK = 3
DSA_TOPK = 256
NUM_BUCKETS = 32
MAX_DISTANCE = 128
CONV_W = 3
N_GROUPS = 4
EXPERTS_PER_GROUP = 8
N_EXPERTS = N_GROUPS * EXPERTS_PER_GROUP
LN_EPS = 1e-5
ATTN_COLS = (H_A * HEAD_DIM, H_A * HEAD_DIM, H_A * HEAD_DIM,
             H_B * HEAD_DIM, KV_B * HEAD_DIM, KV_B * HEAD_DIM,
             H_IDX * D_IDX, D_IDX, H_IDX)

LANES = 128
SUBLANES = 8
VMEM_LIMIT = 56 * 1024 * 1024

ATT_TILE = MOBA_BLOCK
ROW_TILE = 256
FFN_ROWS = 256
NEG_INF = float("-inf")


def _bucket_lows():
    n = np.arange(0, 4 * MAX_DISTANCE)
    max_exact = NUM_BUCKETS // 2
    nf = np.maximum(n, 1).astype(np.float32)
    large = max_exact + (np.log(nf / max_exact) / math.log(MAX_DISTANCE / max_exact)
                         * (NUM_BUCKETS - max_exact)).astype(np.int32)
    bucket = np.where(n < max_exact, n, np.minimum(large, NUM_BUCKETS - 1))
    assert np.all(np.diff(bucket) >= 0) and bucket[-1] == NUM_BUCKETS - 1
    lows = [int(np.argmax(bucket == k)) for k in range(NUM_BUCKETS)]
    assert all(bucket[lo] == k for k, lo in enumerate(lows))
    return tuple(lows)


BUCKET_LOWS = _bucket_lows()
FAR_DIST = BUCKET_LOWS[-1]


def _params(sem):
    return pltpu.CompilerParams(dimension_semantics=sem, vmem_limit_bytes=VMEM_LIMIT)


def _dot(a, b):
    return jnp.dot(a, b, preferred_element_type=F32)


def _dot_nt(a, b):
    return lax.dot_general(a, b, (((1,), (1,)), ((), ())), preferred_element_type=F32)


def _split_bf16(a):
    hi = a.astype(BF16)
    lo = (a - hi.astype(F32)).astype(BF16)
    return hi, lo


def _dot3(a, b, nt=False):
    ah, al = _split_bf16(a)
    bh, bl = _split_bf16(b)
    d = _dot_nt if nt else _dot
    return d(ah, bh) + (d(ah, bl) + d(al, bh))


def _iota(shape, dim):
    return lax.broadcasted_iota(I32, shape, dim)


def _layer_norm(z, g, b):
    mu = jnp.mean(z, axis=-1, keepdims=True)
    zc = z - mu
    var = jnp.mean(zc * zc, axis=-1, keepdims=True)
    return zc * lax.rsqrt(var + LN_EPS) * g + b


def _bias_by_distance(dist, bias_of_bucket):
    acc = jnp.zeros(dist.shape, F32) + bias_of_bucket(0)
    for k in range(1, NUM_BUCKETS):
        acc = jnp.where(dist >= BUCKET_LOWS[k], bias_of_bucket(k), acc)
    return acc


def _sort_key(x):
    bits = lax.bitcast_convert_type(x + 0.0, I32)
    return bits ^ (lax.shift_right_arithmetic(bits, 31) & 0x7FFFFFFF)


_ATTN_OFFS = tuple(int(v) for v in np.cumsum((0,) + ATTN_COLS[:-1]))
_ATTN_IN = sum(ATTN_COLS)
_ATTN_IN_PAD = -(-_ATTN_IN // LANES) * LANES
_TAIL_OFF = _ATTN_OFFS[7]


def _attn_inproj_kernel(x_ref, w_ref, qa, ka, va, qb, kb, vb, qi, ki, wi):
    xb = x_ref[...].astype(BF16)
    for o, off, wd in zip((qa, ka, va, qb, kb, vb, qi), _ATTN_OFFS, ATTN_COLS):
        o[...] = _dot(xb, w_ref[:, off:off + wd])
    tail = _dot(xb, w_ref[:, _TAIL_OFF:_ATTN_IN_PAD])
    ki[...] = tail[:, :D_IDX]
    wi[...] = tail[:, D_IDX:D_IDX + H_IDX]


def _attn_inproj(x2, w_pad):
    n, d = x2.shape
    tm = min(ROW_TILE, n)
    outs = tuple(jax.ShapeDtypeStruct((n, c), F32) for c in ATTN_COLS)
    return pl.pallas_call(
        _attn_inproj_kernel,
        grid=(n // tm,),
        in_specs=[pl.BlockSpec((tm, d), lambda i: (i, 0)),
                  pl.BlockSpec((d, _ATTN_IN_PAD), lambda i: (0, 0))],
        out_specs=tuple(pl.BlockSpec((tm, c), lambda i: (i, 0)) for c in ATTN_COLS),
        out_shape=outs,
        compiler_params=_params(("arbitrary",)),
        name="attn_inproj",
    )(x2, w_pad)


def _build_bias_tiles(bias_ref, scr, head0, n_heads):
    row = _iota((ATT_TILE, ATT_TILE), 0)
    col = _iota((ATT_TILE, ATT_TILE), 1)
    for t in range(2):
        dist = row - col + (ATT_TILE if t == 0 else 0)
        for h in range(n_heads):
            scr[h, t] = _bias_by_distance(dist, lambda k, h=h: bias_ref[k, head0 + h])


def _online_update(s, v, m, l, acc):
    m_new = jnp.maximum(m, jnp.max(s, axis=1, keepdims=True))
    m_safe = jnp.where(m_new == NEG_INF, 0.0, m_new)
    alpha = jnp.exp(m - m_safe)
    p = jnp.exp(s - m_safe)
    l = alpha * l + jnp.sum(p, axis=1, keepdims=True)
    acc = alpha * acc + _dot(p.astype(BF16), v)
    return m_new, l, acc


def _moba_prompt_kernel(bias_ref, q_ref, k_ref, v_ref, o_ref, kmean_scr, bias_scr, *, nblk):
    b, hp, i = pl.program_id(0), pl.program_id(1), pl.program_id(2)
    heads_per_step = LANES // HEAD_DIM

    @pl.when((b == 0) & (hp == 0) & (i == 0))
    def _():
        _build_bias_tiles(bias_ref, bias_scr, 0, H_A)

    @pl.when(i == 0)
    def _():
        for n in range(nblk):
            kmean_scr[n:n + 1, :] = jnp.mean(
                k_ref[n * MOBA_BLOCK:(n + 1) * MOBA_BLOCK, :], axis=0, keepdims=True)

    q = q_ref[...] * (1.0 / math.sqrt(HEAD_DIM))
    row = _iota((ATT_TILE, ATT_TILE), 0)
    col = _iota((ATT_TILE, ATT_TILE), 1)
    blk_lane = _iota((ATT_TILE, nblk), 1).astype(F32)
    outs = []
    for hh in range(heads_per_step):
        h = hp * heads_per_step + hh
        cs = slice(hh * HEAD_DIM, (hh + 1) * HEAD_DIM)
        qh = q[:, cs]
        gate = _dot3(qh, kmean_scr[:, cs], nt=True)
        gate = jnp.where(blk_lane < i.astype(F32), gate, NEG_INF)
        selmask = jnp.zeros((ATT_TILE, nblk), F32)
        for _ in range(min(MOBA_TOPK, nblk)):
            mx = jnp.max(gate, axis=1, keepdims=True)
            first = jnp.min(jnp.where(gate == mx, blk_lane, float(nblk)), axis=1, keepdims=True)
            hit = blk_lane == first
            selmask = jnp.where(hit & (mx > NEG_INF), 1.0, selmask)
            gate = jnp.where(hit, NEG_INF, gate)

        qb = qh.astype(BF16)
        far_bias = bias_ref[NUM_BUCKETS - 1, h]
        start = pl.multiple_of(i * ATT_TILE, ATT_TILE)
        s = _dot_nt(qb, k_ref[pl.ds(start, ATT_TILE), cs].astype(BF16)) + bias_scr[h, 1]
        s = jnp.where(col <= row, s, NEG_INF)
        m0 = jnp.max(s, axis=1, keepdims=True)
        p = jnp.exp(s - m0)
        l0 = jnp.sum(p, axis=1, keepdims=True)
        acc0 = _dot(p.astype(BF16), v_ref[pl.ds(start, ATT_TILE), cs].astype(BF16))

        def body(j, carry, qb=qb, cs=cs, h=h, selmask=selmask, far_bias=far_bias):
            m, l, acc = carry
            ks = pl.multiple_of(j * ATT_TILE, ATT_TILE)
            s = _dot_nt(qb, k_ref[pl.ds(ks, ATT_TILE), cs].astype(BF16))
            s = s + jnp.where(j == i - 1, bias_scr[h, 0], far_bias)
            sel = jnp.sum(jnp.where(blk_lane == j.astype(F32), selmask, 0.0), axis=1, keepdims=True)
            s = jnp.where(sel > 0.0, s, NEG_INF)
            return _online_update(s, v_ref[pl.ds(ks, ATT_TILE), cs].astype(BF16), m, l, acc)

        _, l, acc = lax.fori_loop(0, i, body, (m0, l0, acc0))
        outs.append(acc / l)
    o_ref[...] = jnp.concatenate(outs, axis=1)


def _moba_prompt(q2, k2, v2, rel_bias, batch, seq):
    nq = seq // ATT_TILE
    nblk = seq // MOBA_BLOCK
    width = q2.shape[1]
    return pl.pallas_call(
        functools.partial(_moba_prompt_kernel, nblk=nblk),
        grid=(batch, width // LANES, nq),
        in_specs=[pl.BlockSpec(memory_space=pltpu.SMEM),
                  pl.BlockSpec((ATT_TILE, LANES), lambda b, hp, i: (b * nq + i, hp)),
                  pl.BlockSpec((seq, LANES), lambda b, hp, i: (b, hp)),
                  pl.BlockSpec((seq, LANES), lambda b, hp, i: (b, hp))],
        out_specs=pl.BlockSpec((ATT_TILE, LANES), lambda b, hp, i: (b * nq + i, hp)),
        out_shape=jax.ShapeDtypeStruct((batch * seq, width), F32),
        scratch_shapes=[pltpu.VMEM((nblk, LANES), F32),
                        pltpu.VMEM((H_A, 2, ATT_TILE, ATT_TILE), F32)],
        compiler_params=_params(("arbitrary", "arbitrary", "arbitrary")),
        name="moba_prompt",
    )(rel_bias, q2, k2, v2)


INT_MIN = -2 ** 31


def _strict_upper(n):
    return jnp.where(_iota((n, n), 0) < _iota((n, n), 1), 1.0, 0.0).astype(BF16)


def _select_tile(key, thr, need, carry, tri):
    eqf = jnp.where(key == thr, 1.0, 0.0)
    prefix = _dot(eqf.astype(BF16), tri) + carry
    self_ = jnp.where(key > thr, 1.0, jnp.where(prefix < need, eqf, 0.0))
    return self_, carry + jnp.sum(eqf, axis=1, keepdims=True)


def _kth_largest_key(count_ge, rows, k_sel):
    kk = float(k_sel)
    zero = jnp.zeros((rows, 1), I32)
    thr = jnp.where(count_ge(zero) >= kk, zero, jnp.full((rows, 1), INT_MIN, I32))

    def bit_body(t, thr):
        cand = thr | lax.shift_left(jnp.int32(1), 30 - t)
        return jnp.where(count_ge(cand) >= kk, cand, thr)

    return lax.fori_loop(0, 31, bit_body, thr)


def _dsa_prompt_kernel(bias_ref, qb_ref, qi_ref, wi_ref, kb_ref, vb_ref, ki_ref, o_ref,
                       key_scr, bias_scr, m_scr, l_scr, acc_scr, *, k_sel):
    b, i = pl.program_id(0), pl.program_id(1)

    @pl.when((b == 0) & (i == 0))
    def _():
        _build_bias_tiles(bias_ref, bias_scr, H_A, H_B)

    row = _iota((ATT_TILE, ATT_TILE), 0)
    col = _iota((ATT_TILE, ATT_TILE), 1)
    idx_scale = 1.0 / math.sqrt(D_IDX * H_IDX)

    qi = qi_ref[...]
    qi_h = [_split_bf16(qi[:, h * D_IDX:(h + 1) * D_IDX]) for h in range(H_IDX)]
    wi = wi_ref[...]
    w_h = [wi[:, h:h + 1] for h in range(H_IDX)]

    def score_body(j, _):
        ks = pl.multiple_of(j * ATT_TILE, ATT_TILE)
        kt_hi, kt_lo = _split_bf16(ki_ref[pl.ds(ks, ATT_TILE), :])
        acc = jnp.zeros((ATT_TILE, ATT_TILE), F32)
        for h in range(H_IDX):
            q_hi, q_lo = qi_h[h]
            s = _dot_nt(q_hi, kt_hi) + (_dot_nt(q_hi, kt_lo) + _dot_nt(q_lo, kt_hi))
            acc = acc + jnp.maximum(s, 0.0) * w_h[h]
        score = acc * idx_scale
        causal = (col + (j - i) * ATT_TILE) <= row
        key_scr[j] = _sort_key(jnp.where(causal, score, NEG_INF))
        return 0

    lax.fori_loop(0, i + 1, score_body, 0)

    def count_ge(cand):
        def body(j, cnt):
            return cnt + jnp.sum(jnp.where(key_scr[j] >= cand, 1.0, 0.0), axis=1, keepdims=True)
        return lax.fori_loop(0, i + 1, body, jnp.zeros((ATT_TILE, 1), F32))

    thr = _kth_largest_key(count_ge, ATT_TILE, k_sel)

    def count_gt_body(j, cnt):
        return cnt + jnp.sum(jnp.where(key_scr[j] > thr, 1.0, 0.0), axis=1, keepdims=True)

    need = float(k_sel) - lax.fori_loop(0, i + 1, count_gt_body, jnp.zeros((ATT_TILE, 1), F32))

    q = qb_ref[...] * (1.0 / math.sqrt(HEAD_DIM))
    q_h = [q[:, h * HEAD_DIM:(h + 1) * HEAD_DIM].astype(BF16) for h in range(H_B)]
    tri = _strict_upper(ATT_TILE)
    m_scr[...] = jnp.full(m_scr.shape, NEG_INF, F32)
    l_scr[...] = jnp.zeros(l_scr.shape, F32)
    acc_scr[...] = jnp.zeros(acc_scr.shape, F32)

    def attend_body(j, carry):
        ks = pl.multiple_of(j * ATT_TILE, ATT_TILE)
        self_, carry = _select_tile(key_scr[j], thr, need, carry, tri)
        causal = (col + (j - i) * ATT_TILE) <= row
        keep = jnp.where(causal, self_, 0.0) > 0.0
        near = jnp.maximum(j - (i - 1), 0)
        for n in range(KV_B):
            cs = slice(n * HEAD_DIM, (n + 1) * HEAD_DIM)
            kt = kb_ref[pl.ds(ks, ATT_TILE), cs].astype(BF16)
            vt = vb_ref[pl.ds(ks, ATT_TILE), cs].astype(BF16)
            for g in range(G_B):
                h = n * G_B + g
                bias = jnp.where(j >= i - 1, bias_scr[h, near], bias_ref[NUM_BUCKETS - 1, H_A + h])
                s = jnp.where(keep, _dot_nt(q_h[h], kt) + bias, NEG_INF)
                m, l, acc = _online_update(s, vt, m_scr[h], l_scr[h], acc_scr[h])
                m_scr[h] = m
                l_scr[h] = l
                acc_scr[h] = acc
        return carry

    lax.fori_loop(0, i + 1, attend_body, jnp.zeros((ATT_TILE, 1), F32))
    o_ref[...] = jnp.concatenate([acc_scr[h] / l_scr[h] for h in range(H_B)], axis=1)


def _dsa_prompt(qb2, qi2, wi2, kb2, vb2, ki2, rel_bias, batch, seq):
    nq = seq // ATT_TILE
    k_sel = min(DSA_TOPK, seq // 4)
    qmap = lambda b, i: (b * nq + i, 0)
    kmap = lambda b, i: (b, 0)
    return pl.pallas_call(
        functools.partial(_dsa_prompt_kernel, k_sel=k_sel),
        grid=(batch, nq),
        in_specs=[pl.BlockSpec(memory_space=pltpu.SMEM),
                  pl.BlockSpec((ATT_TILE, H_B * HEAD_DIM), qmap),
                  pl.BlockSpec((ATT_TILE, H_IDX * D_IDX), qmap),
                  pl.BlockSpec((ATT_TILE, H_IDX), qmap),
                  pl.BlockSpec((seq, KV_B * HEAD_DIM), kmap),
                  pl.BlockSpec((seq, KV_B * HEAD_DIM), kmap),
                  pl.BlockSpec((seq, D_IDX), kmap)],
        out_specs=pl.BlockSpec((ATT_TILE, H_B * HEAD_DIM), qmap),
        out_shape=jax.ShapeDtypeStruct((batch * seq, H_B * HEAD_DIM), F32),
        scratch_shapes=[pltpu.VMEM((nq, ATT_TILE, ATT_TILE), I32),
                        pltpu.VMEM((H_B, 2, ATT_TILE, ATT_TILE), F32),
                        pltpu.VMEM((H_B, ATT_TILE, 1), F32),
                        pltpu.VMEM((H_B, ATT_TILE, 1), F32),
                        pltpu.VMEM((H_B, ATT_TILE, HEAD_DIM), F32)],
        compiler_params=_params(("arbitrary", "arbitrary")),
        name="dsa_prompt",
    )(rel_bias, qb2, qi2, wi2, kb2, vb2, ki2)


_T_GROUPS = (1, 2, 4, 5, 7, 8)
_T_ROWS = tuple(ATTN_COLS[g] for g in _T_GROUPS)
_T_OFFS = tuple(int(v) for v in np.cumsum((0,) + _T_ROWS[:-1]))
BF16_ROWS = 2 * SUBLANES
_T_TOTAL = -(-sum(_T_ROWS) // BF16_ROWS) * BF16_ROWS
_N_GROUPS_NORMAL = (0, 1, 3, 4, 6, 7)


def _attn_inproj_prompt_kernel(x_ref, w_ref, wt_ref, qa, ka, qb, kb, qi, ki,
                               kat, vat, kbt, vbt, kit, wit):
    xb = x_ref[...].astype(BF16)
    for o, g in zip((qa, ka, qb, kb, qi), _N_GROUPS_NORMAL[:5]):
        o[...] = _dot(xb, w_ref[:, _ATTN_OFFS[g]:_ATTN_OFFS[g] + ATTN_COLS[g]])
    ki[...] = _dot(xb, w_ref[:, _TAIL_OFF:_ATTN_IN_PAD])[:, :D_IDX]
    for o, off, rows in zip((kat, vat, kbt, vbt, kit, wit), _T_OFFS, _T_ROWS):
        padded = -(-rows // BF16_ROWS) * BF16_ROWS
        o[0] = _dot_nt(wt_ref[off:off + padded, :], xb)[:rows]


def _transposed_proj_weight(w_in):
    cols = [w_in[:, _ATTN_OFFS[g]:_ATTN_OFFS[g] + ATTN_COLS[g]] for g in _T_GROUPS]
    wt = jnp.concatenate(cols, axis=1).T
    return jnp.pad(wt, ((0, _T_TOTAL - wt.shape[0]), (0, 0))).astype(BF16)


def _attn_inproj_prompt(x2, w_pad, w_t, batch, seq):
    n, d = x2.shape
    tm = min(ROW_TILE, seq)
    nt = seq // tm
    normal = [ATTN_COLS[g] for g in _N_GROUPS_NORMAL]
    return pl.pallas_call(
        _attn_inproj_prompt_kernel,
        grid=(batch, nt),
        in_specs=[pl.BlockSpec((tm, d), lambda b, i: (b * nt + i, 0)),
                  pl.BlockSpec(w_pad.shape, lambda b, i: (0, 0)),
                  pl.BlockSpec(w_t.shape, lambda b, i: (0, 0))],
        out_specs=[pl.BlockSpec((tm, c), lambda b, i: (b * nt + i, 0)) for c in normal]
                  + [pl.BlockSpec((1, r, tm), lambda b, i: (b, 0, i)) for r in _T_ROWS],
        out_shape=[jax.ShapeDtypeStruct((n, c), F32) for c in normal]
                  + [jax.ShapeDtypeStruct((batch, r, seq), F32) for r in _T_ROWS],
        compiler_params=_params(("arbitrary", "arbitrary")),
        name="attn_inproj_prompt",
    )(x2, w_pad, w_t)


def _build_bias_tiles_t(bias_ref, scr, head0, n_heads):
    key = _iota((ATT_TILE, ATT_TILE), 0)
    qry = _iota((ATT_TILE, ATT_TILE), 1)
    for t in range(2):
        dist = qry - key + (ATT_TILE if t == 0 else 0)
        for h in range(n_heads):
            scr[h, t] = _bias_by_distance(dist, lambda k, h=h: bias_ref[k, head0 + h])


def _online_update_t(s, vt, m, l, acc):
    m_new = jnp.maximum(m, jnp.max(s, axis=0, keepdims=True))
    m_safe = jnp.where(m_new == NEG_INF, 0.0, m_new)
    alpha = jnp.exp(m - m_safe)
    p = jnp.exp(s - m_safe)
    l = alpha * l + jnp.sum(p, axis=0, keepdims=True)
    acc = alpha * acc + _dot(vt, p.astype(BF16))
    return m_new, l, acc


def _moba_prompt_t_kernel(bias_ref, q_ref, k_ref, vt_ref, o_ref, kmean_scr, bias_scr, *, nblk):
    b, hp, i = pl.program_id(0), pl.program_id(1), pl.program_id(2)
    heads_per_step = LANES // HEAD_DIM

    @pl.when((b == 0) & (hp == 0) & (i == 0))
    def _():
        _build_bias_tiles_t(bias_ref, bias_scr, 0, H_A)

    @pl.when(i == 0)
    def _():
        for n in range(nblk):
            kmean_scr[n:n + 1, :] = jnp.mean(
                k_ref[n * MOBA_BLOCK:(n + 1) * MOBA_BLOCK, :], axis=0, keepdims=True)

    q = q_ref[...] * (1.0 / math.sqrt(HEAD_DIM))
    key = _iota((ATT_TILE, ATT_TILE), 0)
    qry = _iota((ATT_TILE, ATT_TILE), 1)
    blk = _iota((nblk, ATT_TILE), 0).astype(F32)
    start = pl.multiple_of(i * ATT_TILE, ATT_TILE)
    outs = []
    for hh in range(heads_per_step):
        h = hp * heads_per_step + hh
        cs = slice(hh * HEAD_DIM, (hh + 1) * HEAD_DIM)
        qh = q[:, cs]
        gate = _dot3(kmean_scr[:, cs], qh, nt=True)
        gate = jnp.where(blk < i.astype(F32), gate, NEG_INF)
        selmask = jnp.zeros((nblk, ATT_TILE), F32)
        for _ in range(min(MOBA_TOPK, nblk)):
            mx = jnp.max(gate, axis=0, keepdims=True)
            first = jnp.min(jnp.where(gate == mx, blk, float(nblk)), axis=0, keepdims=True)
            hit = blk == first
            selmask = jnp.where(hit & (mx > NEG_INF), 1.0, selmask)
            gate = jnp.where(hit, NEG_INF, gate)

        qb = qh.astype(BF16)
        far_bias = bias_ref[NUM_BUCKETS - 1, h]
        rows = slice(hh * HEAD_DIM, (hh + 1) * HEAD_DIM)
        s = _dot_nt(k_ref[pl.ds(start, ATT_TILE), cs].astype(BF16), qb) + bias_scr[h, 1]
        s = jnp.where(key <= qry, s, NEG_INF)
        m0 = jnp.max(s, axis=0, keepdims=True)
        p = jnp.exp(s - m0)
        l0 = jnp.sum(p, axis=0, keepdims=True)
        acc0 = _dot(vt_ref[0, rows, pl.ds(start, ATT_TILE)].astype(BF16), p.astype(BF16))

        def body(j, carry, qb=qb, cs=cs, rows=rows, h=h, selmask=selmask, far_bias=far_bias):
            m, l, acc = carry
            ks = pl.multiple_of(j * ATT_TILE, ATT_TILE)
            s = _dot_nt(k_ref[pl.ds(ks, ATT_TILE), cs].astype(BF16), qb)
            s = s + jnp.where(j == i - 1, bias_scr[h, 0], far_bias)
            sel = jnp.sum(jnp.where(blk == j.astype(F32), selmask, 0.0), axis=0, keepdims=True)
            s = jnp.where(sel > 0.0, s, NEG_INF)
            vt = vt_ref[0, rows, pl.ds(ks, ATT_TILE)].astype(BF16)
            return _online_update_t(s, vt, m, l, acc)

        _, l, acc = lax.fori_loop(0, i, body, (m0, l0, acc0))
        outs.append(acc / l)
    o_ref[...] = jnp.concatenate(outs, axis=0).T


def _moba_prompt_t(q2, k2, vt3, rel_bias, batch, seq):
    nq = seq // ATT_TILE
    nblk = seq // MOBA_BLOCK
    width = q2.shape[1]
    return pl.pallas_call(
        functools.partial(_moba_prompt_t_kernel, nblk=nblk),
        grid=(batch, width // LANES, nq),
        in_specs=[pl.BlockSpec(memory_space=pltpu.SMEM),
                  pl.BlockSpec((ATT_TILE, LANES), lambda b, hp, i: (b * nq + i, hp)),
                  pl.BlockSpec((seq, LANES), lambda b, hp, i: (b, hp)),
                  pl.BlockSpec((1, LANES, seq), lambda b, hp, i: (b, hp, 0))],
        out_specs=pl.BlockSpec((ATT_TILE, LANES), lambda b, hp, i: (b * nq + i, hp)),
        out_shape=jax.ShapeDtypeStruct((batch * seq, width), F32),
        scratch_shapes=[pltpu.VMEM((nblk, LANES), F32),
                        pltpu.VMEM((H_A, 2, ATT_TILE, ATT_TILE), F32)],
        compiler_params=_params(("arbitrary", "arbitrary", "arbitrary")),
        name="moba_prompt",
    )(rel_bias, q2, k2, vt3)


def _strict_lower(n):
    return jnp.where(_iota((n, n), 1) < _iota((n, n), 0), 1.0, 0.0).astype(BF16)


def _dsa_prompt_t_kernel(bias_ref, qb_ref, qi_ref, wit_ref, kb_ref, vbt_ref, ki_ref, o_ref,
                         key_scr, bias_scr, m_scr, l_scr, acc_scr, *, k_sel):
    b, i = pl.program_id(0), pl.program_id(1)

    @pl.when((b == 0) & (i == 0))
    def _():
        _build_bias_tiles_t(bias_ref, bias_scr, H_A, H_B)

    key_ix = _iota((ATT_TILE, ATT_TILE), 0)
    qry_ix = _iota((ATT_TILE, ATT_TILE), 1)
    idx_scale = 1.0 / math.sqrt(D_IDX * H_IDX)

    qi = qi_ref[...]
    q_cat = []
    for h in range(H_IDX):
        hi, lo = _split_bf16(qi[:, h * D_IDX:(h + 1) * D_IDX])
        q_cat.append(jnp.concatenate([hi, lo, hi], axis=1))
    wit = wit_ref[0]
    w_row = [wit[h:h + 1, :] for h in range(H_IDX)]

    def score_body(j, _):
        ks = pl.multiple_of(j * ATT_TILE, ATT_TILE)
        hi, lo = _split_bf16(ki_ref[pl.ds(ks, ATT_TILE), :])
        k_cat = jnp.concatenate([hi, hi, lo], axis=1)
        acc = jnp.zeros((ATT_TILE, ATT_TILE), F32)
        for h in range(H_IDX):
            acc = acc + jnp.maximum(_dot_nt(k_cat, q_cat[h]), 0.0) * w_row[h]
        causal = (key_ix + (j - i) * ATT_TILE) <= qry_ix
        key_scr[j] = _sort_key(jnp.where(causal, acc * idx_scale, NEG_INF))
        return 0

    lax.fori_loop(0, i + 1, score_body, 0)

    def count_ge(cand):
        def body(j, cnt):
            return cnt + jnp.sum(jnp.where(key_scr[j] >= cand, 1.0, 0.0), axis=0, keepdims=True)
        return lax.fori_loop(0, i + 1, body, jnp.zeros((1, ATT_TILE), F32))

    kk = float(k_sel)
    zero = jnp.zeros((1, ATT_TILE), I32)
    thr0 = jnp.where(count_ge(zero) >= kk, zero, jnp.full((1, ATT_TILE), INT_MIN, I32))

    def bit_body(t, thr):
        cand = thr | lax.shift_left(jnp.int32(1), 30 - t)
        return jnp.where(count_ge(cand) >= kk, cand, thr)

    thr = lax.fori_loop(0, 31, bit_body, thr0)

    def count_gt_body(j, cnt):
        return cnt + jnp.sum(jnp.where(key_scr[j] > thr, 1.0, 0.0), axis=0, keepdims=True)

    need = kk - lax.fori_loop(0, i + 1, count_gt_body, jnp.zeros((1, ATT_TILE), F32))

    q = qb_ref[...] * (1.0 / math.sqrt(HEAD_DIM))
    q_h = [q[:, h * HEAD_DIM:(h + 1) * HEAD_DIM].astype(BF16) for h in range(H_B)]
    lower = _strict_lower(ATT_TILE)
    m_scr[...] = jnp.full(m_scr.shape, NEG_INF, F32)
    l_scr[...] = jnp.zeros(l_scr.shape, F32)
    acc_scr[...] = jnp.zeros(acc_scr.shape, F32)

    def attend_body(j, carry):
        ks = pl.multiple_of(j * ATT_TILE, ATT_TILE)
        key = key_scr[j]
        eqf = jnp.where(key == thr, 1.0, 0.0)
        earlier_ties = _dot(lower, eqf.astype(BF16)) + carry
        member = jnp.where(key > thr, 1.0, jnp.where(earlier_ties < need, eqf, 0.0))
        causal = (key_ix + (j - i) * ATT_TILE) <= qry_ix
        keep = jnp.where(causal, member, 0.0) > 0.0
        near = jnp.maximum(j - (i - 1), 0)
        for n in range(KV_B):
            cs = slice(n * HEAD_DIM, (n + 1) * HEAD_DIM)
            kt = kb_ref[pl.ds(ks, ATT_TILE), cs].astype(BF16)
            vt = vbt_ref[0, cs, pl.ds(ks, ATT_TILE)].astype(BF16)
            for g in range(G_B):
                h = n * G_B + g
                bias = jnp.where(j >= i - 1, bias_scr[h, near], bias_ref[NUM_BUCKETS - 1, H_A + h])
                s = jnp.where(keep, _dot_nt(kt, q_h[h]) + bias, NEG_INF)
                m, l, acc = _online_update_t(s, vt, m_scr[h], l_scr[h], acc_scr[h])
                m_scr[h] = m
                l_scr[h] = l
                acc_scr[h] = acc
        return carry + jnp.sum(eqf, axis=0, keepdims=True)

    lax.fori_loop(0, i + 1, attend_body, jnp.zeros((1, ATT_TILE), F32))
    o_ref[...] = jnp.concatenate([acc_scr[h] / l_scr[h] for h in range(H_B)], axis=0).T


def _dsa_prompt_t(qb2, qi2, wit3, kb2, vbt3, ki2, rel_bias, batch, seq):
    nq = seq // ATT_TILE
    k_sel = min(DSA_TOPK, seq // 4)
    qmap = lambda b, i: (b * nq + i, 0)
    kmap = lambda b, i: (b, 0)
    return pl.pallas_call(
        functools.partial(_dsa_prompt_t_kernel, k_sel=k_sel),
        grid=(batch, nq),
        in_specs=[pl.BlockSpec(memory_space=pltpu.SMEM),
                  pl.BlockSpec((ATT_TILE, H_B * HEAD_DIM), qmap),
                  pl.BlockSpec((ATT_TILE, H_IDX * D_IDX), qmap),
                  pl.BlockSpec((1, H_IDX, ATT_TILE), lambda b, i: (b, 0, i)),
                  pl.BlockSpec((seq, KV_B * HEAD_DIM), kmap),
                  pl.BlockSpec((1, KV_B * HEAD_DIM, seq), lambda b, i: (b, 0, 0)),
                  pl.BlockSpec((seq, D_IDX), kmap)],
        out_specs=pl.BlockSpec((ATT_TILE, H_B * HEAD_DIM), qmap),
        out_shape=jax.ShapeDtypeStruct((batch * seq, H_B * HEAD_DIM), F32),
        scratch_shapes=[pltpu.VMEM((nq, ATT_TILE, ATT_TILE), I32),
                        pltpu.VMEM((H_B, 2, ATT_TILE, ATT_TILE), F32),
                        pltpu.VMEM((H_B, 1, ATT_TILE), F32),
                        pltpu.VMEM((H_B, 1, ATT_TILE), F32),
                        pltpu.VMEM((H_B, HEAD_DIM, ATT_TILE), F32)],
        compiler_params=_params(("arbitrary", "arbitrary")),
        name="dsa_prompt",
    )(rel_bias, qb2, qi2, wit3, kb2, vbt3, ki2)


N_CACHES = 5
TAIL_ROWS = LANES


def _softmax_rows(s):
    m = jnp.max(s, axis=1, keepdims=True)
    p = jnp.exp(s - m)
    return p / jnp.sum(p, axis=1, keepdims=True)


def _stack_heads(x, n_heads, width):
    return jnp.concatenate([x[:, h * width:(h + 1) * width] for h in range(n_heads)], axis=0)


def _sample_attn_kernel(pt_ref, bias_a_ref, bias_b_ref, qa_ref, ka_ref, va_ref, qb_ref, kb_ref,
                        vb_ref, qi_ref, ki_ref, wi_ref, cka, cva, ckb, cvb, cki,
                        ya_ref, yb_ref, ka_buf, va_buf, kb_buf, vb_buf, ki_buf, sem,
                        bias_a_scr, bias_b_scr, *, n_seq, n_pages, page, ts):
    b = pl.program_id(0)
    slot = b % 2
    past = n_pages * page
    lp = past + TAIL_ROWS
    rows = H_A * ts
    caches = (cka, cva, ckb, cvb, cki)
    bufs = (ka_buf, va_buf, kb_buf, vb_buf, ki_buf)

    def page_copies(seq, slot_):
        cps = []
        for p in range(n_pages):
            pg = pt_ref[seq, p]
            for c in range(N_CACHES):
                cps.append(pltpu.make_async_copy(
                    caches[c].at[pg], bufs[c].at[slot_, pl.ds(p * page, page)], sem.at[slot_, c]))
        return cps

    @pl.when(b == 0)
    def _():
        for cp in page_copies(0, 0):
            cp.start()
        dist = past + _iota((rows, lp), 0) % ts - _iota((rows, lp), 1)
        bias_a_scr[...] = _bias_by_distance(dist, lambda k: bias_a_ref[:, k:k + 1])
        bias_b_scr[...] = _bias_by_distance(dist, lambda k: bias_b_ref[:, k:k + 1])

    @pl.when(b + 1 < n_seq)
    def _():
        for cp in page_copies(b + 1, 1 - slot):
            cp.start()

    for buf, new in zip(bufs, (ka_ref, va_ref, kb_ref, vb_ref, ki_ref)):
        buf[slot, past:lp, :] = jnp.zeros((TAIL_ROWS, buf.shape[2]), F32)
        buf[slot, past:past + ts, :] = new[0]

    for cp in page_copies(b, slot):
        cp.wait()

    colpos = _iota((ts, lp), 1)
    qpos = _iota((ts, lp), 0)
    valid = colpos <= past + qpos
    valid_rows = jnp.where(_iota((rows, lp), 1) <= past + _iota((rows, lp), 0) % ts, 1.0, 0.0)
    row_head = _iota((rows, 1), 0) // ts

    n_past_blk = past // MOBA_BLOCK
    qa = qa_ref[0] * (1.0 / math.sqrt(HEAD_DIM))
    width_a = H_A * HEAD_DIM
    q_rows = jnp.concatenate([qa] * H_A, axis=0)
    own_head_a = (_iota((rows, width_a), 1) // HEAD_DIM) == row_head
    q_bd = jnp.where(own_head_a, q_rows, 0.0)
    kmean = jnp.concatenate(
        [jnp.mean(ka_buf[slot, n * MOBA_BLOCK:(n + 1) * MOBA_BLOCK, :], axis=0, keepdims=True)
         for n in range(n_past_blk)], axis=0)
    gate = _dot3(q_bd, kmean, nt=True)
    blk_lane = _iota((rows, n_past_blk), 1).astype(F32)
    selmask = jnp.zeros((rows, n_past_blk), F32)
    for _ in range(min(MOBA_TOPK, n_past_blk)):
        mx = jnp.max(gate, axis=1, keepdims=True)
        first = jnp.min(jnp.where(gate == mx, blk_lane, float(n_past_blk)), axis=1, keepdims=True)
        hit = blk_lane == first
        selmask = jnp.where(hit, 1.0, selmask)
        gate = jnp.where(hit, NEG_INF, gate)
    expand = jnp.where(_iota((n_past_blk, lp), 1) // MOBA_BLOCK == _iota((n_past_blk, lp), 0),
                       1.0, 0.0).astype(BF16)
    in_sel_blk = _dot(selmask.astype(BF16), expand)
    keep_a = jnp.where(_iota((rows, lp), 1) >= past, valid_rows, in_sel_blk) > 0.0
    s = _dot_nt(q_bd.astype(BF16), ka_buf[slot].astype(BF16)) + bias_a_scr[...]
    p = _softmax_rows(jnp.where(keep_a, s, NEG_INF))
    o = jnp.where(own_head_a, _dot(p.astype(BF16), va_buf[slot].astype(BF16)), 0.0)
    ya = o[0:ts]
    for h in range(1, H_A):
        ya = ya + o[h * ts:(h + 1) * ts]
    ya_ref[0] = ya

    k_sel = min(DSA_TOPK, (past + ts) // 4)
    qi_rows = _stack_heads(qi_ref[0], H_IDX, D_IDX)
    wi = wi_ref[0]
    w_rows = jnp.concatenate([wi[:, h:h + 1] for h in range(H_IDX)], axis=0)
    contrib = jnp.maximum(_dot3(qi_rows, ki_buf[slot], nt=True), 0.0) * w_rows
    score = contrib[0:ts]
    for h in range(1, H_IDX):
        score = score + contrib[h * ts:(h + 1) * ts]
    score = jnp.where(valid, score * (1.0 / math.sqrt(D_IDX * H_IDX)), NEG_INF)
    key = _sort_key(score)

    def count_ge(cand):
        return jnp.sum(jnp.where(key >= cand, 1.0, 0.0), axis=1, keepdims=True)

    thr = _kth_largest_key(count_ge, ts, k_sel)
    need = float(k_sel) - jnp.sum(jnp.where(key > thr, 1.0, 0.0), axis=1, keepdims=True)
    tri = _strict_upper(LANES)
    carry = jnp.zeros((ts, 1), F32)
    sel_chunks = []
    for c in range(lp // LANES):
        sel_c, carry = _select_tile(key[:, c * LANES:(c + 1) * LANES], thr, need, carry, tri)
        sel_chunks.append(sel_c)
    keep = jnp.where(valid, jnp.concatenate(sel_chunks, axis=1), 0.0)
    keep_b = jnp.concatenate([keep] * H_B, axis=0) > 0.0

    qb = qb_ref[0] * (1.0 / math.sqrt(HEAD_DIM))
    qb_rows = _stack_heads(qb, H_B, HEAD_DIM)
    width_b = KV_B * HEAD_DIM
    own_kv = (_iota((rows, width_b), 1) // HEAD_DIM) == row_head // G_B
    qb_bd = jnp.where(own_kv, jnp.concatenate([qb_rows] * KV_B, axis=1), 0.0)
    s = _dot_nt(qb_bd.astype(BF16), kb_buf[slot].astype(BF16)) + bias_b_scr[...]
    p = _softmax_rows(jnp.where(keep_b, s, NEG_INF))
    o = _dot(p.astype(BF16), vb_buf[slot].astype(BF16))
    pieces = []
    for h in range(H_B):
        n = h // G_B
        pieces.append(o[h * ts:(h + 1) * ts, n * HEAD_DIM:(n + 1) * HEAD_DIM])
    yb_ref[0] = jnp.concatenate(pieces, axis=1)


def _sample_attn(new_rows, caches, page_table, rel_bias, n_seq, ts):
    n_pages = page_table.shape[1]
    page = caches[0].shape[1]
    past = n_pages * page
    assert past % MOBA_BLOCK == 0 and ts <= SUBLANES
    lp = past + TAIL_ROWS
    rows = H_A * ts
    bias_a_rows = jnp.repeat(rel_bias[:, :H_A].T, ts, axis=0)
    bias_b_rows = jnp.repeat(rel_bias[:, H_A:].T, ts, axis=0)
    full = lambda a: pl.BlockSpec(a.shape, lambda b, pt: (0, 0))
    seq_spec = lambda a: pl.BlockSpec((1, ts, a.shape[2]), lambda b, pt: (b, 0, 0))
    grid_spec = pltpu.PrefetchScalarGridSpec(
        num_scalar_prefetch=1,
        grid=(n_seq,),
        in_specs=[full(bias_a_rows), full(bias_b_rows)] + [seq_spec(a) for a in new_rows]
                 + [pl.BlockSpec(memory_space=pl.ANY)] * N_CACHES,
        out_specs=[pl.BlockSpec((1, ts, H_A * HEAD_DIM), lambda b, pt: (b, 0, 0)),
                   pl.BlockSpec((1, ts, H_B * HEAD_DIM), lambda b, pt: (b, 0, 0))],
        scratch_shapes=[pltpu.VMEM((2, lp, c.shape[2]), F32) for c in caches]
                       + [pltpu.SemaphoreType.DMA((2, N_CACHES)),
                          pltpu.VMEM((rows, lp), F32), pltpu.VMEM((rows, lp), F32)],
    )
    return pl.pallas_call(
        functools.partial(_sample_attn_kernel, n_seq=n_seq, n_pages=n_pages, page=page, ts=ts),
        grid_spec=grid_spec,
        out_shape=[jax.ShapeDtypeStruct((n_seq, ts, H_A * HEAD_DIM), F32),
                   jax.ShapeDtypeStruct((n_seq, ts, H_B * HEAD_DIM), F32)],
        compiler_params=_params(("arbitrary",)),
        name="sample_attn",
    )(page_table, bias_a_rows, bias_b_rows, *new_rows, *caches)


def _sample_attn_t_kernel(pt_ref, bias_a_ref, bias_b_ref, qa_ref, ka_ref, va_ref, qb_ref, kb_ref,
                          vb_ref, qi_ref, ki_ref, wi_ref, cka, cva, ckb, cvb, cki,
                          ya_ref, yb_ref, ka_buf, va_buf, kb_buf, vb_buf, ki_buf, sem,
                          bias_a_scr, bias_b_scr, *, n_seq, n_pages, page, ts):
    b = pl.program_id(0)
    slot = b % 2
    past = n_pages * page
    lp = past + TAIL_ROWS
    rows = H_A * ts
    caches = (cka, cva, ckb, cvb, cki)
    bufs = (ka_buf, va_buf, kb_buf, vb_buf, ki_buf)

    def page_copies(seq, slot_):
        cps = []
        for p in range(n_pages):
            pg = pt_ref[seq, p]
            for c in range(N_CACHES):
                cps.append(pltpu.make_async_copy(caches[c].at[pg], bufs[c].at[slot_, p],
                                                 sem.at[slot_, c]))
        return cps

    @pl.when(b == 0)
    def _():
        for cp in page_copies(0, 0):
            cp.start()
        dist = past + _iota((rows, lp), 0) % ts - _iota((rows, lp), 1)
        bias_a_scr[...] = _bias_by_distance(dist, lambda k: bias_a_ref[:, k:k + 1])
        bias_b_scr[...] = _bias_by_distance(dist, lambda k: bias_b_ref[:, k:k + 1])

    @pl.when(b + 1 < n_seq)
    def _():
        for cp in page_copies(b + 1, 1 - slot):
            cp.start()

    for cp in page_copies(b, slot):
        cp.wait()

    def tail(new_ref):
        new = new_ref[0]
        return jnp.concatenate([new, jnp.zeros((TAIL_ROWS - ts, new.shape[1]), F32)], axis=0)

    def scores(q, buf, new_ref, dot_page, dot_new):
        tiles = [dot_page(q, buf[slot, p]) for p in range(n_pages)]
        return jnp.concatenate(tiles + [dot_new(q, tail(new_ref))], axis=1)

    def weighted_values(p, buf, new_ref):
        pb = p.astype(BF16)
        o = _dot(pb[:, past:], tail(new_ref).astype(BF16))
        for pg in range(n_pages):
            o = o + _dot_nt(pb[:, pg * page:(pg + 1) * page], buf[slot, pg].astype(BF16))
        return o

    bf_page = lambda q, kt: _dot(q, kt.astype(BF16))
    bf_new = lambda q, k: _dot_nt(q, k.astype(BF16))

    colpos = _iota((ts, lp), 1)
    qpos = _iota((ts, lp), 0)
    valid = colpos <= past + qpos
    valid_rows = jnp.where(_iota((rows, lp), 1) <= past + _iota((rows, lp), 0) % ts, 1.0, 0.0)
    row_head = _iota((rows, 1), 0) // ts

    n_past_blk = past // MOBA_BLOCK
    pages_per_blk = MOBA_BLOCK // page
    qa = qa_ref[0] * (1.0 / math.sqrt(HEAD_DIM))
    width_a = H_A * HEAD_DIM
    q_rows = jnp.concatenate([qa] * H_A, axis=0)
    own_head_a = (_iota((rows, width_a), 1) // HEAD_DIM) == row_head
    q_bd = jnp.where(own_head_a, q_rows, 0.0)
    blk_of_lane = _iota((width_a, LANES), 1)
    kmean_t = jnp.zeros((width_a, LANES), F32)
    for n in range(n_past_blk):
        tot = ka_buf[slot, n * pages_per_blk]
        for pp in range(1, pages_per_blk):
            tot = tot + ka_buf[slot, n * pages_per_blk + pp]
        mean_n = jnp.sum(tot, axis=1, keepdims=True) * (1.0 / MOBA_BLOCK)
        kmean_t = jnp.where(blk_of_lane == n, mean_n, kmean_t)
    gate = _dot3(q_bd, kmean_t)
    blk_lane = _iota((rows, LANES), 1).astype(F32)
    gate = jnp.where(blk_lane < float(n_past_blk), gate, NEG_INF)
    selmask = jnp.zeros((rows, LANES), F32)
    for _ in range(min(MOBA_TOPK, n_past_blk)):
        mx = jnp.max(gate, axis=1, keepdims=True)
        first = jnp.min(jnp.where(gate == mx, blk_lane, float(LANES)), axis=1, keepdims=True)
        hit = blk_lane == first
        selmask = jnp.where(hit, 1.0, selmask)
        gate = jnp.where(hit, NEG_INF, gate)
    expand = jnp.where(_iota((LANES, lp), 1) // MOBA_BLOCK == _iota((LANES, lp), 0),
                       1.0, 0.0).astype(BF16)
    in_sel_blk = _dot(selmask.astype(BF16), expand)
    keep_a = jnp.where(_iota((rows, lp), 1) >= past, valid_rows, in_sel_blk) > 0.0
    s = scores(q_bd.astype(BF16), ka_buf, ka_ref, bf_page, bf_new) + bias_a_scr[...]
    p = _softmax_rows(jnp.where(keep_a, s, NEG_INF))
    o = jnp.where(own_head_a, weighted_values(p, va_buf, va_ref), 0.0)
    ya = o[0:ts]
    for h in range(1, H_A):
        ya = ya + o[h * ts:(h + 1) * ts]
    ya_ref[0] = ya

    k_sel = min(DSA_TOPK, (past + ts) // 4)
    qi_rows = _stack_heads(qi_ref[0], H_IDX, D_IDX)
    wi = wi_ref[0]
    w_rows = jnp.concatenate([wi[:, h:h + 1] for h in range(H_IDX)], axis=0)
    s_idx = scores(qi_rows, ki_buf, ki_ref, _dot3, functools.partial(_dot3, nt=True))
    contrib = jnp.maximum(s_idx, 0.0) * w_rows
    score = contrib[0:ts]
    for h in range(1, H_IDX):
        score = score + contrib[h * ts:(h + 1) * ts]
    score = jnp.where(valid, score * (1.0 / math.sqrt(D_IDX * H_IDX)), NEG_INF)
    key = _sort_key(score)

    def count_ge(cand):
        return jnp.sum(jnp.where(key >= cand, 1.0, 0.0), axis=1, keepdims=True)

    thr = _kth_largest_key(count_ge, ts, k_sel)
    need = float(k_sel) - jnp.sum(jnp.where(key > thr, 1.0, 0.0), axis=1, keepdims=True)
    tri = _strict_upper(LANES)
    carry = jnp.zeros((ts, 1), F32)
    sel_chunks = []
    for c in range(lp // LANES):
        sel_c, carry = _select_tile(key[:, c * LANES:(c + 1) * LANES], thr, need, carry, tri)
        sel_chunks.append(sel_c)
    keep = jnp.where(valid, jnp.concatenate(sel_chunks, axis=1), 0.0)
    keep_b = jnp.concatenate([keep] * H_B, axis=0) > 0.0

    qb = qb_ref[0] * (1.0 / math.sqrt(HEAD_DIM))
    qb_rows = _stack_heads(qb, H_B, HEAD_DIM)
    width_b = KV_B * HEAD_DIM
    own_kv = (_iota((rows, width_b), 1) // HEAD_DIM) == row_head // G_B
    qb_bd = jnp.where(own_kv, jnp.concatenate([qb_rows] * KV_B, axis=1), 0.0)
    s = scores(qb_bd.astype(BF16), kb_buf, kb_ref, bf_page, bf_new) + bias_b_scr[...]
    p = _softmax_rows(jnp.where(keep_b, s, NEG_INF))
    o = weighted_values(p, vb_buf, vb_ref)
    pieces = []
    for h in range(H_B):
        n = h // G_B
        pieces.append(o[h * ts:(h + 1) * ts, n * HEAD_DIM:(n + 1) * HEAD_DIM])
    yb_ref[0] = jnp.concatenate(pieces, axis=1)


def _sample_attn_t(new_rows, caches_t, page_table, rel_bias, n_seq, ts):
    n_pages = page_table.shape[1]
    page = caches_t[0].shape[2]
    past = n_pages * page
    assert past % MOBA_BLOCK == 0 and MOBA_BLOCK % page == 0 and page == LANES and ts <= SUBLANES
    assert past // MOBA_BLOCK <= LANES
    lp = past + TAIL_ROWS
    rows = H_A * ts
    bias_a_rows = jnp.repeat(rel_bias[:, :H_A].T, ts, axis=0)
    bias_b_rows = jnp.repeat(rel_bias[:, H_A:].T, ts, axis=0)
    full = lambda a: pl.BlockSpec(a.shape, lambda b, pt: (0, 0))
    seq_spec = lambda a: pl.BlockSpec((1, ts, a.shape[2]), lambda b, pt: (b, 0, 0))
    grid_spec = pltpu.PrefetchScalarGridSpec(
        num_scalar_prefetch=1,
        grid=(n_seq,),
        in_specs=[full(bias_a_rows), full(bias_b_rows)] + [seq_spec(a) for a in new_rows]
                 + [pl.BlockSpec(memory_space=pl.ANY)] * N_CACHES,
        out_specs=[pl.BlockSpec((1, ts, H_A * HEAD_DIM), lambda b, pt: (b, 0, 0)),
                   pl.BlockSpec((1, ts, H_B * HEAD_DIM), lambda b, pt: (b, 0, 0))],
        scratch_shapes=[pltpu.VMEM((2, n_pages, c.shape[1], page), F32) for c in caches_t]
                       + [pltpu.SemaphoreType.DMA((2, N_CACHES)),
                          pltpu.VMEM((rows, lp), F32), pltpu.VMEM((rows, lp), F32)],
    )
    return pl.pallas_call(
        functools.partial(_sample_attn_t_kernel, n_seq=n_seq, n_pages=n_pages, page=page, ts=ts),
        grid_spec=grid_spec,
        out_shape=[jax.ShapeDtypeStruct((n_seq, ts, H_A * HEAD_DIM), F32),
                   jax.ShapeDtypeStruct((n_seq, ts, H_B * HEAD_DIM), F32)],
        compiler_params=_params(("arbitrary",)),
        name="sample_attn",
    )(page_table, bias_a_rows, bias_b_rows, *new_rows, *caches_t)


def _attn_out_kernel(ya_ref, yb_ref, x_ref, w_ref, g_ref, b_ref, o_ref, *, alpha):
    wa = ya_ref.shape[1]
    y = (_dot(ya_ref[...].astype(BF16), w_ref[0:wa, :])
         + _dot(yb_ref[...].astype(BF16), w_ref[wa:, :]))
    o_ref[...] = _layer_norm(alpha * x_ref[...] + y, g_ref[...], b_ref[...])


def _attn_out(ya2, yb2, x2, w_out, g, b, alpha):
    n, d = x2.shape
    tm = min(ROW_TILE, n)
    row = lambda c: pl.BlockSpec((tm, c), lambda i: (i, 0))
    full = lambda a: pl.BlockSpec(a.shape, lambda i: (0, 0))
    return pl.pallas_call(
        functools.partial(_attn_out_kernel, alpha=alpha),
        grid=(n // tm,),
        in_specs=[row(ya2.shape[1]), row(yb2.shape[1]), row(d), full(w_out), full(g), full(b)],
        out_specs=row(d),
        out_shape=jax.ShapeDtypeStruct((n, d), F32),
        compiler_params=_params(("arbitrary",)),
        name="attn_out_ln",
    )(ya2, yb2, x2, w_out, g, b)


def _conv_core(x, u, u1, u2, gate_b, cw_ref, wout_ref, g_ref, b_ref, alpha):
    c = u2 * cw_ref[0:1, :] + u1 * cw_ref[1:2, :] + u * cw_ref[2:3, :]
    y = _dot((gate_b * c).astype(BF16), wout_ref[...])
    return _layer_norm(alpha * x + y, g_ref[...], b_ref[...])


def _conv_in(x, win_ref, dc):
    xb = x.astype(BF16)
    h = _dot(xb, win_ref[:, 0:dc])
    gate_b = _dot(xb, win_ref[:, dc:2 * dc])
    gate_c = _dot(xb, win_ref[:, 2 * dc:3 * dc])
    return gate_c * h, gate_b


def _conv_prompt_kernel(x_ref, win_ref, cw_ref, wout_ref, g_ref, b_ref, o_ref, st_ref, tail_scr,
                        *, alpha):
    tm, dc = x_ref.shape[0], wout_ref.shape[0]

    @pl.when(pl.program_id(1) == 0)
    def _():
        tail_scr[...] = jnp.zeros(tail_scr.shape, F32)

    x = x_ref[...]
    u, gate_b = _conv_in(x, win_ref, dc)
    ext = jnp.concatenate([tail_scr[...], u], axis=0)
    u1 = ext[SUBLANES - 1:SUBLANES - 1 + tm]
    u2 = ext[SUBLANES - 2:SUBLANES - 2 + tm]
    o_ref[...] = _conv_core(x, u, u1, u2, gate_b, cw_ref, wout_ref, g_ref, b_ref, alpha)
    tail_scr[...] = u[tm - SUBLANES:tm]
    st_ref[0] = u[tm - (CONV_W - 1):tm]


def _conv_prompt(x2, w_in, conv_w, w_out, g, b, alpha, batch, seq):
    n, d = x2.shape
    dc = w_out.shape[0]
    tm = min(ROW_TILE, seq)
    nt = seq // tm
    full = lambda a: pl.BlockSpec(a.shape, lambda bb, i: (0, 0))
    return pl.pallas_call(
        functools.partial(_conv_prompt_kernel, alpha=alpha),
        grid=(batch, nt),
        in_specs=[pl.BlockSpec((tm, d), lambda bb, i: (bb * nt + i, 0)),
                  full(w_in), full(conv_w), full(w_out), full(g), full(b)],
        out_specs=[pl.BlockSpec((tm, d), lambda bb, i: (bb * nt + i, 0)),
                   pl.BlockSpec((1, CONV_W - 1, dc), lambda bb, i: (bb, 0, 0))],
        out_shape=[jax.ShapeDtypeStruct((n, d), F32),
                   jax.ShapeDtypeStruct((batch, CONV_W - 1, dc), F32)],
        scratch_shapes=[pltpu.VMEM((SUBLANES, dc), F32)],
        compiler_params=_params(("arbitrary", "arbitrary")),
        name="conv_prompt_ln",
    )(x2, w_in, conv_w, w_out, g, b)


def _conv_sample_kernel(x_ref, p0_ref, p1_ref, win_ref, cw_ref, wout_ref, g_ref, b_ref,
                        o_ref, u_ref, *, alpha, ts):
    n, dc = x_ref.shape[0], wout_ref.shape[0]
    x = x_ref[...]
    u, gate_b = _conv_in(x, win_ref, dc)
    ext = jnp.concatenate([jnp.zeros((SUBLANES, dc), F32), u], axis=0)
    tpos = _iota((n, dc), 0) % ts
    u1 = jnp.where(tpos == 0, p1_ref[...], ext[SUBLANES - 1:SUBLANES - 1 + n])
    u2 = jnp.where(tpos == 0, p0_ref[...],
                   jnp.where(tpos == 1, p1_ref[...], ext[SUBLANES - 2:SUBLANES - 2 + n]))
    o_ref[...] = _conv_core(x, u, u1, u2, gate_b, cw_ref, wout_ref, g_ref, b_ref, alpha)
    u_ref[...] = u


def _conv_sample(x2, past0, past1, w_in, conv_w, w_out, g, b, alpha, ts):
    n, d = x2.shape
    dc = w_out.shape[0]
    tm = min(ROW_TILE, n)
    assert tm % ts == 0
    full = lambda a: pl.BlockSpec(a.shape, lambda i: (0, 0))
    row = lambda c: pl.BlockSpec((tm, c), lambda i: (i, 0))
    return pl.pallas_call(
        functools.partial(_conv_sample_kernel, alpha=alpha, ts=ts),
        grid=(n // tm,),
        in_specs=[row(d), row(dc), row(dc), full(w_in), full(conv_w), full(w_out), full(g), full(b)],
        out_specs=[row(d), row(dc)],
        out_shape=[jax.ShapeDtypeStruct((n, d), F32), jax.ShapeDtypeStruct((n, dc), F32)],
        compiler_params=_params(("arbitrary",)),
        name="conv_sample_ln",
    )(x2, past0, past1, w_in, conv_w, w_out, g, b)


DMA_UNROLL = 8
GROUP_LANE0 = 0
EXPERT_LANE0 = 32


def _first_argmax(v, lane_f):
    mx = jnp.max(v, axis=1, keepdims=True)
    first = jnp.min(jnp.where(v == mx, lane_f, float(LANES)), axis=1, keepdims=True)
    return mx, first


def _router_kernel(x_ref, w_ref, bias_ref, cin_ref, e0_ref, e1_ref, g0_ref, g1_ref, r0_ref, r1_ref,
                   cnt_ref, carry_scr):
    tm = x_ref.shape[0]

    @pl.when(pl.program_id(0) == 0)
    def _():
        carry_scr[...] = cin_ref[...]

    logits = _dot3(x_ref[...], w_ref[...]) + bias_ref[...]
    lane = _iota((tm, LANES), 1)
    lane_f = lane.astype(F32)
    gl = jnp.where((lane >= GROUP_LANE0) & (lane < GROUP_LANE0 + N_GROUPS), logits, NEG_INF)
    gmax, gfirst = _first_argmax(gl, lane_f)
    pg = jnp.exp(gl - gmax)
    g_val = jnp.max(pg / jnp.sum(pg, axis=1, keepdims=True), axis=1, keepdims=True)
    g_idx = gfirst - float(GROUP_LANE0)
    grp_of_lane = ((lane - EXPERT_LANE0) // EXPERTS_PER_GROUP).astype(F32)
    in_grp = (lane >= EXPERT_LANE0) & (lane < EXPERT_LANE0 + N_EXPERTS) & (grp_of_lane == g_idx)
    el = jnp.where(in_grp, logits, NEG_INF)
    emax, first0 = _first_argmax(el, lane_f)
    pe = jnp.exp(el - emax)
    pe = pe / jnp.sum(pe, axis=1, keepdims=True)
    hit0 = lane_f == first0
    v0 = jnp.max(jnp.where(hit0, pe, 0.0), axis=1, keepdims=True)
    _, first1 = _first_argmax(jnp.where(hit0, NEG_INF, el), lane_f)
    hit1 = lane_f == first1
    v1 = jnp.max(jnp.where(hit1, pe, 0.0), axis=1, keepdims=True)
    denom = v0 + v1
    g0_ref[...] = g_val * v0 / denom
    g1_ref[...] = g_val * v1 / denom
    e0_ref[...] = first0.astype(I32) - EXPERT_LANE0
    e1_ref[...] = first1.astype(I32) - EXPERT_LANE0
    onehot = jnp.where(hit0 | hit1, 1.0, 0.0)
    lower = jnp.where(_iota((tm, tm), 1) < _iota((tm, tm), 0), 1.0, 0.0).astype(BF16)
    before = _dot(lower, onehot.astype(BF16)) + carry_scr[...]
    r0_ref[...] = jnp.sum(jnp.where(hit0, before, 0.0), axis=1, keepdims=True).astype(I32)
    r1_ref[...] = jnp.sum(jnp.where(hit1, before, 0.0), axis=1, keepdims=True).astype(I32)
    carry_scr[...] = carry_scr[...] + jnp.sum(onehot, axis=0, keepdims=True)
    cnt_ref[...] = carry_scr[...]


def _router(x2, w_slab, b_slab, counts_in):
    n, d = x2.shape
    tm = min(ROW_TILE, n)
    full = lambda a: pl.BlockSpec(a.shape, lambda i: (0, 0))
    col = pl.BlockSpec((tm, 1), lambda i: (i, 0))
    return pl.pallas_call(
        _router_kernel,
        grid=(n // tm,),
        in_specs=[pl.BlockSpec((tm, d), lambda i: (i, 0)), full(w_slab), full(b_slab), full(counts_in)],
        out_specs=[col] * 6 + [pl.BlockSpec((1, LANES), lambda i: (0, 0))],
        out_shape=[jax.ShapeDtypeStruct((n, 1), I32), jax.ShapeDtypeStruct((n, 1), I32),
                   jax.ShapeDtypeStruct((n, 1), F32), jax.ShapeDtypeStruct((n, 1), F32),
                   jax.ShapeDtypeStruct((n, 1), I32), jax.ShapeDtypeStruct((n, 1), I32),
                   jax.ShapeDtypeStruct((1, LANES), F32)],
        scratch_shapes=[pltpu.VMEM((1, LANES), F32)],
        compiler_params=_params(("arbitrary",)),
        name="moe_router",
    )(x2, w_slab, b_slab, counts_in)


def _dispatch_kernel(e0_ref, e1_ref, r0_ref, r1_ref, start_ref, x_ref, xr_in, xr_out, sem):
    del xr_in
    tm = x_ref.shape[0]
    base = pl.program_id(0) * tm

    def copies(t):
        p0 = start_ref[e0_ref[base + t]] + r0_ref[base + t]
        p1 = start_ref[e1_ref[base + t]] + r1_ref[base + t]
        src = x_ref.at[pl.ds(t, 1)]
        return (pltpu.make_async_copy(src, xr_out.at[pl.ds(p0, 1)], sem),
                pltpu.make_async_copy(src, xr_out.at[pl.ds(p1, 1)], sem))

    def start(t, _):
        for cp in copies(t):
            cp.start()
        return 0

    def wait(t, _):
        for cp in copies(t):
            cp.wait()
        return 0

    lax.fori_loop(0, tm, start, 0, unroll=DMA_UNROLL)
    lax.fori_loop(0, tm, wait, 0, unroll=DMA_UNROLL)


def _dispatch(route, start_pad, x2, xr):
    n, d = x2.shape
    tm = min(ROW_TILE, n)
    grid_spec = pltpu.PrefetchScalarGridSpec(
        num_scalar_prefetch=5,
        grid=(n // tm,),
        in_specs=[pl.BlockSpec((tm, d), lambda i, *_: (i, 0)), pl.BlockSpec(memory_space=pl.ANY)],
        out_specs=pl.BlockSpec(memory_space=pl.ANY),
        scratch_shapes=[pltpu.SemaphoreType.DMA(())],
    )
    return pl.pallas_call(
        _dispatch_kernel,
        grid_spec=grid_spec,
        out_shape=jax.ShapeDtypeStruct(xr.shape, F32),
        input_output_aliases={6: 0},
        compiler_params=_params(("arbitrary",)),
        name="moe_dispatch",
    )(*route, start_pad, x2, xr)


def _ffn_kernel(be_ref, nu_ref, x_ref, wg_ref, wu_ref, wd_ref, o_ref, wg_s, wu_s, wd_s):
    i = pl.program_id(0)

    @pl.when(i < nu_ref[0])
    def _():
        @pl.when((i == 0) | (be_ref[i] != be_ref[jnp.maximum(i - 1, 0)]))
        def _():
            wg_s[...] = wg_ref[0, 0].astype(BF16)
            wu_s[...] = wu_ref[0, 0].astype(BF16)
            wd_s[...] = wd_ref[0, 0].astype(BF16)

        xb = x_ref[...].astype(BF16)
        gate = _dot(xb, wg_s[...])
        hidden = gate * jax.nn.sigmoid(gate) * _dot(xb, wu_s[...])
        o_ref[...] = _dot(hidden.astype(BF16), wd_s[...])

    @pl.when(i >= nu_ref[0])
    def _():
        o_ref[...] = jnp.zeros(o_ref.shape, F32)


def _grouped_ffn(block_expert, n_used, xr, w_gate, w_up, w_down, layer):
    rows, d = xr.shape
    de = w_gate.shape[3]
    nblk = rows // FFN_ROWS
    grid_spec = pltpu.PrefetchScalarGridSpec(
        num_scalar_prefetch=2,
        grid=(nblk,),
        in_specs=[pl.BlockSpec((FFN_ROWS, d), lambda i, be, nu: (jnp.minimum(i, nu[0] - 1), 0)),
                  pl.BlockSpec((1, 1, d, de), lambda i, be, nu: (layer, be[i], 0, 0)),
                  pl.BlockSpec((1, 1, d, de), lambda i, be, nu: (layer, be[i], 0, 0)),
                  pl.BlockSpec((1, 1, de, d), lambda i, be, nu: (layer, be[i], 0, 0))],
        out_specs=pl.BlockSpec((FFN_ROWS, d), lambda i, be, nu: (i, 0)),
        scratch_shapes=[pltpu.VMEM((d, de), BF16), pltpu.VMEM((d, de), BF16), pltpu.VMEM((de, d), BF16)],
    )
    return pl.pallas_call(
        _ffn_kernel,
        grid_spec=grid_spec,
        out_shape=jax.ShapeDtypeStruct((rows, d), F32),
        compiler_params=_params(("arbitrary",)),
        name="moe_ffn",
    )(block_expert, n_used, xr, w_gate, w_up, w_down)


def _combine_kernel(e0_ref, e1_ref, r0_ref, r1_ref, start_ref, x_ref, g0_ref, g1_ref, lg_ref, lb_ref,
                    yr_ref, o_ref, ybuf, sem, *, alpha):
    tm = x_ref.shape[0]
    base = pl.program_id(0) * tm

    def copies(t):
        p0 = start_ref[e0_ref[base + t]] + r0_ref[base + t]
        p1 = start_ref[e1_ref[base + t]] + r1_ref[base + t]
        return (pltpu.make_async_copy(yr_ref.at[pl.ds(p0, 1)], ybuf.at[0, pl.ds(t, 1)], sem),
                pltpu.make_async_copy(yr_ref.at[pl.ds(p1, 1)], ybuf.at[1, pl.ds(t, 1)], sem))

    def start(t, _):
        for cp in copies(t):
            cp.start()
        return 0

    def wait(t, _):
        for cp in copies(t):
            cp.wait()
        return 0

    lax.fori_loop(0, tm, start, 0, unroll=DMA_UNROLL)
    lax.fori_loop(0, tm, wait, 0, unroll=DMA_UNROLL)
    y = ybuf[0] * g0_ref[...] + ybuf[1] * g1_ref[...]
    o_ref[...] = _layer_norm(alpha * x_ref[...] + y, lg_ref[...], lb_ref[...])


def _combine(route, start_pad, x2, g0, g1, ln_g, ln_b, yr, alpha):
    n, d = x2.shape
    tm = min(ROW_TILE, n)
    col = pl.BlockSpec((tm, 1), lambda i, *_: (i, 0))
    vec = pl.BlockSpec((1, d), lambda i, *_: (0, 0))
    grid_spec = pltpu.PrefetchScalarGridSpec(
        num_scalar_prefetch=5,
        grid=(n // tm,),
        in_specs=[pl.BlockSpec((tm, d), lambda i, *_: (i, 0)), col, col, vec, vec,
                  pl.BlockSpec(memory_space=pl.ANY)],
        out_specs=pl.BlockSpec((tm, d), lambda i, *_: (i, 0)),
        scratch_shapes=[pltpu.VMEM((2, tm, d), F32), pltpu.SemaphoreType.DMA(())],
    )
    return pl.pallas_call(
        functools.partial(_combine_kernel, alpha=alpha),
        grid_spec=grid_spec,
        out_shape=jax.ShapeDtypeStruct((n, d), F32),
        compiler_params=_params(("arbitrary",)),
        name="moe_combine_ln",
    )(*route, start_pad, x2, g0, g1, ln_g, ln_b, yr)


def _hier_moe_ln(xs, w_rg, b_rg, w_re, b_re, w_gate, w_up, w_down, layer, ln_g, ln_b, alpha):
    d = xs[0].shape[1]
    w_slab = jnp.zeros((d, LANES), F32)
    w_slab = w_slab.at[:, GROUP_LANE0:GROUP_LANE0 + N_GROUPS].set(w_rg)
    w_slab = w_slab.at[:, EXPERT_LANE0:EXPERT_LANE0 + N_EXPERTS].set(w_re)
    b_slab = jnp.zeros((1, LANES), F32)
    b_slab = b_slab.at[0, GROUP_LANE0:GROUP_LANE0 + N_GROUPS].set(b_rg)
    b_slab = b_slab.at[0, EXPERT_LANE0:EXPERT_LANE0 + N_EXPERTS].set(b_re)

    counts = jnp.zeros((1, LANES), F32)
    routes, gates = [], []
    for x2 in xs:
        e0, e1, g0, g1, r0, r1, counts = _router(x2, w_slab, b_slab, counts)
        routes.append(tuple(a.reshape(-1) for a in (e0, e1, r0, r1)))
        gates.append((g0, g1))

    cnt = counts[0, EXPERT_LANE0:EXPERT_LANE0 + N_EXPERTS].astype(I32)
    padded = (cnt + FFN_ROWS - 1) // FFN_ROWS * FFN_ROWS
    end_pad = jnp.cumsum(padded)
    start_pad = (end_pad - padded).astype(I32)
    n_assign = 2 * sum(x2.shape[0] for x2 in xs)
    nblk = -(-n_assign // FFN_ROWS) + N_EXPERTS
    blk_start = jnp.arange(nblk, dtype=I32) * FFN_ROWS
    block_expert = jnp.minimum(
        jnp.sum((end_pad[None, :] <= blk_start[:, None]).astype(I32), axis=1), N_EXPERTS - 1)
    n_used = (end_pad[-1:] // FFN_ROWS).astype(I32)

    xr = jnp.zeros((nblk * FFN_ROWS, d), F32)
    for x2, route in zip(xs, routes):
        xr = _dispatch(route, start_pad, x2, xr)
    yr = _grouped_ffn(block_expert, n_used, xr, w_gate, w_up, w_down, layer)
    return [_combine(route, start_pad, x2, g0, g1, ln_g, ln_b, yr, alpha)
            for x2, route, (g0, g1) in zip(xs, routes, gates)]


def _stack(arrays):
    return arrays[0][None] if len(arrays) == 1 else jnp.stack(arrays)


def kernel(x_prompt, x_sample, cache_k_a, cache_v_a, cache_k_b, cache_v_b, cache_k_idx, state_conv, page_table, rel_bias, w_attn_in, w_attn_out, w_conv_in, conv_w, w_conv_out, w_router_group, b_router_group, w_router_expert, b_router_expert, w_exp_gate, w_exp_up, w_exp_down, ln_g, ln_b):
    depth = ln_g.shape[0]
    alpha = float((2 * depth) ** 0.25)
    bp, tp, d = x_prompt.shape
    bs, ts, _ = x_sample.shape
    assert ts >= CONV_W - 1
    xp = x_prompt.reshape(bp * tp, d)
    xs = x_sample.reshape(bs * ts, d)
    attn_p, attn_s, conv_p, conv_s = [], [], [], []
    kv_cols = (1, 2, 4, 5, 7)
    for layer in range(depth):
        i = layer // 2
        g_mix, b_mix = ln_g[layer, 0][None], ln_b[layer, 0][None]
        if layer % 2 == 0:
            w_in = jnp.pad(w_attn_in[i], ((0, 0), (0, _ATTN_IN_PAD - _ATTN_IN))).astype(BF16)
            w_out = w_attn_out[i].astype(BF16)
            qa, ka, qb, kb, qi, ki, kat, vat, kbt, vbt, kit, wit = _attn_inproj_prompt(
                xp, w_in, _transposed_proj_weight(w_attn_in[i]), bp, tp)
            proj_s = _attn_inproj(xs, w_in)
            ya_p = _moba_prompt_t(qa, ka, vat, rel_bias, bp, tp)
            yb_p = _dsa_prompt_t(qb, qi, wit, kb, vbt, ki, rel_bias, bp, tp)
            caches_t = [jnp.moveaxis(c[i], 1, -1).reshape(c.shape[1], -1, c.shape[2])
                        for c in (cache_k_a, cache_v_a, cache_k_b, cache_v_b, cache_k_idx)]
            ya_s, yb_s = _sample_attn_t([a.reshape(bs, ts, -1) for a in proj_s], caches_t,
                                        page_table, rel_bias, bs, ts)
            xp = _attn_out(ya_p, yb_p, xp, w_out, g_mix, b_mix, alpha)
            xs = _attn_out(ya_s.reshape(bs * ts, -1), yb_s.reshape(bs * ts, -1), xs, w_out,
                           g_mix, b_mix, alpha)
            attn_p.append([kat, vat, kbt, vbt, kit])
            attn_s.append([proj_s[c] for c in kv_cols])
        else:
            w_in = w_conv_in[i].astype(BF16)
            w_out = w_conv_out[i].astype(BF16)
            xp, st_p = _conv_prompt(xp, w_in, conv_w[i], w_out, g_mix, b_mix, alpha, bp, tp)
            past = state_conv[i]
            xs, u_s = _conv_sample(xs, jnp.repeat(past[:, 0], ts, axis=0),
                                   jnp.repeat(past[:, 1], ts, axis=0),
                                   w_in, conv_w[i], w_out, g_mix, b_mix, alpha, ts)
            conv_p.append(st_p)
            conv_s.append(u_s.reshape(bs, ts, -1)[:, ts - (CONV_W - 1):])
        xp, xs = _hier_moe_ln([xp, xs], w_router_group[layer], b_router_group[layer],
                              w_router_expert[layer], b_router_expert[layer],
                              w_exp_gate, w_exp_up, w_exp_down, layer,
                              ln_g[layer, 1][None], ln_b[layer, 1][None], alpha)

    def rows(group, j, b, t, shape):
        return _stack([layer_rows[j].reshape((b, t) + shape) for layer_rows in group])

    def rows_t(group, j, b, t, shape):
        return _stack([jnp.moveaxis(layer_rows[j].reshape((b,) + shape + (t,)), -1, 1)
                       for layer_rows in group])

    head_shapes = ((H_A, HEAD_DIM), (H_A, HEAD_DIM), (KV_B, HEAD_DIM), (KV_B, HEAD_DIM), (D_IDX,))
    return ((xp.reshape(bp, tp, d), xs.reshape(bs, ts, d))
            + tuple(rows_t(attn_p, j, bp, tp, head_shapes[j]) for j in range(5))
            + (_stack(conv_p),)
            + tuple(rows(attn_s, j, bs, ts, head_shapes[j]) for j in range(5))
            + (_stack(conv_s),))
```

```python
import functools
import math

import numpy as np
import jax
import jax.numpy as jnp
from jax import lax
from jax.experimental import pallas as pl
from jax.experimental.pallas import tpu as pltpu

F32 = jnp.float32
BF16 = jnp.bfloat16
I32 = jnp.int32

HEAD_DIM = 64
H_A = 8
H_B = 8
KV_B = 2
G_B = H_B // KV_B
H_IDX = 8
D_IDX = 64
MOBA_BLOCK = 256
MOBA_TOPK = 3
DSA_TOPK = 256
NUM_BUCKETS = 32
MAX_DISTANCE = 128
CONV_W = 3
N_GROUPS = 4
EXPERTS_PER_GROUP = 8
N_EXPERTS = N_GROUPS * EXPERTS_PER_GROUP
LN_EPS = 1e-5
ATTN_COLS = (H_A * HEAD_DIM, H_A * HEAD_DIM, H_A * HEAD_DIM,
             H_B * HEAD_DIM, KV_B * HEAD_DIM, KV_B * HEAD_DIM,
             H_IDX * D_IDX, D_IDX, H_IDX)

LANES = 128
SUBLANES = 8
VMEM_LIMIT = 56 * 1024 * 1024

ATT_TILE = MOBA_BLOCK
ROW_TILE = 256
FFN_ROWS = 256
NEG_INF = float("-inf")


def _bucket_lows():
    n = np.arange(0, 4 * MAX_DISTANCE)
    max_exact = NUM_BUCKETS // 2
    nf = np.maximum(n, 1).astype(np.float32)
    large = max_exact + (np.log(nf / max_exact) / math.log(MAX_DISTANCE / max_exact)
                         * (NUM_BUCKETS - max_exact)).astype(np.int32)
    bucket = np.where(n < max_exact, n, np.minimum(large, NUM_BUCKETS - 1))
    assert np.all(np.diff(bucket) >= 0) and bucket[-1] == NUM_BUCKETS - 1
    lows = [int(np.argmax(bucket == k)) for k in range(NUM_BUCKETS)]
    assert all(bucket[lo] == k for k, lo in enumerate(lows))
    return tuple(lows)


BUCKET_LOWS = _bucket_lows()
FAR_DIST = BUCKET_LOWS[-1]


def _params(sem):
    return pltpu.CompilerParams(dimension_semantics=sem, vmem_limit_bytes=VMEM_LIMIT)


def _dot(a, b):
    return jnp.dot(a, b, preferred_element_type=F32)


def _dot_nt(a, b):
    return lax.dot_general(a, b, (((1,), (1,)), ((), ())), preferred_element_type=F32)


def _split_bf16(a):
    hi = a.astype(BF16)
    lo = (a - hi.astype(F32)).astype(BF16)
    return hi, lo


def _dot3(a, b, nt=False):
    ah, al = _split_bf16(a)
    bh, bl = _split_bf16(b)
    d = _dot_nt if nt else _dot
    return d(ah, bh) + (d(ah, bl) + d(al, bh))


def _iota(shape, dim):
    return lax.broadcasted_iota(I32, shape, dim)


def _layer_norm(z, g, b):
    mu = jnp.mean(z, axis=-1, keepdims=True)
    zc = z - mu
    var = jnp.mean(zc * zc, axis=-1, keepdims=True)
    return zc * lax.rsqrt(var + LN_EPS) * g + b


def _bias_by_distance(dist, bias_of_bucket):
    acc = jnp.zeros(dist.shape, F32) + bias_of_bucket(0)
    for k in range(1, NUM_BUCKETS):
        acc = jnp.where(dist >= BUCKET_LOWS[k], bias_of_bucket(k), acc)
    return acc


def _sort_key(x):
    bits = lax.bitcast_convert_type(x + 0.0, I32)
    return bits ^ (lax.shift_right_arithmetic(bits, 31) & 0x7FFFFFFF)


_ATTN_OFFS = tuple(int(v) for v in np.cumsum((0,) + ATTN_COLS[:-1]))
_ATTN_IN = sum(ATTN_COLS)
_ATTN_IN_PAD = -(-_ATTN_IN // LANES) * LANES
_TAIL_OFF = _ATTN_OFFS[7]


def _attn_inproj_kernel(x_ref, w_ref, qa, ka, va, qb, kb, vb, qi, ki, wi):
    xb = x_ref[...].astype(BF16)
    for o, off, wd in zip((qa, ka, va, qb, kb, vb, qi), _ATTN_OFFS, ATTN_COLS):
        o[...] = _dot(xb, w_ref[:, off:off + wd])
    tail = _dot(xb, w_ref[:, _TAIL_OFF:_ATTN_IN_PAD])
    ki[...] = tail[:, :D_IDX]
    wi[...] = tail[:, D_IDX:D_IDX + H_IDX]


def _attn_inproj(x2, w_pad):
    n, d = x2.shape
    tm = min(ROW_TILE, n)
    outs = tuple(jax.ShapeDtypeStruct((n, c), F32) for c in ATTN_COLS)
    return pl.pallas_call(
        _attn_inproj_kernel,
        grid=(n // tm,),
        in_specs=[pl.BlockSpec((tm, d), lambda i: (i, 0)),
                  pl.BlockSpec((d, _ATTN_IN_PAD), lambda i: (0, 0))],
        out_specs=tuple(pl.BlockSpec((tm, c), lambda i: (i, 0)) for c in ATTN_COLS),
        out_shape=outs,
        compiler_params=_params(("arbitrary",)),
        name="attn_inproj",
    )(x2, w_pad)


def _build_bias_tiles(bias_ref, scr, head0, n_heads):
    row = _iota((ATT_TILE, ATT_TILE), 0)
    col = _iota((ATT_TILE, ATT_TILE), 1)
    for t in range(2):
        dist = row - col + (ATT_TILE if t == 0 else 0)
        for h in range(n_heads):
            scr[h, t] = _bias_by_distance(dist, lambda k, h=h: bias_ref[k, head0 + h])


def _online_update(s, v, m, l, acc):
    m_new = jnp.maximum(m, jnp.max(s, axis=1, keepdims=True))
    m_safe = jnp.where(m_new == NEG_INF, 0.0, m_new)
    alpha = jnp.exp(m - m_safe)
    p = jnp.exp(s - m_safe)
    l = alpha * l + jnp.sum(p, axis=1, keepdims=True)
    acc = alpha * acc + _dot(p.astype(BF16), v)
    return m_new, l, acc


def _moba_prompt_kernel(bias_ref, q_ref, k_ref, v_ref, o_ref, kmean_scr, bias_scr, *, nblk):
    b, hp, i = pl.program_id(0), pl.program_id(1), pl.program_id(2)
    heads_per_step = LANES // HEAD_DIM

    @pl.when((b == 0) & (hp == 0) & (i == 0))
    def _():
        _build_bias_tiles(bias_ref, bias_scr, 0, H_A)

    @pl.when(i == 0)
    def _():
        for n in range(nblk):
            kmean_scr[n:n + 1, :] = jnp.mean(
                k_ref[n * MOBA_BLOCK:(n + 1) * MOBA_BLOCK, :], axis=0, keepdims=True)

    q = q_ref[...] * (1.0 / math.sqrt(HEAD_DIM))
    row = _iota((ATT_TILE, ATT_TILE), 0)
    col = _iota((ATT_TILE, ATT_TILE), 1)
    blk_lane = _iota((ATT_TILE, nblk), 1).astype(F32)
    outs = []
    for hh in range(heads_per_step):
        h = hp * heads_per_step + hh
        cs = slice(hh * HEAD_DIM, (hh + 1) * HEAD_DIM)
        qh = q[:, cs]
        gate = _dot3(qh, kmean_scr[:, cs], nt=True)
        gate = jnp.where(blk_lane < i.astype(F32), gate, NEG_INF)
        selmask = jnp.zeros((ATT_TILE, nblk), F32)
        for _ in range(min(MOBA_TOPK, nblk)):
            mx = jnp.max(gate, axis=1, keepdims=True)
            first = jnp.min(jnp.where(gate == mx, blk_lane, float(nblk)), axis=1, keepdims=True)
            hit = blk_lane == first
            selmask = jnp.where(hit & (mx > NEG_INF), 1.0, selmask)
            gate = jnp.where(hit, NEG_INF, gate)

        qb = qh.astype(BF16)
        far_bias = bias_ref[NUM_BUCKETS - 1, h]
        start = pl.multiple_of(i * ATT_TILE, ATT_TILE)
        s = _dot_nt(qb, k_ref[pl.ds(start, ATT_TILE), cs].astype(BF16)) + bias_scr[h, 1]
        s = jnp.where(col <= row, s, NEG_INF)
        m0 = jnp.max(s, axis=1, keepdims=True)
        p = jnp.exp(s - m0)
        l0 = jnp.sum(p, axis=1, keepdims=True)
        acc0 = _dot(p.astype(BF16), v_ref[pl.ds(start, ATT_TILE), cs].astype(BF16))

        def body(j, carry, qb=qb, cs=cs, h=h, selmask=selmask, far_bias=far_bias):
            m, l, acc = carry
            ks = pl.multiple_of(j * ATT_TILE, ATT_TILE)
            s = _dot_nt(qb, k_ref[pl.ds(ks, ATT_TILE), cs].astype(BF16))
            s = s + jnp.where(j == i - 1, bias_scr[h, 0], far_bias)
            sel = jnp.sum(jnp.where(blk_lane == j.astype(F32), selmask, 0.0), axis=1, keepdims=True)
            s = jnp.where(sel > 0.0, s, NEG_INF)
            return _online_update(s, v_ref[pl.ds(ks, ATT_TILE), cs].astype(BF16), m, l, acc)

        _, l, acc = lax.fori_loop(0, i, body, (m0, l0, acc0))
        outs.append(acc / l)
    o_ref[...] = jnp.concatenate(outs, axis=1)


def _moba_prompt(q2, k2, v2, rel_bias, batch, seq):
    nq = seq // ATT_TILE
    nblk = seq // MOBA_BLOCK
    width = q2.shape[1]
    return pl.pallas_call(
        functools.partial(_moba_prompt_kernel, nblk=nblk),
        grid=(batch, width // LANES, nq),
        in_specs=[pl.BlockSpec(memory_space=pltpu.SMEM),
                  pl.BlockSpec((ATT_TILE, LANES), lambda b, hp, i: (b * nq + i, hp)),
                  pl.BlockSpec((seq, LANES), lambda b, hp, i: (b, hp)),
                  pl.BlockSpec((seq, LANES), lambda b, hp, i: (b, hp))],
        out_specs=pl.BlockSpec((ATT_TILE, LANES), lambda b, hp, i: (b * nq + i, hp)),
        out_shape=jax.ShapeDtypeStruct((batch * seq, width), F32),
        scratch_shapes=[pltpu.VMEM((nblk, LANES), F32),
                        pltpu.VMEM((H_A, 2, ATT_TILE, ATT_TILE), F32)],
        compiler_params=_params(("arbitrary", "arbitrary", "arbitrary")),
        name="moba_prompt",
    )(rel_bias, q2, k2, v2)


INT_MIN = -2 ** 31


def _strict_upper(n):
    return jnp.where(_iota((n, n), 0) < _iota((n, n), 1), 1.0, 0.0).astype(BF16)


def _select_tile(key, thr, need, carry, tri):
    eqf = jnp.where(key == thr, 1.0, 0.0)
    prefix = _dot(eqf.astype(BF16), tri) + carry
    self_ = jnp.where(key > thr, 1.0, jnp.where(prefix < need, eqf, 0.0))
    return self_, carry + jnp.sum(eqf, axis=1, keepdims=True)


def _kth_largest_key(count_ge, rows, k_sel, unrolled=False):
    kk = float(k_sel)
    zero = jnp.zeros((rows, 1), I32)
    thr = jnp.where(count_ge(zero) >= kk, zero, jnp.full((rows, 1), INT_MIN, I32))

    def bit_body(t, thr):
        cand = thr | lax.shift_left(jnp.int32(1), 30 - t)
        return jnp.where(count_ge(cand) >= kk, cand, thr)

    if unrolled:
        for t in range(31):
            cand = thr | (1 << (30 - t))
            thr = jnp.where(count_ge(cand) >= kk, cand, thr)
        return thr
    return lax.fori_loop(0, 31, bit_body, thr)


def _dsa_prompt_kernel(bias_ref, qb_ref, qi_ref, wi_ref, kb_ref, vb_ref, ki_ref, o_ref,
                       key_scr, bias_scr, m_scr, l_scr, acc_scr, *, k_sel):
    b, i = pl.program_id(0), pl.program_id(1)

    @pl.when((b == 0) & (i == 0))
    def _():
        _build_bias_tiles(bias_ref, bias_scr, H_A, H_B)

    row = _iota((ATT_TILE, ATT_TILE), 0)
    col = _iota((ATT_TILE, ATT_TILE), 1)
    idx_scale = 1.0 / math.sqrt(D_IDX * H_IDX)

    qi = qi_ref[...]
    qi_h = [_split_bf16(qi[:, h * D_IDX:(h + 1) * D_IDX]) for h in range(H_IDX)]
    wi = wi_ref[...]
    w_h = [wi[:, h:h + 1] for h in range(H_IDX)]

    def score_body(j, _):
        ks = pl.multiple_of(j * ATT_TILE, ATT_TILE)
        kt_hi, kt_lo = _split_bf16(ki_ref[pl.ds(ks, ATT_TILE), :])
        acc = jnp.zeros((ATT_TILE, ATT_TILE), F32)
        for h in range(H_IDX):
            q_hi, q_lo = qi_h[h]
            s = _dot_nt(q_hi, kt_hi) + (_dot_nt(q_hi, kt_lo) + _dot_nt(q_lo, kt_hi))
            acc = acc + jnp.maximum(s, 0.0) * w_h[h]
        score = acc * idx_scale
        causal = (col + (j - i) * ATT_TILE) <= row
        key_scr[j] = _sort_key(jnp.where(causal, score, NEG_INF))
        return 0

    lax.fori_loop(0, i + 1, score_body, 0)

    def count_ge(cand):
        def body(j, cnt):
            return cnt + jnp.sum(jnp.where(key_scr[j] >= cand, 1.0, 0.0), axis=1, keepdims=True)
        return lax.fori_loop(0, i + 1, body, jnp.zeros((ATT_TILE, 1), F32))

    thr = _kth_largest_key(count_ge, ATT_TILE, k_sel)

    def count_gt_body(j, cnt):
        return cnt + jnp.sum(jnp.where(key_scr[j] > thr, 1.0, 0.0), axis=1, keepdims=True)

    need = float(k_sel) - lax.fori_loop(0, i + 1, count_gt_body, jnp.zeros((ATT_TILE, 1), F32))

    q = qb_ref[...] * (1.0 / math.sqrt(HEAD_DIM))
    q_h = [q[:, h * HEAD_DIM:(h + 1) * HEAD_DIM].astype(BF16) for h in range(H_B)]
    tri = _strict_upper(ATT_TILE)
    m_scr[...] = jnp.full(m_scr.shape, NEG_INF, F32)
    l_scr[...] = jnp.zeros(l_scr.shape, F32)
    acc_scr[...] = jnp.zeros(acc_scr.shape, F32)

    def attend_body(j, carry):
        ks = pl.multiple_of(j * ATT_TILE, ATT_TILE)
        self_, carry = _select_tile(key_scr[j], thr, need, carry, tri)
        causal = (col + (j - i) * ATT_TILE) <= row
        keep = jnp.where(causal, self_, 0.0) > 0.0
        near = jnp.maximum(j - (i - 1), 0)
        for n in range(KV_B):
            cs = slice(n * HEAD_DIM, (n + 1) * HEAD_DIM)
            kt = kb_ref[pl.ds(ks, ATT_TILE), cs].astype(BF16)
            vt = vb_ref[pl.ds(ks, ATT_TILE), cs].astype(BF16)
            for g in range(G_B):
                h = n * G_B + g
                bias = jnp.where(j >= i - 1, bias_scr[h, near], bias_ref[NUM_BUCKETS - 1, H_A + h])
                s = jnp.where(keep, _dot_nt(q_h[h], kt) + bias, NEG_INF)
                m, l, acc = _online_update(s, vt, m_scr[h], l_scr[h], acc_scr[h])
                m_scr[h] = m
                l_scr[h] = l
                acc_scr[h] = acc
        return carry

    lax.fori_loop(0, i + 1, attend_body, jnp.zeros((ATT_TILE, 1), F32))
    o_ref[...] = jnp.concatenate([acc_scr[h] / l_scr[h] for h in range(H_B)], axis=1)


def _dsa_prompt(qb2, qi2, wi2, kb2, vb2, ki2, rel_bias, batch, seq):
    nq = seq // ATT_TILE
    k_sel = min(DSA_TOPK, seq // 4)
    qmap = lambda b, i: (b * nq + i, 0)
    kmap = lambda b, i: (b, 0)
    return pl.pallas_call(
        functools.partial(_dsa_prompt_kernel, k_sel=k_sel),
        grid=(batch, nq),
        in_specs=[pl.BlockSpec(memory_space=pltpu.SMEM),
                  pl.BlockSpec((ATT_TILE, H_B * HEAD_DIM), qmap),
                  pl.BlockSpec((ATT_TILE, H_IDX * D_IDX), qmap),
                  pl.BlockSpec((ATT_TILE, H_IDX), qmap),
                  pl.BlockSpec((seq, KV_B * HEAD_DIM), kmap),
                  pl.BlockSpec((seq, KV_B * HEAD_DIM), kmap),
                  pl.BlockSpec((seq, D_IDX), kmap)],
        out_specs=pl.BlockSpec((ATT_TILE, H_B * HEAD_DIM), qmap),
        out_shape=jax.ShapeDtypeStruct((batch * seq, H_B * HEAD_DIM), F32),
        scratch_shapes=[pltpu.VMEM((nq, ATT_TILE, ATT_TILE), I32),
                        pltpu.VMEM((H_B, 2, ATT_TILE, ATT_TILE), F32),
                        pltpu.VMEM((H_B, ATT_TILE, 1), F32),
                        pltpu.VMEM((H_B, ATT_TILE, 1), F32),
                        pltpu.VMEM((H_B, ATT_TILE, HEAD_DIM), F32)],
        compiler_params=_params(("arbitrary", "arbitrary")),
        name="dsa_prompt",
    )(rel_bias, qb2, qi2, wi2, kb2, vb2, ki2)


_T_GROUPS = (1, 2, 4, 5, 7, 8)
_T_ROWS = tuple(ATTN_COLS[g] for g in _T_GROUPS)
_T_OFFS = tuple(int(v) for v in np.cumsum((0,) + _T_ROWS[:-1]))
BF16_ROWS = 2 * SUBLANES
_T_TOTAL = -(-sum(_T_ROWS) // BF16_ROWS) * BF16_ROWS
_N_GROUPS_NORMAL = (0, 1, 3, 4, 6, 7)


def _attn_inproj_prompt_kernel(x_ref, w_ref, wt_ref, qa, ka, qb, kb, qi, ki,
                               kat, vat, kbt, vbt, kit, wit):
    xb = x_ref[...].astype(BF16)
    for o, g in zip((qa, ka, qb, kb, qi), _N_GROUPS_NORMAL[:5]):
        o[...] = _dot(xb, w_ref[:, _ATTN_OFFS[g]:_ATTN_OFFS[g] + ATTN_COLS[g]])
    ki[...] = _dot(xb, w_ref[:, _TAIL_OFF:_ATTN_IN_PAD])[:, :D_IDX]
    for o, off, rows in zip((kat, vat, kbt, vbt, kit, wit), _T_OFFS, _T_ROWS):
        padded = -(-rows // BF16_ROWS) * BF16_ROWS
        o[0] = _dot_nt(wt_ref[off:off + padded, :], xb)[:rows]


def _transposed_proj_weight(w_in):
    cols = [w_in[:, _ATTN_OFFS[g]:_ATTN_OFFS[g] + ATTN_COLS[g]] for g in _T_GROUPS]
    wt = jnp.concatenate(cols, axis=1).T
    return jnp.pad(wt, ((0, _T_TOTAL - wt.shape[0]), (0, 0))).astype(BF16)


def _attn_inproj_prompt(x2, w_pad, w_t, batch, seq):
    n, d = x2.shape
    tm = min(ROW_TILE, seq)
    nt = seq // tm
    normal = [ATTN_COLS[g] for g in _N_GROUPS_NORMAL]
    return pl.pallas_call(
        _attn_inproj_prompt_kernel,
        grid=(batch, nt),
        in_specs=[pl.BlockSpec((tm, d), lambda b, i: (b * nt + i, 0)),
                  pl.BlockSpec(w_pad.shape, lambda b, i: (0, 0)),
                  pl.BlockSpec(w_t.shape, lambda b, i: (0, 0))],
        out_specs=[pl.BlockSpec((tm, c), lambda b, i: (b * nt + i, 0)) for c in normal]
                  + [pl.BlockSpec((1, r, tm), lambda b, i: (b, 0, i)) for r in _T_ROWS],
        out_shape=[jax.ShapeDtypeStruct((n, c), F32) for c in normal]
                  + [jax.ShapeDtypeStruct((batch, r, seq), F32) for r in _T_ROWS],
        compiler_params=_params(("arbitrary", "arbitrary")),
        name="attn_inproj_prompt",
    )(x2, w_pad, w_t)


def _build_bias_tiles_t(bias_ref, scr, head0, n_heads):
    key = _iota((ATT_TILE, ATT_TILE), 0)
    qry = _iota((ATT_TILE, ATT_TILE), 1)
    for t in range(2):
        dist = qry - key + (ATT_TILE if t == 0 else 0)
        for h in range(n_heads):
            scr[h, t] = _bias_by_distance(dist, lambda k, h=h: bias_ref[k, head0 + h])


def _online_update_t(s, vt, m, l, acc):
    m_new = jnp.maximum(m, jnp.max(s, axis=0, keepdims=True))
    m_safe = jnp.where(m_new == NEG_INF, 0.0, m_new)
    alpha = jnp.exp(m - m_safe)
    p = jnp.exp(s - m_safe)
    l = alpha * l + jnp.sum(p, axis=0, keepdims=True)
    acc = alpha * acc + _dot(vt, p.astype(BF16))
    return m_new, l, acc


def _softmax_fold(state, s, vt, shift=None):
    m_old, l_old, acc_old = state
    tile_max = jnp.max(s, axis=0, keepdims=True)
    if shift is not None:
        tile_max = tile_max + shift
    m_new = jnp.maximum(m_old, tile_max)
    m_safe = jnp.where(m_new == NEG_INF, 0.0, m_new)
    alpha = jnp.exp(m_old - m_safe)
    p = jnp.exp(s - (m_safe if shift is None else m_safe - shift))
    return (m_new, alpha * l_old + jnp.sum(p, axis=0, keepdims=True),
            alpha * acc_old + _dot(vt, p.astype(BF16)))


def _store_states(scrs, states):
    for h, state in enumerate(states):
        for scr, value in zip(scrs, state):
            scr[h] = value


def _load_states(scrs, n_heads):
    return tuple(tuple(scr[h] for scr in scrs) for h in range(n_heads))


def _moba_prompt_kernel8(bias_ref, q_ref, k_ref, vt_ref, o_ref, kmean_scr, bias_scr, sel_scr,
                         m_scr, l_scr, acc_scr, *, nblk):
    b, i = pl.program_id(0), pl.program_id(1)

    @pl.when((b == 0) & (i == 0))
    def _():
        _build_bias_tiles_t(bias_ref, bias_scr, 0, H_A)

    @pl.when(i == 0)
    def _():
        for n in range(nblk):
            kmean_scr[n:n + 1, :] = jnp.mean(
                k_ref[n * MOBA_BLOCK:(n + 1) * MOBA_BLOCK, :], axis=0, keepdims=True)

    q = q_ref[...] * (1.0 / math.sqrt(HEAD_DIM))
    key = _iota((ATT_TILE, ATT_TILE), 0)
    qry = _iota((ATT_TILE, ATT_TILE), 1)
    blk = _iota((nblk, ATT_TILE), 0).astype(F32)
    heads = [slice(h * HEAD_DIM, (h + 1) * HEAD_DIM) for h in range(H_A)]
    q_h = []
    for h in range(H_A):
        qh = q[:, heads[h]]
        q_h.append(qh.astype(BF16))
        gate = _dot3(kmean_scr[:, heads[h]], qh, nt=True)
        gate = jnp.where(blk < i.astype(F32), gate, NEG_INF)
        selmask = jnp.zeros((nblk, ATT_TILE), F32)
        for _ in range(min(MOBA_TOPK, nblk)):
            mx = jnp.max(gate, axis=0, keepdims=True)
            first = jnp.min(jnp.where(gate == mx, blk, float(nblk)), axis=0, keepdims=True)
            hit = blk == first
            selmask = jnp.where(hit & (mx > NEG_INF), 1.0, selmask)
            gate = jnp.where(hit, NEG_INF, gate)
        sel_scr[h] = jnp.where(selmask > 0.0, 0.0, NEG_INF)

    def k_tile(ks, h):
        return k_ref[pl.ds(ks, ATT_TILE), heads[h]].astype(BF16)

    def vt_tile(ks, h):
        return vt_ref[0, heads[h], pl.ds(ks, ATT_TILE)].astype(BF16)

    scrs = (m_scr, l_scr, acc_scr)
    empty = (jnp.full((1, ATT_TILE), NEG_INF, F32), jnp.zeros((1, ATT_TILE), F32),
             jnp.zeros((HEAD_DIM, ATT_TILE), F32))

    start = pl.multiple_of(i * ATT_TILE, ATT_TILE)
    own = []
    for h in range(H_A):
        s = _dot_nt(k_tile(start, h), q_h[h]) + bias_scr[h, 1]
        own.append(_softmax_fold(empty, jnp.where(key <= qry, s, NEG_INF), vt_tile(start, h)))
    _store_states(scrs, own)

    @pl.when(i >= 1)
    def _():
        ks = pl.multiple_of((i - 1) * ATT_TILE, ATT_TILE)
        states = _load_states(scrs, H_A)
        _store_states(scrs, [
            _softmax_fold(states[h], _dot_nt(k_tile(ks, h), q_h[h])
                          + (bias_scr[h, 0] + sel_scr[h, pl.ds(i - 1, 1), :]), vt_tile(ks, h))
            for h in range(H_A)])

    def far_body(j, states):
        ks = pl.multiple_of(j * ATT_TILE, ATT_TILE)
        return tuple(
            _softmax_fold(states[h], _dot_nt(k_tile(ks, h), q_h[h]) + sel_scr[h, pl.ds(j, 1), :],
                          vt_tile(ks, h), shift=bias_ref[NUM_BUCKETS - 1, h])
            for h in range(H_A))

    states = lax.fori_loop(0, i - 1, far_body, _load_states(scrs, H_A))
    o_ref[...] = jnp.concatenate([acc / l for _, l, acc in states], axis=0).T


def _moba_prompt8(q2, k2, vt3, rel_bias, batch, seq):
    assert ATT_TILE >= FAR_DIST
    nq = seq // ATT_TILE
    nblk = seq // MOBA_BLOCK
    width = q2.shape[1]
    return pl.pallas_call(
        functools.partial(_moba_prompt_kernel8, nblk=nblk),
        grid=(batch, nq),
        in_specs=[pl.BlockSpec(memory_space=pltpu.SMEM),
                  pl.BlockSpec((ATT_TILE, width), lambda b, i: (b * nq + i, 0)),
                  pl.BlockSpec((seq, width), lambda b, i: (b, 0)),
                  pl.BlockSpec((1, width, seq), lambda b, i: (b, 0, 0))],
        out_specs=pl.BlockSpec((ATT_TILE, width), lambda b, i: (b * nq + i, 0)),
        out_shape=jax.ShapeDtypeStruct((batch * seq, width), F32),
        scratch_shapes=[pltpu.VMEM((nblk, width), F32),
                        pltpu.VMEM((H_A, 2, ATT_TILE, ATT_TILE), F32),
                        pltpu.VMEM((H_A, nblk, ATT_TILE), F32),
                        pltpu.VMEM((H_A, 1, ATT_TILE), F32),
                        pltpu.VMEM((H_A, 1, ATT_TILE), F32),
                        pltpu.VMEM((H_A, HEAD_DIM, ATT_TILE), F32)],
        compiler_params=_params(("arbitrary", "arbitrary")),
        name="moba_prompt",
    )(rel_bias, q2, k2, vt3)


def _moba_prompt_t_kernel(bias_ref, q_ref, k_ref, vt_ref, o_ref, kmean_scr, bias_scr, *, nblk):
    b, hp, i = pl.program_id(0), pl.program_id(1), pl.program_id(2)
    heads_per_step = LANES // HEAD_DIM

    @pl.when((b == 0) & (hp == 0) & (i == 0))
    def _():
        _build_bias_tiles_t(bias_ref, bias_scr, 0, H_A)

    @pl.when(i == 0)
    def _():
        for n in range(nblk):
            kmean_scr[n:n + 1, :] = jnp.mean(
                k_ref[n * MOBA_BLOCK:(n + 1) * MOBA_BLOCK, :], axis=0, keepdims=True)

    q = q_ref[...] * (1.0 / math.sqrt(HEAD_DIM))
    key = _iota((ATT_TILE, ATT_TILE), 0)
    qry = _iota((ATT_TILE, ATT_TILE), 1)
    blk = _iota((nblk, ATT_TILE), 0).astype(F32)
    start = pl.multiple_of(i * ATT_TILE, ATT_TILE)
    outs = []
    for hh in range(heads_per_step):
        h = hp * heads_per_step + hh
        cs = slice(hh * HEAD_DIM, (hh + 1) * HEAD_DIM)
        qh = q[:, cs]
        gate = _dot3(kmean_scr[:, cs], qh, nt=True)
        gate = jnp.where(blk < i.astype(F32), gate, NEG_INF)
        selmask = jnp.zeros((nblk, ATT_TILE), F32)
        for _ in range(min(MOBA_TOPK, nblk)):
            mx = jnp.max(gate, axis=0, keepdims=True)
            first = jnp.min(jnp.where(gate == mx, blk, float(nblk)), axis=0, keepdims=True)
            hit = blk == first
            selmask = jnp.where(hit & (mx > NEG_INF), 1.0, selmask)
            gate = jnp.where(hit, NEG_INF, gate)

        qb = qh.astype(BF16)
        far_bias = bias_ref[NUM_BUCKETS - 1, h]
        rows = slice(hh * HEAD_DIM, (hh + 1) * HEAD_DIM)
        s = _dot_nt(k_ref[pl.ds(start, ATT_TILE), cs].astype(BF16), qb) + bias_scr[h, 1]
        s = jnp.where(key <= qry, s, NEG_INF)
        m0 = jnp.max(s, axis=0, keepdims=True)
        p = jnp.exp(s - m0)
        l0 = jnp.sum(p, axis=0, keepdims=True)
        acc0 = _dot(vt_ref[0, rows, pl.ds(start, ATT_TILE)].astype(BF16), p.astype(BF16))

        def body(j, carry, qb=qb, cs=cs, rows=rows, h=h, selmask=selmask, far_bias=far_bias):
            m, l, acc = carry
            ks = pl.multiple_of(j * ATT_TILE, ATT_TILE)
            s = _dot_nt(k_ref[pl.ds(ks, ATT_TILE), cs].astype(BF16), qb)
            s = s + jnp.where(j == i - 1, bias_scr[h, 0], far_bias)
            sel = jnp.sum(jnp.where(blk == j.astype(F32), selmask, 0.0), axis=0, keepdims=True)
            s = jnp.where(sel > 0.0, s, NEG_INF)
            vt = vt_ref[0, rows, pl.ds(ks, ATT_TILE)].astype(BF16)
            return _online_update_t(s, vt, m, l, acc)

        _, l, acc = lax.fori_loop(0, i, body, (m0, l0, acc0))
        outs.append(acc / l)
    o_ref[...] = jnp.concatenate(outs, axis=0).T


def _moba_prompt_t(q2, k2, vt3, rel_bias, batch, seq):
    nq = seq // ATT_TILE
    nblk = seq // MOBA_BLOCK
    width = q2.shape[1]
    return pl.pallas_call(
        functools.partial(_moba_prompt_t_kernel, nblk=nblk),
        grid=(batch, width // LANES, nq),
        in_specs=[pl.BlockSpec(memory_space=pltpu.SMEM),
                  pl.BlockSpec((ATT_TILE, LANES), lambda b, hp, i: (b * nq + i, hp)),
                  pl.BlockSpec((seq, LANES), lambda b, hp, i: (b, hp)),
                  pl.BlockSpec((1, LANES, seq), lambda b, hp, i: (b, hp, 0))],
        out_specs=pl.BlockSpec((ATT_TILE, LANES), lambda b, hp, i: (b * nq + i, hp)),
        out_shape=jax.ShapeDtypeStruct((batch * seq, width), F32),
        scratch_shapes=[pltpu.VMEM((nblk, LANES), F32),
                        pltpu.VMEM((H_A, 2, ATT_TILE, ATT_TILE), F32)],
        compiler_params=_params(("arbitrary", "arbitrary", "arbitrary")),
        name="moba_prompt",
    )(rel_bias, q2, k2, vt3)


def _strict_lower(n):
    return jnp.where(_iota((n, n), 1) < _iota((n, n), 0), 1.0, 0.0).astype(BF16)


def _dsa_prompt_t_kernel(bias_ref, qb_ref, qi_ref, wit_ref, kb_ref, vbt_ref, ki_ref, o_ref,
                         key_scr, bias_scr, m_scr, l_scr, acc_scr, ties_scr, *, k_sel):
    b, i = pl.program_id(0), pl.program_id(1)

    @pl.when((b == 0) & (i == 0))
    def _():
        _build_bias_tiles_t(bias_ref, bias_scr, H_A, H_B)

    key_ix = _iota((ATT_TILE, ATT_TILE), 0)
    qry_ix = _iota((ATT_TILE, ATT_TILE), 1)
    idx_scale = 1.0 / math.sqrt(D_IDX * H_IDX)

    qi = qi_ref[...]
    q_cat = []
    for h in range(H_IDX):
        hi, lo = _split_bf16(qi[:, h * D_IDX:(h + 1) * D_IDX])
        q_cat.append(jnp.concatenate([hi, lo, hi], axis=1))
    wit = wit_ref[0]
    w_row = [wit[h:h + 1, :] for h in range(H_IDX)]

    def score_body(j, _):
        ks = pl.multiple_of(j * ATT_TILE, ATT_TILE)
        hi, lo = _split_bf16(ki_ref[pl.ds(ks, ATT_TILE), :])
        k_cat = jnp.concatenate([hi, hi, lo], axis=1)
        acc = jnp.zeros((ATT_TILE, ATT_TILE), F32)
        for h in range(H_IDX):
            acc = acc + jnp.maximum(_dot_nt(k_cat, q_cat[h]), 0.0) * w_row[h]
        causal = (key_ix + (j - i) * ATT_TILE) <= qry_ix
        key_scr[j] = _sort_key(jnp.where(causal, acc * idx_scale, NEG_INF))
        return 0

    lax.fori_loop(0, i + 1, score_body, 0)

    def count_ge(cand):
        def body(j, cnt):
            return cnt + jnp.sum(jnp.where(key_scr[j] >= cand, 1.0, 0.0), axis=0, keepdims=True)
        return lax.fori_loop(0, i + 1, body, jnp.zeros((1, ATT_TILE), F32))

    kk = float(k_sel)
    zero = jnp.zeros((1, ATT_TILE), I32)
    thr0 = jnp.where(count_ge(zero) >= kk, zero, jnp.full((1, ATT_TILE), INT_MIN, I32))

    def bit_body(t, thr):
        cand = thr | lax.shift_left(jnp.int32(1), 30 - t)
        return jnp.where(count_ge(cand) >= kk, cand, thr)

    thr = lax.fori_loop(0, 31, bit_body, thr0)

    def above_and_ties_body(j, carry):
        above, ties = carry
        key = key_scr[j]
        ties_scr[j] = ties
        above = above + jnp.sum(jnp.where(key > thr, 1.0, 0.0), axis=0, keepdims=True)
        ties = ties + jnp.sum(jnp.where(key == thr, 1.0, 0.0), axis=0, keepdims=True)
        return above, ties

    zero_row = jnp.zeros((1, ATT_TILE), F32)
    above, _ = lax.fori_loop(0, i + 1, above_and_ties_body, (zero_row, zero_row))
    need = kk - above

    q = qb_ref[...] * (1.0 / math.sqrt(HEAD_DIM))
    q_h = [q[:, h * HEAD_DIM:(h + 1) * HEAD_DIM].astype(BF16) for h in range(H_B)]
    lower = _strict_lower(ATT_TILE)
    scrs = (m_scr, l_scr, acc_scr)

    def additive_mask(j):
        key = key_scr[j]
        eqf = jnp.where(key == thr, 1.0, 0.0)
        earlier_ties = _dot(lower, eqf.astype(BF16)) + ties_scr[j]
        member = jnp.where(key > thr, 1.0, jnp.where(earlier_ties < need, eqf, 0.0))
        causal = (key_ix + (j - i) * ATT_TILE) <= qry_ix
        return jnp.where(causal, jnp.where(member > 0.0, 0.0, NEG_INF), NEG_INF)

    def attend(j, bias_tile, states):
        ks = pl.multiple_of(j * ATT_TILE, ATT_TILE)
        negmask = additive_mask(j)
        out = []
        for n in range(KV_B):
            cs = slice(n * HEAD_DIM, (n + 1) * HEAD_DIM)
            kt = kb_ref[pl.ds(ks, ATT_TILE), cs].astype(BF16)
            vt = vbt_ref[0, cs, pl.ds(ks, ATT_TILE)].astype(BF16)
            for g in range(G_B):
                h = n * G_B + g
                tile = bias_tile(h)
                if tile is None:
                    out.append(_softmax_fold(states[h], _dot_nt(kt, q_h[h]) + negmask, vt,
                                             shift=bias_ref[NUM_BUCKETS - 1, H_A + h]))
                else:
                    out.append(_softmax_fold(states[h], _dot_nt(kt, q_h[h]) + (tile + negmask), vt))
        return tuple(out)

    empty = (jnp.full((1, ATT_TILE), NEG_INF, F32), zero_row, jnp.zeros((HEAD_DIM, ATT_TILE), F32))
    _store_states(scrs, attend(i, lambda h: bias_scr[h, 1], (empty,) * H_B))

    @pl.when(i >= 1)
    def _():
        _store_states(scrs, attend(i - 1, lambda h: bias_scr[h, 0], _load_states(scrs, H_B)))

    states = lax.fori_loop(0, i - 1, lambda j, st: attend(j, lambda h: None, st),
                           _load_states(scrs, H_B))
    o_ref[...] = jnp.concatenate([acc / l for _, l, acc in states], axis=0).T


def _dsa_prompt_t(qb2, qi2, wit3, kb2, vbt3, ki2, rel_bias, batch, seq):
    nq = seq // ATT_TILE
    k_sel = min(DSA_TOPK, seq // 4)
    qmap = lambda b, i: (b * nq + i, 0)
    kmap = lambda b, i: (b, 0)
    return pl.pallas_call(
        functools.partial(_dsa_prompt_t_kernel, k_sel=k_sel),
        grid=(batch, nq),
        in_specs=[pl.BlockSpec(memory_space=pltpu.SMEM),
                  pl.BlockSpec((ATT_TILE, H_B * HEAD_DIM), qmap),
                  pl.BlockSpec((ATT_TILE, H_IDX * D_IDX), qmap),
                  pl.BlockSpec((1, H_IDX, ATT_TILE), lambda b, i: (b, 0, i)),
                  pl.BlockSpec((seq, KV_B * HEAD_DIM), kmap),
                  pl.BlockSpec((1, KV_B * HEAD_DIM, seq), lambda b, i: (b, 0, 0)),
                  pl.BlockSpec((seq, D_IDX), kmap)],
        out_specs=pl.BlockSpec((ATT_TILE, H_B * HEAD_DIM), qmap),
        out_shape=jax.ShapeDtypeStruct((batch * seq, H_B * HEAD_DIM), F32),
        scratch_shapes=[pltpu.VMEM((nq, ATT_TILE, ATT_TILE), I32),
                        pltpu.VMEM((H_B, 2, ATT_TILE, ATT_TILE), F32),
                        pltpu.VMEM((H_B, 1, ATT_TILE), F32),
                        pltpu.VMEM((H_B, 1, ATT_TILE), F32),
                        pltpu.VMEM((H_B, HEAD_DIM, ATT_TILE), F32),
                        pltpu.VMEM((nq, 1, ATT_TILE), F32)],
        compiler_params=_params(("arbitrary", "arbitrary")),
        name="dsa_prompt",
    )(rel_bias, qb2, qi2, wit3, kb2, vbt3, ki2)


N_CACHES = 5
TAIL_ROWS = LANES


def _softmax_rows(s):
    m = jnp.max(s, axis=1, keepdims=True)
    p = jnp.exp(s - m)
    return p / jnp.sum(p, axis=1, keepdims=True)


def _stack_heads(x, n_heads, width):
    return jnp.concatenate([x[:, h * width:(h + 1) * width] for h in range(n_heads)], axis=0)


def _sample_attn_kernel(pt_ref, bias_a_ref, bias_b_ref, qa_ref, ka_ref, va_ref, qb_ref, kb_ref,
                        vb_ref, qi_ref, ki_ref, wi_ref, cka, cva, ckb, cvb, cki,
                        ya_ref, yb_ref, ka_buf, va_buf, kb_buf, vb_buf, ki_buf, sem,
                        bias_a_scr, bias_b_scr, *, n_seq, n_pages, page, ts):
    b = pl.program_id(0)
    slot = b % 2
    past = n_pages * page
    lp = past + TAIL_ROWS
    rows = H_A * ts
    caches = (cka, cva, ckb, cvb, cki)
    bufs = (ka_buf, va_buf, kb_buf, vb_buf, ki_buf)

    def page_copies(seq, slot_):
        cps = []
        for p in range(n_pages):
            pg = pt_ref[seq, p]
            for c in range(N_CACHES):
                cps.append(pltpu.make_async_copy(
                    caches[c].at[pg], bufs[c].at[slot_, pl.ds(p * page, page)], sem.at[slot_, c]))
        return cps

    @pl.when(b == 0)
    def _():
        for cp in page_copies(0, 0):
            cp.start()
        dist = past + _iota((rows, lp), 0) % ts - _iota((rows, lp), 1)
        bias_a_scr[...] = _bias_by_distance(dist, lambda k: bias_a_ref[:, k:k + 1])
        bias_b_scr[...] = _bias_by_distance(dist, lambda k: bias_b_ref[:, k:k + 1])

    @pl.when(b + 1 < n_seq)
    def _():
        for cp in page_copies(b + 1, 1 - slot):
            cp.start()

    for buf, new in zip(bufs, (ka_ref, va_ref, kb_ref, vb_ref, ki_ref)):
        buf[slot, past:lp, :] = jnp.zeros((TAIL_ROWS, buf.shape[2]), F32)
        buf[slot, past:past + ts, :] = new[0]

    for cp in page_copies(b, slot):
        cp.wait()

    colpos = _iota((ts, lp), 1)
    qpos = _iota((ts, lp), 0)
    valid = colpos <= past + qpos
    valid_rows = jnp.where(_iota((rows, lp), 1) <= past + _iota((rows, lp), 0) % ts, 1.0, 0.0)
    row_head = _iota((rows, 1), 0) // ts

    n_past_blk = past // MOBA_BLOCK
    qa = qa_ref[0] * (1.0 / math.sqrt(HEAD_DIM))
    width_a = H_A * HEAD_DIM
    q_rows = jnp.concatenate([qa] * H_A, axis=0)
    own_head_a = (_iota((rows, width_a), 1) // HEAD_DIM) == row_head
    q_bd = jnp.where(own_head_a, q_rows, 0.0)
    kmean = jnp.concatenate(
        [jnp.mean(ka_buf[slot, n * MOBA_BLOCK:(n + 1) * MOBA_BLOCK, :], axis=0, keepdims=True)
         for n in range(n_past_blk)], axis=0)
    gate = _dot3(q_bd, kmean, nt=True)
    blk_lane = _iota((rows, n_past_blk), 1).astype(F32)
    selmask = jnp.zeros((rows, n_past_blk), F32)
    for _ in range(min(MOBA_TOPK, n_past_blk)):
        mx = jnp.max(gate, axis=1, keepdims=True)
        first = jnp.min(jnp.where(gate == mx, blk_lane, float(n_past_blk)), axis=1, keepdims=True)
        hit = blk_lane == first
        selmask = jnp.where(hit, 1.0, selmask)
        gate = jnp.where(hit, NEG_INF, gate)
    expand = jnp.where(_iota((n_past_blk, lp), 1) // MOBA_BLOCK == _iota((n_past_blk, lp), 0),
                       1.0, 0.0).astype(BF16)
    in_sel_blk = _dot(selmask.astype(BF16), expand)
    keep_a = jnp.where(_iota((rows, lp), 1) >= past, valid_rows, in_sel_blk) > 0.0
    s = _dot_nt(q_bd.astype(BF16), ka_buf[slot].astype(BF16)) + bias_a_scr[...]
    p = _softmax_rows(jnp.where(keep_a, s, NEG_INF))
    o = jnp.where(own_head_a, _dot(p.astype(BF16), va_buf[slot].astype(BF16)), 0.0)
    ya = o[0:ts]
    for h in range(1, H_A):
        ya = ya + o[h * ts:(h + 1) * ts]
    ya_ref[0] = ya

    k_sel = min(DSA_TOPK, (past + ts) // 4)
    qi_rows = _stack_heads(qi_ref[0], H_IDX, D_IDX)
    wi = wi_ref[0]
    w_rows = jnp.concatenate([wi[:, h:h + 1] for h in range(H_IDX)], axis=0)
    contrib = jnp.maximum(_dot3(qi_rows, ki_buf[slot], nt=True), 0.0) * w_rows
    score = contrib[0:ts]
    for h in range(1, H_IDX):
        score = score + contrib[h * ts:(h + 1) * ts]
    score = jnp.where(valid, score * (1.0 / math.sqrt(D_IDX * H_IDX)), NEG_INF)
    key = _sort_key(score)

    def count_ge(cand):
        return jnp.sum(jnp.where(key >= cand, 1.0, 0.0), axis=1, keepdims=True)

    thr = _kth_largest_key(count_ge, ts, k_sel, unrolled=True)
    need = float(k_sel) - jnp.sum(jnp.where(key > thr, 1.0, 0.0), axis=1, keepdims=True)
    tri = _strict_upper(LANES)
    carry = jnp.zeros((ts, 1), F32)
    sel_chunks = []
    for c in range(lp // LANES):
        sel_c, carry = _select_tile(key[:, c * LANES:(c + 1) * LANES], thr, need, carry, tri)
        sel_chunks.append(sel_c)
    keep = jnp.where(valid, jnp.concatenate(sel_chunks, axis=1), 0.0)
    keep_b = jnp.concatenate([keep] * H_B, axis=0) > 0.0

    qb = qb_ref[0] * (1.0 / math.sqrt(HEAD_DIM))
    qb_rows = _stack_heads(qb, H_B, HEAD_DIM)
    width_b = KV_B * HEAD_DIM
    own_kv = (_iota((rows, width_b), 1) // HEAD_DIM) == row_head // G_B
    qb_bd = jnp.where(own_kv, jnp.concatenate([qb_rows] * KV_B, axis=1), 0.0)
    s = _dot_nt(qb_bd.astype(BF16), kb_buf[slot].astype(BF16)) + bias_b_scr[...]
    p = _softmax_rows(jnp.where(keep_b, s, NEG_INF))
    o = _dot(p.astype(BF16), vb_buf[slot].astype(BF16))
    pieces = []
    for h in range(H_B):
        n = h // G_B
        pieces.append(o[h * ts:(h + 1) * ts, n * HEAD_DIM:(n + 1) * HEAD_DIM])
    yb_ref[0] = jnp.concatenate(pieces, axis=1)


def _sample_attn(new_rows, caches, page_table, rel_bias, n_seq, ts):
    n_pages = page_table.shape[1]
    page = caches[0].shape[1]
    past = n_pages * page
    assert past % MOBA_BLOCK == 0 and ts <= SUBLANES
    lp = past + TAIL_ROWS
    rows = H_A * ts
    bias_a_rows = jnp.repeat(rel_bias[:, :H_A].T, ts, axis=0)
    bias_b_rows = jnp.repeat(rel_bias[:, H_A:].T, ts, axis=0)
    full = lambda a: pl.BlockSpec(a.shape, lambda b, pt: (0, 0))
    seq_spec = lambda a: pl.BlockSpec((1, ts, a.shape[2]), lambda b, pt: (b, 0, 0))
    grid_spec = pltpu.PrefetchScalarGridSpec(
        num_scalar_prefetch=1,
        grid=(n_seq,),
        in_specs=[full(bias_a_rows), full(bias_b_rows)] + [seq_spec(a) for a in new_rows]
                 + [pl.BlockSpec(memory_space=pl.ANY)] * N_CACHES,
        out_specs=[pl.BlockSpec((1, ts, H_A * HEAD_DIM), lambda b, pt: (b, 0, 0)),
                   pl.BlockSpec((1, ts, H_B * HEAD_DIM), lambda b, pt: (b, 0, 0))],
        scratch_shapes=[pltpu.VMEM((2, lp, c.shape[2]), F32) for c in caches]
                       + [pltpu.SemaphoreType.DMA((2, N_CACHES)),
                          pltpu.VMEM((rows, lp), F32), pltpu.VMEM((rows, lp), F32)],
    )
    return pl.pallas_call(
        functools.partial(_sample_attn_kernel, n_seq=n_seq, n_pages=n_pages, page=page, ts=ts),
        grid_spec=grid_spec,
        out_shape=[jax.ShapeDtypeStruct((n_seq, ts, H_A * HEAD_DIM), F32),
                   jax.ShapeDtypeStruct((n_seq, ts, H_B * HEAD_DIM), F32)],
        compiler_params=_params(("arbitrary",)),
        name="sample_attn",
    )(page_table, bias_a_rows, bias_b_rows, *new_rows, *caches)


def _sample_attn_t_kernel(pt_ref, bias_a_ref, bias_b_ref, qa_ref, ka_ref, va_ref, qb_ref, kb_ref,
                          vb_ref, qi_ref, ki_ref, wi_ref, cka, cva, ckb, cvb, cki,
                          ya_ref, yb_ref, ka_buf, va_buf, kb_buf, vb_buf, ki_buf, sem,
                          bias_a_scr, bias_b_scr, *, n_seq, n_pages, page, ts):
    b = pl.program_id(0)
    slot = b % 2
    past = n_pages * page
    lp = past + TAIL_ROWS
    rows = H_A * ts
    caches = (cka, cva, ckb, cvb, cki)
    bufs = (ka_buf, va_buf, kb_buf, vb_buf, ki_buf)

    def page_copies(seq, slot_):
        cps = []
        for p in range(n_pages):
            pg = pt_ref[seq, p]
            for c in range(N_CACHES):
                cps.append(pltpu.make_async_copy(caches[c].at[pg], bufs[c].at[slot_, p],
                                                 sem.at[slot_, c]))
        return cps

    @pl.when(b == 0)
    def _():
        for cp in page_copies(0, 0):
            cp.start()
        dist = past + _iota((rows, lp), 0) % ts - _iota((rows, lp), 1)
        bias_a_scr[...] = _bias_by_distance(dist, lambda k: bias_a_ref[:, k:k + 1])
        bias_b_scr[...] = _bias_by_distance(dist, lambda k: bias_b_ref[:, k:k + 1])

    @pl.when(b + 1 < n_seq)
    def _():
        for cp in page_copies(b + 1, 1 - slot):
            cp.start()

    for cp in page_copies(b, slot):
        cp.wait()

    def tail(new_ref):
        new = new_ref[0]
        return jnp.concatenate([new, jnp.zeros((TAIL_ROWS - ts, new.shape[1]), F32)], axis=0)

    def scores(q, buf, new_ref, dot_page, dot_new):
        tiles = [dot_page(q, buf[slot, p]) for p in range(n_pages)]
        return jnp.concatenate(tiles + [dot_new(q, tail(new_ref))], axis=1)

    def weighted_values(p, buf, new_ref):
        pb = p.astype(BF16)
        o = _dot(pb[:, past:], tail(new_ref).astype(BF16))
        for pg in range(n_pages):
            o = o + _dot_nt(pb[:, pg * page:(pg + 1) * page], buf[slot, pg].astype(BF16))
        return o

    bf_page = lambda q, kt: _dot(q, kt.astype(BF16))
    bf_new = lambda q, k: _dot_nt(q, k.astype(BF16))

    colpos = _iota((ts, lp), 1)
    qpos = _iota((ts, lp), 0)
    valid = colpos <= past + qpos
    valid_rows = jnp.where(_iota((rows, lp), 1) <= past + _iota((rows, lp), 0) % ts, 1.0, 0.0)
    row_head = _iota((rows, 1), 0) // ts

    n_past_blk = past // MOBA_BLOCK
    pages_per_blk = MOBA_BLOCK // page
    qa = qa_ref[0] * (1.0 / math.sqrt(HEAD_DIM))
    width_a = H_A * HEAD_DIM
    q_rows = jnp.concatenate([qa] * H_A, axis=0)
    own_head_a = (_iota((rows, width_a), 1) // HEAD_DIM) == row_head
    q_bd = jnp.where(own_head_a, q_rows, 0.0)
    blk_of_lane = _iota((width_a, LANES), 1)
    kmean_t = jnp.zeros((width_a, LANES), F32)
    for n in range(n_past_blk):
        tot = ka_buf[slot, n * pages_per_blk]
        for pp in range(1, pages_per_blk):
            tot = tot + ka_buf[slot, n * pages_per_blk + pp]
        mean_n = jnp.sum(tot, axis=1, keepdims=True) * (1.0 / MOBA_BLOCK)
        kmean_t = jnp.where(blk_of_lane == n, mean_n, kmean_t)
    gate = _dot3(q_bd, kmean_t)
    blk_lane = _iota((rows, LANES), 1).astype(F32)
    gate = jnp.where(blk_lane < float(n_past_blk), gate, NEG_INF)
    selmask = jnp.zeros((rows, LANES), F32)
    for _ in range(min(MOBA_TOPK, n_past_blk)):
        mx = jnp.max(gate, axis=1, keepdims=True)
        first = jnp.min(jnp.where(gate == mx, blk_lane, float(LANES)), axis=1, keepdims=True)
        hit = blk_lane == first
        selmask = jnp.where(hit, 1.0, selmask)
        gate = jnp.where(hit, NEG_INF, gate)
    expand = jnp.where(_iota((LANES, lp), 1) // MOBA_BLOCK == _iota((LANES, lp), 0),
                       1.0, 0.0).astype(BF16)
    in_sel_blk = _dot(selmask.astype(BF16), expand)
    keep_a = jnp.where(_iota((rows, lp), 1) >= past, valid_rows, in_sel_blk) > 0.0
    s = scores(q_bd.astype(BF16), ka_buf, ka_ref, bf_page, bf_new) + bias_a_scr[...]
    p = _softmax_rows(jnp.where(keep_a, s, NEG_INF))
    o = jnp.where(own_head_a, weighted_values(p, va_buf, va_ref), 0.0)
    ya = o[0:ts]
    for h in range(1, H_A):
        ya = ya + o[h * ts:(h + 1) * ts]
    ya_ref[0] = ya

    k_sel = min(DSA_TOPK, (past + ts) // 4)
    qi_rows = _stack_heads(qi_ref[0], H_IDX, D_IDX)
    wi = wi_ref[0]
    w_rows = jnp.concatenate([wi[:, h:h + 1] for h in range(H_IDX)], axis=0)
    s_idx = scores(qi_rows, ki_buf, ki_ref, _dot3, functools.partial(_dot3, nt=True))
    contrib = jnp.maximum(s_idx, 0.0) * w_rows
    score = contrib[0:ts]
    for h in range(1, H_IDX):
        score = score + contrib[h * ts:(h + 1) * ts]
    score = jnp.where(valid, score * (1.0 / math.sqrt(D_IDX * H_IDX)), NEG_INF)
    key = _sort_key(score)

    def count_ge(cand):
        return jnp.sum(jnp.where(key >= cand, 1.0, 0.0), axis=1, keepdims=True)

    thr = _kth_largest_key(count_ge, ts, k_sel, unrolled=True)
    need = float(k_sel) - jnp.sum(jnp.where(key > thr, 1.0, 0.0), axis=1, keepdims=True)
    tri = _strict_upper(LANES)
    carry = jnp.zeros((ts, 1), F32)
    sel_chunks = []
    for c in range(lp // LANES):
        sel_c, carry = _select_tile(key[:, c * LANES:(c + 1) * LANES], thr, need, carry, tri)
        sel_chunks.append(sel_c)
    keep = jnp.where(valid, jnp.concatenate(sel_chunks, axis=1), 0.0)
    keep_b = jnp.concatenate([keep] * H_B, axis=0) > 0.0

    qb = qb_ref[0] * (1.0 / math.sqrt(HEAD_DIM))
    qb_rows = _stack_heads(qb, H_B, HEAD_DIM)
    width_b = KV_B * HEAD_DIM
    own_kv = (_iota((rows, width_b), 1) // HEAD_DIM) == row_head // G_B
    qb_bd = jnp.where(own_kv, jnp.concatenate([qb_rows] * KV_B, axis=1), 0.0)
    s = scores(qb_bd.astype(BF16), kb_buf, kb_ref, bf_page, bf_new) + bias_b_scr[...]
    p = _softmax_rows(jnp.where(keep_b, s, NEG_INF))
    o = weighted_values(p, vb_buf, vb_ref)
    pieces = []
    for h in range(H_B):
        n = h // G_B
        pieces.append(o[h * ts:(h + 1) * ts, n * HEAD_DIM:(n + 1) * HEAD_DIM])
    yb_ref[0] = jnp.concatenate(pieces, axis=1)


def _sample_attn_t(new_rows, caches_t, page_table, rel_bias, n_seq, ts):
    n_pages = page_table.shape[1]
    page = caches_t[0].shape[2]
    past = n_pages * page
    assert past % MOBA_BLOCK == 0 and MOBA_BLOCK % page == 0 and page == LANES and ts <= SUBLANES
    assert past // MOBA_BLOCK <= LANES
    lp = past + TAIL_ROWS
    rows = H_A * ts
    bias_a_rows = jnp.repeat(rel_bias[:, :H_A].T, ts, axis=0)
    bias_b_rows = jnp.repeat(rel_bias[:, H_A:].T, ts, axis=0)
    full = lambda a: pl.BlockSpec(a.shape, lambda b, pt: (0, 0))
    seq_spec = lambda a: pl.BlockSpec((1, ts, a.shape[2]), lambda b, pt: (b, 0, 0))
    grid_spec = pltpu.PrefetchScalarGridSpec(
        num_scalar_prefetch=1,
        grid=(n_seq,),
        in_specs=[full(bias_a_rows), full(bias_b_rows)] + [seq_spec(a) for a in new_rows]
                 + [pl.BlockSpec(memory_space=pl.ANY)] * N_CACHES,
        out_specs=[pl.BlockSpec((1, ts, H_A * HEAD_DIM), lambda b, pt: (b, 0, 0)),
                   pl.BlockSpec((1, ts, H_B * HEAD_DIM), lambda b, pt: (b, 0, 0))],
        scratch_shapes=[pltpu.VMEM((2, n_pages, c.shape[1], page), F32) for c in caches_t]
                       + [pltpu.SemaphoreType.DMA((2, N_CACHES)),
                          pltpu.VMEM((rows, lp), F32), pltpu.VMEM((rows, lp), F32)],
    )
    return pl.pallas_call(
        functools.partial(_sample_attn_t_kernel, n_seq=n_seq, n_pages=n_pages, page=page, ts=ts),
        grid_spec=grid_spec,
        out_shape=[jax.ShapeDtypeStruct((n_seq, ts, H_A * HEAD_DIM), F32),
                   jax.ShapeDtypeStruct((n_seq, ts, H_B * HEAD_DIM), F32)],
        compiler_params=_params(("arbitrary",)),
        name="sample_attn",
    )(page_table, bias_a_rows, bias_b_rows, *new_rows, *caches_t)


def _attn_out_kernel(ya_ref, yb_ref, x_ref, w_ref, g_ref, b_ref, o_ref, *, alpha):
    wa = ya_ref.shape[1]
    y = (_dot(ya_ref[...].astype(BF16), w_ref[0:wa, :])
         + _dot(yb_ref[...].astype(BF16), w_ref[wa:, :]))
    o_ref[...] = _layer_norm(alpha * x_ref[...] + y, g_ref[...], b_ref[...])


def _attn_out(ya2, yb2, x2, w_out, g, b, alpha):
    n, d = x2.shape
    tm = min(ROW_TILE, n)
    row = lambda c: pl.BlockSpec((tm, c), lambda i: (i, 0))
    full = lambda a: pl.BlockSpec(a.shape, lambda i: (0, 0))
    return pl.pallas_call(
        functools.partial(_attn_out_kernel, alpha=alpha),
        grid=(n // tm,),
        in_specs=[row(ya2.shape[1]), row(yb2.shape[1]), row(d), full(w_out), full(g), full(b)],
        out_specs=row(d),
        out_shape=jax.ShapeDtypeStruct((n, d), F32),
        compiler_params=_params(("arbitrary",)),
        name="attn_out_ln",
    )(ya2, yb2, x2, w_out, g, b)


def _conv_core(x, u, u1, u2, gate_b, cw_ref, wout_ref, g_ref, b_ref, alpha):
    c = u2 * cw_ref[0:1, :] + u1 * cw_ref[1:2, :] + u * cw_ref[2:3, :]
    y = _dot((gate_b * c).astype(BF16), wout_ref[...])
    return _layer_norm(alpha * x + y, g_ref[...], b_ref[...])


def _conv_in(x, win_ref, dc):
    xb = x.astype(BF16)
    h = _dot(xb, win_ref[:, 0:dc])
    gate_b = _dot(xb, win_ref[:, dc:2 * dc])
    gate_c = _dot(xb, win_ref[:, 2 * dc:3 * dc])
    return gate_c * h, gate_b


def _conv_prompt_kernel(x_ref, win_ref, cw_ref, wout_ref, g_ref, b_ref, o_ref, st_ref, tail_scr,
                        *, alpha):
    tm, dc = x_ref.shape[0], wout_ref.shape[0]

    @pl.when(pl.program_id(1) == 0)
    def _():
        tail_scr[...] = jnp.zeros(tail_scr.shape, F32)

    x = x_ref[...]
    u, gate_b = _conv_in(x, win_ref, dc)
    ext = jnp.concatenate([tail_scr[...], u], axis=0)
    u1 = ext[SUBLANES - 1:SUBLANES - 1 + tm]
    u2 = ext[SUBLANES - 2:SUBLANES - 2 + tm]
    o_ref[...] = _conv_core(x, u, u1, u2, gate_b, cw_ref, wout_ref, g_ref, b_ref, alpha)
    tail_scr[...] = u[tm - SUBLANES:tm]
    st_ref[0] = u[tm - (CONV_W - 1):tm]


def _conv_prompt(x2, w_in, conv_w, w_out, g, b, alpha, batch, seq):
    n, d = x2.shape
    dc = w_out.shape[0]
    tm = min(ROW_TILE, seq)
    nt = seq // tm
    full = lambda a: pl.BlockSpec(a.shape, lambda bb, i: (0, 0))
    return pl.pallas_call(
        functools.partial(_conv_prompt_kernel, alpha=alpha),
        grid=(batch, nt),
        in_specs=[pl.BlockSpec((tm, d), lambda bb, i: (bb * nt + i, 0)),
                  full(w_in), full(conv_w), full(w_out), full(g), full(b)],
        out_specs=[pl.BlockSpec((tm, d), lambda bb, i: (bb * nt + i, 0)),
                   pl.BlockSpec((1, CONV_W - 1, dc), lambda bb, i: (bb, 0, 0))],
        out_shape=[jax.ShapeDtypeStruct((n, d), F32),
                   jax.ShapeDtypeStruct((batch, CONV_W - 1, dc), F32)],
        scratch_shapes=[pltpu.VMEM((SUBLANES, dc), F32)],
        compiler_params=_params(("arbitrary", "arbitrary")),
        name="conv_prompt_ln",
    )(x2, w_in, conv_w, w_out, g, b)


def _conv_sample_kernel(x_ref, p0_ref, p1_ref, win_ref, cw_ref, wout_ref, g_ref, b_ref,
                        o_ref, u_ref, *, alpha, ts):
    n, dc = x_ref.shape[0], wout_ref.shape[0]
    x = x_ref[...]
    u, gate_b = _conv_in(x, win_ref, dc)
    ext = jnp.concatenate([jnp.zeros((SUBLANES, dc), F32), u], axis=0)
    tpos = _iota((n, dc), 0) % ts
    u1 = jnp.where(tpos == 0, p1_ref[...], ext[SUBLANES - 1:SUBLANES - 1 + n])
    u2 = jnp.where(tpos == 0, p0_ref[...],
                   jnp.where(tpos == 1, p1_ref[...], ext[SUBLANES - 2:SUBLANES - 2 + n]))
    o_ref[...] = _conv_core(x, u, u1, u2, gate_b, cw_ref, wout_ref, g_ref, b_ref, alpha)
    u_ref[...] = u


def _conv_sample(x2, past0, past1, w_in, conv_w, w_out, g, b, alpha, ts):
    n, d = x2.shape
    dc = w_out.shape[0]
    tm = min(ROW_TILE, n)
    assert tm % ts == 0
    full = lambda a: pl.BlockSpec(a.shape, lambda i: (0, 0))
    row = lambda c: pl.BlockSpec((tm, c), lambda i: (i, 0))
    return pl.pallas_call(
        functools.partial(_conv_sample_kernel, alpha=alpha, ts=ts),
        grid=(n // tm,),
        in_specs=[row(d), row(dc), row(dc), full(w_in), full(conv_w), full(w_out), full(g), full(b)],
        out_specs=[row(d), row(dc)],
        out_shape=[jax.ShapeDtypeStruct((n, d), F32), jax.ShapeDtypeStruct((n, dc), F32)],
        compiler_params=_params(("arbitrary",)),
        name="conv_sample_ln",
    )(x2, past0, past1, w_in, conv_w, w_out, g, b)


DMA_UNROLL = 8
GROUP_LANE0 = 0
EXPERT_LANE0 = 32


def _first_argmax(v, lane_f):
    mx = jnp.max(v, axis=1, keepdims=True)
    first = jnp.min(jnp.where(v == mx, lane_f, float(LANES)), axis=1, keepdims=True)
    return mx, first


def _router_kernel(x_ref, w_ref, bias_ref, cin_ref, e0_ref, e1_ref, g0_ref, g1_ref, r0_ref, r1_ref,
                   cnt_ref, carry_scr):
    tm = x_ref.shape[0]

    @pl.when(pl.program_id(0) == 0)
    def _():
        carry_scr[...] = cin_ref[...]

    logits = _dot3(x_ref[...], w_ref[...]) + bias_ref[...]
    lane = _iota((tm, LANES), 1)
    lane_f = lane.astype(F32)
    gl = jnp.where((lane >= GROUP_LANE0) & (lane < GROUP_LANE0 + N_GROUPS), logits, NEG_INF)
    gmax, gfirst = _first_argmax(gl, lane_f)
    pg = jnp.exp(gl - gmax)
    g_val = jnp.max(pg / jnp.sum(pg, axis=1, keepdims=True), axis=1, keepdims=True)
    g_idx = gfirst - float(GROUP_LANE0)
    grp_of_lane = ((lane - EXPERT_LANE0) // EXPERTS_PER_GROUP).astype(F32)
    in_grp = (lane >= EXPERT_LANE0) & (lane < EXPERT_LANE0 + N_EXPERTS) & (grp_of_lane == g_idx)
    el = jnp.where(in_grp, logits, NEG_INF)
    emax, first0 = _first_argmax(el, lane_f)
    pe = jnp.exp(el - emax)
    pe = pe / jnp.sum(pe, axis=1, keepdims=True)
    hit0 = lane_f == first0
    v0 = jnp.max(jnp.where(hit0, pe, 0.0), axis=1, keepdims=True)
    _, first1 = _first_argmax(jnp.where(hit0, NEG_INF, el), lane_f)
    hit1 = lane_f == first1
    v1 = jnp.max(jnp.where(hit1, pe, 0.0), axis=1, keepdims=True)
    denom = v0 + v1
    g0_ref[...] = g_val * v0 / denom
    g1_ref[...] = g_val * v1 / denom
    e0_ref[...] = first0.astype(I32) - EXPERT_LANE0
    e1_ref[...] = first1.astype(I32) - EXPERT_LANE0
    onehot = jnp.where(hit0 | hit1, 1.0, 0.0)
    lower = jnp.where(_iota((tm, tm), 1) < _iota((tm, tm), 0), 1.0, 0.0).astype(BF16)
    before = _dot(lower, onehot.astype(BF16)) + carry_scr[...]
    r0_ref[...] = jnp.sum(jnp.where(hit0, before, 0.0), axis=1, keepdims=True).astype(I32)
    r1_ref[...] = jnp.sum(jnp.where(hit1, before, 0.0), axis=1, keepdims=True).astype(I32)
    carry_scr[...] = carry_scr[...] + jnp.sum(onehot, axis=0, keepdims=True)
    cnt_ref[...] = carry_scr[...]


def _router(x2, w_slab, b_slab, counts_in):
    n, d = x2.shape
    tm = min(ROW_TILE, n)
    full = lambda a: pl.BlockSpec(a.shape, lambda i: (0, 0))
    col = pl.BlockSpec((tm, 1), lambda i: (i, 0))
    return pl.pallas_call(
        _router_kernel,
        grid=(n // tm,),
        in_specs=[pl.BlockSpec((tm, d), lambda i: (i, 0)), full(w_slab), full(b_slab), full(counts_in)],
        out_specs=[col] * 6 + [pl.BlockSpec((1, LANES), lambda i: (0, 0))],
        out_shape=[jax.ShapeDtypeStruct((n, 1), I32), jax.ShapeDtypeStruct((n, 1), I32),
                   jax.ShapeDtypeStruct((n, 1), F32), jax.ShapeDtypeStruct((n, 1), F32),
                   jax.ShapeDtypeStruct((n, 1), I32), jax.ShapeDtypeStruct((n, 1), I32),
                   jax.ShapeDtypeStruct((1, LANES), F32)],
        scratch_shapes=[pltpu.VMEM((1, LANES), F32)],
        compiler_params=_params(("arbitrary",)),
        name="moe_router",
    )(x2, w_slab, b_slab, counts_in)


def _row_tokens_kernel(*refs, sizes):
    out_ref = refs[-1]

    def zero(r, _):
        out_ref[r] = 0
        return 0

    lax.fori_loop(0, out_ref.shape[0], zero, 0, unroll=DMA_UNROLL)
    base = 0
    for g, n in enumerate(sizes):
        def place(t, _, p0=refs[2 * g], p1=refs[2 * g + 1], base=base):
            out_ref[p0[t]] = base + t
            out_ref[p1[t]] = base + t
            return 0

        lax.fori_loop(0, n, place, 0, unroll=DMA_UNROLL)
        base += n


def _row_tokens(positions, n_rows):
    flat = [p for pair in positions for p in pair]
    smem = pl.BlockSpec(memory_space=pltpu.SMEM)
    return pl.pallas_call(
        functools.partial(_row_tokens_kernel, sizes=tuple(p0.shape[0] for p0, _ in positions)),
        in_specs=[smem] * len(flat),
        out_specs=smem,
        out_shape=jax.ShapeDtypeStruct((n_rows,), I32),
        name="moe_row_tokens",
    )(*flat)


def _ffn_kernel(be_ref, nu_ref, tok_ref, x_hbm, wg_ref, wu_ref, wd_ref, o_ref,
                xbuf, sem, wg_s, wu_s, wd_s):
    del nu_ref
    i = pl.program_id(0)
    last = pl.num_programs(0) - 1
    slot = i % 2
    nxt = jnp.minimum(i + 1, last)

    def copy(blk, slot_, r):
        return pltpu.make_async_copy(x_hbm.at[pl.ds(tok_ref[blk * FFN_ROWS + r], 1)],
                                     xbuf.at[slot_, pl.ds(r, 1)], sem.at[slot_])

    def looped(fn):
        def body(r, _):
            fn(r)
            return 0
        lax.fori_loop(0, FFN_ROWS, body, 0, unroll=DMA_UNROLL)

    @pl.when(i == 0)
    def _():
        looped(lambda r: copy(0, 0, r).start())

    looped(lambda r: copy(i, slot, r).wait())

    @pl.when((i == 0) | (be_ref[i] != be_ref[jnp.maximum(i - 1, 0)]))
    def _():
        wg_s[...] = wg_ref[0, 0].astype(BF16)
        wu_s[...] = wu_ref[0, 0].astype(BF16)
        wd_s[...] = wd_ref[0, 0].astype(BF16)

    for r in range(FFN_ROWS):
        copy(nxt, 1 - slot, r).start()
    xb = xbuf[slot].astype(BF16)
    gate = _dot(xb, wg_s[...])
    hidden = gate * jax.nn.sigmoid(gate) * _dot(xb, wu_s[...])
    o_ref[...] = _dot(hidden.astype(BF16), wd_s[...])

    @pl.when(i == last)
    def _():
        looped(lambda r: copy(nxt, 1 - slot, r).wait())


def _grouped_ffn(block_expert, n_used, row_tok, x_all, w_gate, w_up, w_down, layer):
    d = x_all.shape[1]
    de = w_gate.shape[3]
    nblk = row_tok.shape[0] // FFN_ROWS
    wmap = lambda i, be, nu, tok: (layer, be[i], 0, 0)
    grid_spec = pltpu.PrefetchScalarGridSpec(
        num_scalar_prefetch=3,
        grid=(nblk,),
        in_specs=[pl.BlockSpec(memory_space=pl.ANY),
                  pl.BlockSpec((1, 1, d, de), wmap),
                  pl.BlockSpec((1, 1, d, de), wmap),
                  pl.BlockSpec((1, 1, de, d), wmap)],
        out_specs=pl.BlockSpec((FFN_ROWS, d), lambda i, be, nu, tok: (i, 0)),
        scratch_shapes=[pltpu.VMEM((2, FFN_ROWS, d), F32), pltpu.SemaphoreType.DMA((2,)),
                        pltpu.VMEM((d, de), BF16), pltpu.VMEM((d, de), BF16),
                        pltpu.VMEM((de, d), BF16)],
    )
    return pl.pallas_call(
        _ffn_kernel,
        grid_spec=grid_spec,
        out_shape=jax.ShapeDtypeStruct((nblk * FFN_ROWS, d), F32),
        compiler_params=_params(("arbitrary",)),
        name="moe_ffn",
    )(block_expert, n_used, row_tok, x_all, w_gate, w_up, w_down)


def _combine_kernel(p0_ref, p1_ref, x_ref, g0_ref, g1_ref, lg_ref, lb_ref,
                    yr_ref, o_ref, ybuf, sem, *, alpha):
    tm = x_ref.shape[0]
    base = pl.program_id(0) * tm

    def copies(t):
        p0 = p0_ref[base + t]
        p1 = p1_ref[base + t]
        return (pltpu.make_async_copy(yr_ref.at[pl.ds(p0, 1)], ybuf.at[0, pl.ds(t, 1)], sem),
                pltpu.make_async_copy(yr_ref.at[pl.ds(p1, 1)], ybuf.at[1, pl.ds(t, 1)], sem))

    def start(t, _):
        for cp in copies(t):
            cp.start()
        return 0

    def wait(t, _):
        for cp in copies(t):
            cp.wait()
        return 0

    lax.fori_loop(0, tm, start, 0, unroll=DMA_UNROLL)
    lax.fori_loop(0, tm, wait, 0, unroll=DMA_UNROLL)
    y = ybuf[0] * g0_ref[...] + ybuf[1] * g1_ref[...]
    o_ref[...] = _layer_norm(alpha * x_ref[...] + y, lg_ref[...], lb_ref[...])


def _combine(pos, x2, g0, g1, ln_g, ln_b, yr, alpha):
    n, d = x2.shape
    tm = min(ROW_TILE, n)
    col = pl.BlockSpec((tm, 1), lambda i, *_: (i, 0))
    vec = pl.BlockSpec((1, d), lambda i, *_: (0, 0))
    grid_spec = pltpu.PrefetchScalarGridSpec(
        num_scalar_prefetch=2,
        grid=(n // tm,),
        in_specs=[pl.BlockSpec((tm, d), lambda i, *_: (i, 0)), col, col, vec, vec,
                  pl.BlockSpec(memory_space=pl.ANY)],
        out_specs=pl.BlockSpec((tm, d), lambda i, *_: (i, 0)),
        scratch_shapes=[pltpu.VMEM((2, tm, d), F32), pltpu.SemaphoreType.DMA(())],
    )
    return pl.pallas_call(
        functools.partial(_combine_kernel, alpha=alpha),
        grid_spec=grid_spec,
        out_shape=jax.ShapeDtypeStruct((n, d), F32),
        compiler_params=_params(("arbitrary",)),
        name="moe_combine_ln",
    )(*pos, x2, g0, g1, ln_g, ln_b, yr)


def _hier_moe_ln(xs, w_rg, b_rg, w_re, b_re, w_gate, w_up, w_down, layer, ln_g, ln_b, alpha):
    d = xs[0].shape[1]
    w_slab = jnp.zeros((d, LANES), F32)
    w_slab = w_slab.at[:, GROUP_LANE0:GROUP_LANE0 + N_GROUPS].set(w_rg)
    w_slab = w_slab.at[:, EXPERT_LANE0:EXPERT_LANE0 + N_EXPERTS].set(w_re)
    b_slab = jnp.zeros((1, LANES), F32)
    b_slab = b_slab.at[0, GROUP_LANE0:GROUP_LANE0 + N_GROUPS].set(b_rg)
    b_slab = b_slab.at[0, EXPERT_LANE0:EXPERT_LANE0 + N_EXPERTS].set(b_re)

    counts = jnp.zeros((1, LANES), F32)
    routes, gates = [], []
    for x2 in xs:
        e0, e1, g0, g1, r0, r1, counts = _router(x2, w_slab, b_slab, counts)
        routes.append(tuple(a.reshape(-1) for a in (e0, e1, r0, r1)))
        gates.append((g0, g1))

    cnt = counts[0, EXPERT_LANE0:EXPERT_LANE0 + N_EXPERTS].astype(I32)
    padded = (cnt + FFN_ROWS - 1) // FFN_ROWS * FFN_ROWS
    end_pad = jnp.cumsum(padded)
    start_pad = (end_pad - padded).astype(I32)
    n_assign = 2 * sum(x2.shape[0] for x2 in xs)
    nblk = -(-n_assign // FFN_ROWS) + N_EXPERTS
    blk_start = jnp.arange(nblk, dtype=I32) * FFN_ROWS
    block_expert = jnp.minimum(
        jnp.sum((end_pad[None, :] <= blk_start[:, None]).astype(I32), axis=1), N_EXPERTS - 1)
    n_used = (end_pad[-1:] // FFN_ROWS).astype(I32)

    positions = [(start_pad[e0] + r0, start_pad[e1] + r1) for e0, e1, r0, r1 in routes]
    row_tok = _row_tokens(positions, nblk * FFN_ROWS)
    x_all = xs[0] if len(xs) == 1 else jnp.concatenate(xs, axis=0)
    yr = _grouped_ffn(block_expert, n_used, row_tok, x_all, w_gate, w_up, w_down, layer)
    return [_combine(pos, x2, g0, g1, ln_g, ln_b, yr, alpha)
            for x2, pos, (g0, g1) in zip(xs, positions, gates)]


def _stack(arrays):
    return arrays[0][None] if len(arrays) == 1 else jnp.stack(arrays)


def kernel(x_prompt, x_sample, cache_k_a, cache_v_a, cache_k_b, cache_v_b, cache_k_idx, state_conv, page_table, rel_bias, w_attn_in, w_attn_out, w_conv_in, conv_w, w_conv_out, w_router_group, b_router_group, w_router_expert, b_router_expert, w_exp_gate, w_exp_up, w_exp_down, ln_g, ln_b):
    depth = ln_g.shape[0]
    alpha = float((2 * depth) ** 0.25)
    bp, tp, d = x_prompt.shape
    bs, ts, _ = x_sample.shape
    assert ts >= CONV_W - 1
    xp = x_prompt.reshape(bp * tp, d)
    xs = x_sample.reshape(bs * ts, d)
    attn_p, attn_s, conv_p, conv_s = [], [], [], []
    kv_cols = (1, 2, 4, 5, 7)
    for layer in range(depth):
        i = layer // 2
        g_mix, b_mix = ln_g[layer, 0][None], ln_b[layer, 0][None]
        if layer % 2 == 0:
            w_in = jnp.pad(w_attn_in[i], ((0, 0), (0, _ATTN_IN_PAD - _ATTN_IN))).astype(BF16)
            w_out = w_attn_out[i].astype(BF16)
            qa, ka, qb, kb, qi, ki, kat, vat, kbt, vbt, kit, wit = _attn_inproj_prompt(
                xp, w_in, _transposed_proj_weight(w_attn_in[i]), bp, tp)
            proj_s = _attn_inproj(xs, w_in)
            ya_p = _moba_prompt8(qa, ka, vat, rel_bias, bp, tp)
            yb_p = _dsa_prompt_t(qb, qi, wit, kb, vbt, ki, rel_bias, bp, tp)
            caches_t = [jnp.moveaxis(c[i], 1, -1).reshape(c.shape[1], -1, c.shape[2])
                        for c in (cache_k_a, cache_v_a, cache_k_b, cache_v_b, cache_k_idx)]
            ya_s, yb_s = _sample_attn_t([a.reshape(bs, ts, -1) for a in proj_s], caches_t,
                                        page_table, rel_bias, bs, ts)
            xp = _attn_out(ya_p, yb_p, xp, w_out, g_mix, b_mix, alpha)
            xs = _attn_out(ya_s.reshape(bs * ts, -1), yb_s.reshape(bs * ts, -1), xs, w_out,
                           g_mix, b_mix, alpha)
            attn_p.append([kat, vat, kbt, vbt, kit])
            attn_s.append([proj_s[c] for c in kv_cols])
        else:
            w_in = w_conv_in[i].astype(BF16)
            w_out = w_conv_out[i].astype(BF16)
            xp, st_p = _conv_prompt(xp, w_in, conv_w[i], w_out, g_mix, b_mix, alpha, bp, tp)
            past = state_conv[i]
            xs, u_s = _conv_sample(xs, jnp.repeat(past[:, 0], ts, axis=0),
                                   jnp.repeat(past[:, 1], ts, axis=0),
                                   w_in, conv_w[i], w_out, g_mix, b_mix, alpha, ts)
            conv_p.append(st_p)
            conv_s.append(u_s.reshape(bs, ts, -1)[:, ts - (CONV_W - 1):])
        xp, xs = _hier_moe_ln([xp, xs], w_router_group[layer], b_router_group[layer],
                              w_router_expert[layer], b_router_expert[layer],
                              w_exp_gate, w_exp_up, w_exp_down, layer,
                              ln_g[layer, 1][None], ln_b[layer, 1][None], alpha)

    def rows(group, j, b, t, shape):
        return _stack([layer_rows[j].reshape((b, t) + shape) for layer_rows in group])

    def rows_t(group, j, b, t, shape):
        return _stack([jnp.moveaxis(layer_rows[j].reshape((b,) + shape + (t,)), -1, 1)
                       for layer_rows in group])

    head_shapes = ((H_A, HEAD_DIM), (H_A, HEAD_DIM), (KV_B, HEAD_DIM), (KV_B, HEAD_DIM), (D_IDX,))
    return ((xp.reshape(bp, tp, d), xs.reshape(bs, ts, d))
            + tuple(rows_t(attn_p, j, bp, tp, head_shapes[j]) for j in range(5))
            + (_stack(conv_p),)
            + tuple(rows(attn_s, j, bs, ts, head_shapes[j]) for j in range(5))
            + (_stack(conv_s),))
```

```python
import functools
import math

import numpy as np
import jax
import jax.numpy as jnp
from jax import lax
from jax.experimental import pallas as pl
from jax.experimental.pallas import tpu as pltpu

F32 = jnp.float32
BF16 = jnp.bfloat16
I32 = jnp.int32

HEAD_DIM = 64
H_A = 8
H_B = 8
KV_B = 2
G_B = H_B // KV_B
H_IDX = 8
D_IDX = 64
MOBA_BLOCK = 256
MOBA_TOPK = 3
DSA_TOPK = 256
NUM_BUCKETS = 32
MAX_DISTANCE = 128
CONV_W = 3
N_GROUPS = 4
EXPERTS_PER_GROUP = 8
N_EXPERTS = N_GROUPS * EXPERTS_PER_GROUP
LN_EPS = 1e-5
ATTN_COLS = (H_A * HEAD_DIM, H_A * HEAD_DIM, H_A * HEAD_DIM,
             H_B * HEAD_DIM, KV_B * HEAD_DIM, KV_B * HEAD_DIM,
             H_IDX * D_IDX, D_IDX, H_IDX)

LANES = 128
SUBLANES = 8
VMEM_LIMIT = 56 * 1024 * 1024

ATT_TILE = MOBA_BLOCK
ROW_TILE = 256
FFN_ROWS = 256
NEG_INF = float("-inf")


def _bucket_lows():
    n = np.arange(0, 4 * MAX_DISTANCE)
    max_exact = NUM_BUCKETS // 2
    nf = np.maximum(n, 1).astype(np.float32)
    large = max_exact + (np.log(nf / max_exact) / math.log(MAX_DISTANCE / max_exact)
                         * (NUM_BUCKETS - max_exact)).astype(np.int32)
    bucket = np.where(n < max_exact, n, np.minimum(large, NUM_BUCKETS - 1))
    assert np.all(np.diff(bucket) >= 0) and bucket[-1] == NUM_BUCKETS - 1
    lows = [int(np.argmax(bucket == k)) for k in range(NUM_BUCKETS)]
    assert all(bucket[lo] == k for k, lo in enumerate(lows))
    return tuple(lows)


BUCKET_LOWS = _bucket_lows()
FAR_DIST = BUCKET_LOWS[-1]


def _params(sem):
    return pltpu.CompilerParams(dimension_semantics=sem, vmem_limit_bytes=VMEM_LIMIT)


def _dot(a, b):
    return jnp.dot(a, b, preferred_element_type=F32)


def _dot_nt(a, b):
    return lax.dot_general(a, b, (((1,), (1,)), ((), ())), preferred_element_type=F32)


def _split_bf16(a):
    hi = a.astype(BF16)
    lo = (a - hi.astype(F32)).astype(BF16)
    return hi, lo


def _dot3(a, b, nt=False):
    ah, al = _split_bf16(a)
    bh, bl = _split_bf16(b)
    d = _dot_nt if nt else _dot
    return d(ah, bh) + (d(ah, bl) + d(al, bh))


def _iota(shape, dim):
    return lax.broadcasted_iota(I32, shape, dim)


def _layer_norm(z, g, b):
    mu = jnp.mean(z, axis=-1, keepdims=True)
    zc = z - mu
    var = jnp.mean(zc * zc, axis=-1, keepdims=True)
    return zc * lax.rsqrt(var + LN_EPS) * g + b


def _bias_by_distance(dist, bias_of_bucket):
    acc = jnp.zeros(dist.shape, F32) + bias_of_bucket(0)
    for k in range(1, NUM_BUCKETS):
        acc = jnp.where(dist >= BUCKET_LOWS[k], bias_of_bucket(k), acc)
    return acc


def _sort_key(x):
    bits = lax.bitcast_convert_type(x + 0.0, I32)
    return bits ^ (lax.shift_right_arithmetic(bits, 31) & 0x7FFFFFFF)


_ATTN_OFFS = tuple(int(v) for v in np.cumsum((0,) + ATTN_COLS[:-1]))
_ATTN_IN = sum(ATTN_COLS)
_ATTN_IN_PAD = -(-_ATTN_IN // LANES) * LANES
_TAIL_OFF = _ATTN_OFFS[7]


def _attn_inproj_kernel(x_ref, w_ref, qa, ka, va, qb, kb, vb, qi, ki, wi):
    xb = x_ref[...].astype(BF16)
    for o, off, wd in zip((qa, ka, va, qb, kb, vb, qi), _ATTN_OFFS, ATTN_COLS):
        o[...] = _dot(xb, w_ref[:, off:off + wd])
    tail = _dot(xb, w_ref[:, _TAIL_OFF:_ATTN_IN_PAD])
    ki[...] = tail[:, :D_IDX]
    wi[...] = tail[:, D_IDX:D_IDX + H_IDX]


def _attn_inproj(x2, w_pad):
    n, d = x2.shape
    tm = min(ROW_TILE, n)
    outs = tuple(jax.ShapeDtypeStruct((n, c), F32) for c in ATTN_COLS)
    return pl.pallas_call(
        _attn_inproj_kernel,
        grid=(n // tm,),
        in_specs=[pl.BlockSpec((tm, d), lambda i: (i, 0)),
                  pl.BlockSpec((d, _ATTN_IN_PAD), lambda i: (0, 0))],
        out_specs=tuple(pl.BlockSpec((tm, c), lambda i: (i, 0)) for c in ATTN_COLS),
        out_shape=outs,
        compiler_params=_params(("arbitrary",)),
        name="attn_inproj",
    )(x2, w_pad)


_T_GROUPS = (1, 2, 4, 5, 7, 8)
_T_ROWS = tuple(ATTN_COLS[g] for g in _T_GROUPS)
_T_OFFS = tuple(int(v) for v in np.cumsum((0,) + _T_ROWS[:-1]))
BF16_ROWS = 2 * SUBLANES
_T_TOTAL = -(-sum(_T_ROWS) // BF16_ROWS) * BF16_ROWS
_N_GROUPS_NORMAL = (0, 1, 3, 4, 6, 7)


def _attn_inproj_prompt_kernel(x_ref, w_ref, wt_ref, qa, ka, qb, kb, qi, ki,
                               kat, vat, kbt, vbt, kit, wit):
    xb = x_ref[...].astype(BF16)
    for o, g in zip((qa, ka, qb, kb, qi), _N_GROUPS_NORMAL[:5]):
        o[...] = _dot(xb, w_ref[:, _ATTN_OFFS[g]:_ATTN_OFFS[g] + ATTN_COLS[g]])
    ki[...] = _dot(xb, w_ref[:, _TAIL_OFF:_ATTN_IN_PAD])[:, :D_IDX]
    for o, off, rows in zip((kat, vat, kbt, vbt, kit, wit), _T_OFFS, _T_ROWS):
        padded = -(-rows // BF16_ROWS) * BF16_ROWS
        o[0] = _dot_nt(wt_ref[off:off + padded, :], xb)[:rows]


def _transposed_proj_weight(w_in):
    cols = [w_in[:, _ATTN_OFFS[g]:_ATTN_OFFS[g] + ATTN_COLS[g]] for g in _T_GROUPS]
    wt = jnp.concatenate(cols, axis=1).T
    return jnp.pad(wt, ((0, _T_TOTAL - wt.shape[0]), (0, 0))).astype(BF16)


def _attn_inproj_prompt(x2, w_pad, w_t, batch, seq):
    n, d = x2.shape
    tm = min(ROW_TILE, seq)
    nt = seq // tm
    normal = [ATTN_COLS[g] for g in _N_GROUPS_NORMAL]
    return pl.pallas_call(
        _attn_inproj_prompt_kernel,
        grid=(batch, nt),
        in_specs=[pl.BlockSpec((tm, d), lambda b, i: (b * nt + i, 0)),
                  pl.BlockSpec(w_pad.shape, lambda b, i: (0, 0)),
                  pl.BlockSpec(w_t.shape, lambda b, i: (0, 0))],
        out_specs=[pl.BlockSpec((tm, c), lambda b, i: (b * nt + i, 0)) for c in normal]
                  + [pl.BlockSpec((1, r, tm), lambda b, i: (b, 0, i)) for r in _T_ROWS],
        out_shape=[jax.ShapeDtypeStruct((n, c), F32) for c in normal]
                  + [jax.ShapeDtypeStruct((batch, r, seq), F32) for r in _T_ROWS],
        compiler_params=_params(("arbitrary", "arbitrary")),
        name="attn_inproj_prompt",
    )(x2, w_pad, w_t)


def _build_bias_tiles_t(bias_ref, scr, head0, n_heads):
    key = _iota((ATT_TILE, ATT_TILE), 0)
    qry = _iota((ATT_TILE, ATT_TILE), 1)
    for t in range(2):
        dist = qry - key + (ATT_TILE if t == 0 else 0)
        for h in range(n_heads):
            scr[h, t] = _bias_by_distance(dist, lambda k, h=h: bias_ref[k, head0 + h])


def _softmax_fold(state, s, vt, shift=None):
    m_old, l_old, acc_old = state
    tile_max = jnp.max(s, axis=0, keepdims=True)
    if shift is not None:
        tile_max = tile_max + shift
    m_new = jnp.maximum(m_old, tile_max)
    m_safe = jnp.where(m_new == NEG_INF, 0.0, m_new)
    alpha = jnp.exp(m_old - m_safe)
    p = jnp.exp(s - (m_safe if shift is None else m_safe - shift))
    return (m_new, alpha * l_old + jnp.sum(p, axis=0, keepdims=True),
            alpha * acc_old + _dot(vt, p.astype(BF16)))


def _store_states(scrs, states):
    for h, state in enumerate(states):
        for scr, value in zip(scrs, state):
            scr[h] = value


def _load_states(scrs, n_heads):
    return tuple(tuple(scr[h] for scr in scrs) for h in range(n_heads))


def _moba_prompt_kernel8(bias_ref, q_ref, k_ref, vt_ref, o_ref, kmean_scr, bias_scr, sel_scr,
                         m_scr, l_scr, acc_scr, *, nblk):
    b, i = pl.program_id(0), pl.program_id(1)

    @pl.when((b == 0) & (i == 0))
    def _():
        _build_bias_tiles_t(bias_ref, bias_scr, 0, H_A)

    @pl.when(i == 0)
    def _():
        for n in range(nblk):
            kmean_scr[n:n + 1, :] = jnp.mean(
                k_ref[n * MOBA_BLOCK:(n + 1) * MOBA_BLOCK, :], axis=0, keepdims=True)

    q = q_ref[...] * (1.0 / math.sqrt(HEAD_DIM))
    key = _iota((ATT_TILE, ATT_TILE), 0)
    qry = _iota((ATT_TILE, ATT_TILE), 1)
    blk = _iota((nblk, ATT_TILE), 0).astype(F32)
    heads = [slice(h * HEAD_DIM, (h + 1) * HEAD_DIM) for h in range(H_A)]
    q_h = []
    for h in range(H_A):
        qh = q[:, heads[h]]
        q_h.append(qh.astype(BF16))
        gate = _dot3(kmean_scr[:, heads[h]], qh, nt=True)
        gate = jnp.where(blk < i.astype(F32), gate, NEG_INF)
        selmask = jnp.zeros((nblk, ATT_TILE), F32)
        for _ in range(min(MOBA_TOPK, nblk)):
            mx = jnp.max(gate, axis=0, keepdims=True)
            first = jnp.min(jnp.where(gate == mx, blk, float(nblk)), axis=0, keepdims=True)
            hit = blk == first
            selmask = jnp.where(hit & (mx > NEG_INF), 1.0, selmask)
            gate = jnp.where(hit, NEG_INF, gate)
        sel_scr[h] = jnp.where(selmask > 0.0, 0.0, NEG_INF)

    def k_tile(ks, h):
        return k_ref[pl.ds(ks, ATT_TILE), heads[h]].astype(BF16)

    def vt_tile(ks, h):
        return vt_ref[0, heads[h], pl.ds(ks, ATT_TILE)].astype(BF16)

    scrs = (m_scr, l_scr, acc_scr)
    empty = (jnp.full((1, ATT_TILE), NEG_INF, F32), jnp.zeros((1, ATT_TILE), F32),
             jnp.zeros((HEAD_DIM, ATT_TILE), F32))

    start = pl.multiple_of(i * ATT_TILE, ATT_TILE)
    own = []
    for h in range(H_A):
        s = _dot_nt(k_tile(start, h), q_h[h]) + bias_scr[h, 1]
        own.append(_softmax_fold(empty, jnp.where(key <= qry, s, NEG_INF), vt_tile(start, h)))
    _store_states(scrs, own)

    @pl.when(i >= 1)
    def _():
        ks = pl.multiple_of((i - 1) * ATT_TILE, ATT_TILE)
        states = _load_states(scrs, H_A)
        _store_states(scrs, [
            _softmax_fold(states[h], _dot_nt(k_tile(ks, h), q_h[h])
                          + (bias_scr[h, 0] + sel_scr[h, pl.ds(i - 1, 1), :]), vt_tile(ks, h))
            for h in range(H_A)])

    def far_body(j, states):
        ks = pl.multiple_of(j * ATT_TILE, ATT_TILE)
        return tuple(
            _softmax_fold(states[h], _dot_nt(k_tile(ks, h), q_h[h]) + sel_scr[h, pl.ds(j, 1), :],
                          vt_tile(ks, h), shift=bias_ref[NUM_BUCKETS - 1, h])
            for h in range(H_A))

    states = lax.fori_loop(0, i - 1, far_body, _load_states(scrs, H_A))
    o_ref[...] = jnp.concatenate([acc / l for _, l, acc in states], axis=0).T


def _moba_prompt8(q2, k2, vt3, rel_bias, batch, seq):
    assert ATT_TILE >= FAR_DIST
    nq = seq // ATT_TILE
    nblk = seq // MOBA_BLOCK
    width = q2.shape[1]
    return pl.pallas_call(
        functools.partial(_moba_prompt_kernel8, nblk=nblk),
        grid=(batch, nq),
        in_specs=[pl.BlockSpec(memory_space=pltpu.SMEM),
                  pl.BlockSpec((ATT_TILE, width), lambda b, i: (b * nq + i, 0)),
                  pl.BlockSpec((seq, width), lambda b, i: (b, 0)),
                  pl.BlockSpec((1, width, seq), lambda b, i: (b, 0, 0))],
        out_specs=pl.BlockSpec((ATT_TILE, width), lambda b, i: (b * nq + i, 0)),
        out_shape=jax.ShapeDtypeStruct((batch * seq, width), F32),
        scratch_shapes=[pltpu.VMEM((nblk, width), F32),
                        pltpu.VMEM((H_A, 2, ATT_TILE, ATT_TILE), F32),
                        pltpu.VMEM((H_A, nblk, ATT_TILE), F32),
                        pltpu.VMEM((H_A, 1, ATT_TILE), F32),
                        pltpu.VMEM((H_A, 1, ATT_TILE), F32),
                        pltpu.VMEM((H_A, HEAD_DIM, ATT_TILE), F32)],
        compiler_params=_params(("arbitrary", "arbitrary")),
        name="moba_prompt",
    )(rel_bias, q2, k2, vt3)


INT_MIN = -2 ** 31


def _strict_upper(n):
    return jnp.where(_iota((n, n), 0) < _iota((n, n), 1), 1.0, 0.0).astype(BF16)


def _select_tile(key, thr, need, carry, tri):
    eqf = jnp.where(key == thr, 1.0, 0.0)
    prefix = _dot(eqf.astype(BF16), tri) + carry
    self_ = jnp.where(key > thr, 1.0, jnp.where(prefix < need, eqf, 0.0))
    return self_, carry + jnp.sum(eqf, axis=1, keepdims=True)


RADIX_BITS = 3


def _kth_largest_key(count_ge, rows, k_sel):
    kk = float(k_sel)
    zero = jnp.zeros((rows, 1), I32)
    thr = jnp.where(count_ge(zero) >= kk, zero, jnp.full((rows, 1), INT_MIN, I32))

    def digit_step(thr, shift, n_digits):
        digit = jnp.zeros((rows, 1), I32)
        for m in range(1, n_digits):
            cand = thr | jnp.left_shift(jnp.int32(m), shift)
            digit = digit + jnp.where(count_ge(cand) >= kk, 1, 0)
        return thr | jnp.left_shift(digit, shift)

    n_steps, first_bits = divmod(31, RADIX_BITS)
    if first_bits:
        thr = digit_step(thr, jnp.int32(31 - first_bits), 1 << first_bits)
    return lax.fori_loop(
        0, n_steps,
        lambda t, thr: digit_step(thr, 31 - first_bits - RADIX_BITS * (t + 1), 1 << RADIX_BITS), thr)


def _strict_lower(n):
    return jnp.where(_iota((n, n), 1) < _iota((n, n), 0), 1.0, 0.0).astype(BF16)


def _dsa_prompt_t_kernel(bias_ref, qb_ref, qi_ref, wit_ref, kb_ref, vbt_ref, ki_ref, o_ref,
                         key_scr, bias_scr, m_scr, l_scr, acc_scr, ties_scr, *, k_sel):
    b, i = pl.program_id(0), pl.program_id(1)

    @pl.when((b == 0) & (i == 0))
    def _():
        _build_bias_tiles_t(bias_ref, bias_scr, H_A, H_B)

    key_ix = _iota((ATT_TILE, ATT_TILE), 0)
    qry_ix = _iota((ATT_TILE, ATT_TILE), 1)
    idx_scale = 1.0 / math.sqrt(D_IDX * H_IDX)

    qi = qi_ref[...]
    q_cat = []
    for h in range(H_IDX):
        hi, lo = _split_bf16(qi[:, h * D_IDX:(h + 1) * D_IDX])
        q_cat.append(jnp.concatenate([hi, lo, hi], axis=1))
    wit = wit_ref[0]
    w_row = [wit[h:h + 1, :] for h in range(H_IDX)]

    def score_body(j, _):
        ks = pl.multiple_of(j * ATT_TILE, ATT_TILE)
        hi, lo = _split_bf16(ki_ref[pl.ds(ks, ATT_TILE), :])
        k_cat = jnp.concatenate([hi, hi, lo], axis=1)
        acc = jnp.zeros((ATT_TILE, ATT_TILE), F32)
        for h in range(H_IDX):
            acc = acc + jnp.maximum(_dot_nt(k_cat, q_cat[h]), 0.0) * w_row[h]
        causal = (key_ix + (j - i) * ATT_TILE) <= qry_ix
        key_scr[j] = _sort_key(jnp.where(causal, acc * idx_scale, NEG_INF))
        return 0

    lax.fori_loop(0, i + 1, score_body, 0)

    def count_ge(cand):
        def body(j, cnt):
            return cnt + jnp.sum(jnp.where(key_scr[j] >= cand, 1.0, 0.0), axis=0, keepdims=True)
        return lax.fori_loop(0, i + 1, body, jnp.zeros((1, ATT_TILE), F32))

    kk = float(k_sel)
    zero = jnp.zeros((1, ATT_TILE), I32)
    thr0 = jnp.where(count_ge(zero) >= kk, zero, jnp.full((1, ATT_TILE), INT_MIN, I32))

    def bit_body(t, thr):
        cand = thr | lax.shift_left(jnp.int32(1), 30 - t)
        return jnp.where(count_ge(cand) >= kk, cand, thr)

    thr = lax.fori_loop(0, 31, bit_body, thr0)

    def above_and_ties_body(j, carry):
        above, ties = carry
        key = key_scr[j]
        ties_scr[j] = ties
        above = above + jnp.sum(jnp.where(key > thr, 1.0, 0.0), axis=0, keepdims=True)
        ties = ties + jnp.sum(jnp.where(key == thr, 1.0, 0.0), axis=0, keepdims=True)
        return above, ties

    zero_row = jnp.zeros((1, ATT_TILE), F32)
    above, _ = lax.fori_loop(0, i + 1, above_and_ties_body, (zero_row, zero_row))
    need = kk - above

    q = qb_ref[...] * (1.0 / math.sqrt(HEAD_DIM))
    q_h = [q[:, h * HEAD_DIM:(h + 1) * HEAD_DIM].astype(BF16) for h in range(H_B)]
    lower = _strict_lower(ATT_TILE)
    scrs = (m_scr, l_scr, acc_scr)

    def additive_mask(j):
        key = key_scr[j]
        eqf = jnp.where(key == thr, 1.0, 0.0)
        earlier_ties = _dot(lower, eqf.astype(BF16)) + ties_scr[j]
        member = jnp.where(key > thr, 1.0, jnp.where(earlier_ties < need, eqf, 0.0))
        causal = (key_ix + (j - i) * ATT_TILE) <= qry_ix
        return jnp.where(causal, jnp.where(member > 0.0, 0.0, NEG_INF), NEG_INF)

    def attend(j, bias_tile, states):
        ks = pl.multiple_of(j * ATT_TILE, ATT_TILE)
        negmask = additive_mask(j)
        out = []
        for n in range(KV_B):
            cs = slice(n * HEAD_DIM, (n + 1) * HEAD_DIM)
            kt = kb_ref[pl.ds(ks, ATT_TILE), cs].astype(BF16)
            vt = vbt_ref[0, cs, pl.ds(ks, ATT_TILE)].astype(BF16)
            for g in range(G_B):
                h = n * G_B + g
                tile = bias_tile(h)
                if tile is None:
                    out.append(_softmax_fold(states[h], _dot_nt(kt, q_h[h]) + negmask, vt,
                                             shift=bias_ref[NUM_BUCKETS - 1, H_A + h]))
                else:
                    out.append(_softmax_fold(states[h], _dot_nt(kt, q_h[h]) + (tile + negmask), vt))
        return tuple(out)

    empty = (jnp.full((1, ATT_TILE), NEG_INF, F32), zero_row, jnp.zeros((HEAD_DIM, ATT_TILE), F32))
    _store_states(scrs, attend(i, lambda h: bias_scr[h, 1], (empty,) * H_B))

    @pl.when(i >= 1)
    def _():
        _store_states(scrs, attend(i - 1, lambda h: bias_scr[h, 0], _load_states(scrs, H_B)))

    states = lax.fori_loop(0, i - 1, lambda j, st: attend(j, lambda h: None, st),
                           _load_states(scrs, H_B))
    o_ref[...] = jnp.concatenate([acc / l for _, l, acc in states], axis=0).T


def _dsa_prompt_t(qb2, qi2, wit3, kb2, vbt3, ki2, rel_bias, batch, seq):
    assert ATT_TILE >= FAR_DIST
    nq = seq // ATT_TILE
    k_sel = min(DSA_TOPK, seq // 4)
    qmap = lambda b, i: (b * nq + i, 0)
    kmap = lambda b, i: (b, 0)
    return pl.pallas_call(
        functools.partial(_dsa_prompt_t_kernel, k_sel=k_sel),
        grid=(batch, nq),
        in_specs=[pl.BlockSpec(memory_space=pltpu.SMEM),
                  pl.BlockSpec((ATT_TILE, H_B * HEAD_DIM), qmap),
                  pl.BlockSpec((ATT_TILE, H_IDX * D_IDX), qmap),
                  pl.BlockSpec((1, H_IDX, ATT_TILE), lambda b, i: (b, 0, i)),
                  pl.BlockSpec((seq, KV_B * HEAD_DIM), kmap),
                  pl.BlockSpec((1, KV_B * HEAD_DIM, seq), lambda b, i: (b, 0, 0)),
                  pl.BlockSpec((seq, D_IDX), kmap)],
        out_specs=pl.BlockSpec((ATT_TILE, H_B * HEAD_DIM), qmap),
        out_shape=jax.ShapeDtypeStruct((batch * seq, H_B * HEAD_DIM), F32),
        scratch_shapes=[pltpu.VMEM((nq, ATT_TILE, ATT_TILE), I32),
                        pltpu.VMEM((H_B, 2, ATT_TILE, ATT_TILE), F32),
                        pltpu.VMEM((H_B, 1, ATT_TILE), F32),
                        pltpu.VMEM((H_B, 1, ATT_TILE), F32),
                        pltpu.VMEM((H_B, HEAD_DIM, ATT_TILE), F32),
                        pltpu.VMEM((nq, 1, ATT_TILE), F32)],
        compiler_params=_params(("arbitrary", "arbitrary")),
        name="dsa_prompt",
    )(rel_bias, qb2, qi2, wit3, kb2, vbt3, ki2)


N_CACHES = 5
TAIL_ROWS = LANES


def _softmax_rows(s):
    m = jnp.max(s, axis=1, keepdims=True)
    p = jnp.exp(s - m)
    return p / jnp.sum(p, axis=1, keepdims=True)


def _stack_heads(x, n_heads, width):
    return jnp.concatenate([x[:, h * width:(h + 1) * width] for h in range(n_heads)], axis=0)


def _sample_attn_kernel(pt_ref, bias_a_ref, bias_b_ref, qa_ref, ka_ref, va_ref, qb_ref, kb_ref,
                        vb_ref, qi_ref, ki_ref, wi_ref, cka, cva, ckb, cvb, cki,
                        ya_ref, yb_ref, ka_buf, va_buf, kb_buf, vb_buf, ki_buf, sem,
                        bias_a_scr, bias_b_scr, *, n_seq, n_pages, page, ts):
    b = pl.program_id(0)
    slot = b % 2
    past = n_pages * page
    lp = past + TAIL_ROWS
    rows = H_A * ts
    caches = (cka, cva, ckb, cvb, cki)
    bufs = (ka_buf, va_buf, kb_buf, vb_buf, ki_buf)

    def page_copies(seq, slot_):
        cps = []
        for p in range(n_pages):
            pg = pt_ref[seq, p]
            for c in range(N_CACHES):
                cps.append(pltpu.make_async_copy(caches[c].at[pg], bufs[c].at[slot_, p],
                                                 sem.at[slot_, c]))
        return cps

    @pl.when(b == 0)
    def _():
        for cp in page_copies(0, 0):
            cp.start()
        dist = past + _iota((rows, lp), 0) % ts - _iota((rows, lp), 1)
        bias_a_scr[...] = _bias_by_distance(dist, lambda k: bias_a_ref[:, k:k + 1])
        bias_b_scr[...] = _bias_by_distance(dist, lambda k: bias_b_ref[:, k:k + 1])

    @pl.when(b + 1 < n_seq)
    def _():
        for cp in page_copies(b + 1, 1 - slot):
            cp.start()

    for cp in page_copies(b, slot):
        cp.wait()

    def tail(new_ref):
        new = new_ref[0]
        return jnp.concatenate([new, jnp.zeros((TAIL_ROWS - ts, new.shape[1]), F32)], axis=0)

    def scores(q, buf, new_ref, dot_page, dot_new):
        tiles = [dot_page(q, buf[slot, p]) for p in range(n_pages)]
        return jnp.concatenate(tiles + [dot_new(q, tail(new_ref))], axis=1)

    def weighted_values(p, buf, new_ref):
        pb = p.astype(BF16)
        o = _dot(pb[:, past:], tail(new_ref).astype(BF16))
        for pg in range(n_pages):
            o = o + _dot_nt(pb[:, pg * page:(pg + 1) * page], buf[slot, pg].astype(BF16))
        return o

    bf_page = lambda q, kt: _dot(q, kt.astype(BF16))
    bf_new = lambda q, k: _dot_nt(q, k.astype(BF16))

    colpos = _iota((ts, lp), 1)
    qpos = _iota((ts, lp), 0)
    valid = colpos <= past + qpos
    valid_rows = jnp.where(_iota((rows, lp), 1) <= past + _iota((rows, lp), 0) % ts, 1.0, 0.0)
    row_head = _iota((rows, 1), 0) // ts

    n_past_blk = past // MOBA_BLOCK
    pages_per_blk = MOBA_BLOCK // page
    qa = qa_ref[0] * (1.0 / math.sqrt(HEAD_DIM))
    width_a = H_A * HEAD_DIM
    q_rows = jnp.concatenate([qa] * H_A, axis=0)
    own_head_a = (_iota((rows, width_a), 1) // HEAD_DIM) == row_head
    q_bd = jnp.where(own_head_a, q_rows, 0.0)
    blk_of_lane = _iota((width_a, LANES), 1)
    kmean_t = jnp.zeros((width_a, LANES), F32)
    for n in range(n_past_blk):
        tot = ka_buf[slot, n * pages_per_blk]
        for pp in range(1, pages_per_blk):
            tot = tot + ka_buf[slot, n * pages_per_blk + pp]
        mean_n = jnp.sum(tot, axis=1, keepdims=True) * (1.0 / MOBA_BLOCK)
        kmean_t = jnp.where(blk_of_lane == n, mean_n, kmean_t)
    gate = _dot3(q_bd, kmean_t)
    blk_lane = _iota((rows, LANES), 1).astype(F32)
    gate = jnp.where(blk_lane < float(n_past_blk), gate, NEG_INF)
    selmask = jnp.zeros((rows, LANES), F32)
    for _ in range(min(MOBA_TOPK, n_past_blk)):
        mx = jnp.max(gate, axis=1, keepdims=True)
        first = jnp.min(jnp.where(gate == mx, blk_lane, float(LANES)), axis=1, keepdims=True)
        hit = blk_lane == first
        selmask = jnp.where(hit, 1.0, selmask)
        gate = jnp.where(hit, NEG_INF, gate)
    expand = jnp.where(_iota((LANES, lp), 1) // MOBA_BLOCK == _iota((LANES, lp), 0),
                       1.0, 0.0).astype(BF16)
    in_sel_blk = _dot(selmask.astype(BF16), expand)
    keep_a = jnp.where(_iota((rows, lp), 1) >= past, valid_rows, in_sel_blk) > 0.0
    s = scores(q_bd.astype(BF16), ka_buf, ka_ref, bf_page, bf_new) + bias_a_scr[...]
    p = _softmax_rows(jnp.where(keep_a, s, NEG_INF))
    o = jnp.where(own_head_a, weighted_values(p, va_buf, va_ref), 0.0)
    ya = o[0:ts]
    for h in range(1, H_A):
        ya = ya + o[h * ts:(h + 1) * ts]
    ya_ref[0] = ya

    k_sel = min(DSA_TOPK, (past + ts) // 4)
    qi_rows = _stack_heads(qi_ref[0], H_IDX, D_IDX)
    wi = wi_ref[0]
    w_rows = jnp.concatenate([wi[:, h:h + 1] for h in range(H_IDX)], axis=0)
    s_idx = scores(qi_rows, ki_buf, ki_ref, _dot3, functools.partial(_dot3, nt=True))
    contrib = jnp.maximum(s_idx, 0.0) * w_rows
    score = contrib[0:ts]
    for h in range(1, H_IDX):
        score = score + contrib[h * ts:(h + 1) * ts]
    score = jnp.where(valid, score * (1.0 / math.sqrt(D_IDX * H_IDX)), NEG_INF)
    key = _sort_key(score)

    def count_ge(cand):
        return jnp.sum(jnp.where(key >= cand, 1.0, 0.0), axis=1, keepdims=True)

    thr = _kth_largest_key(count_ge, ts, k_sel)
    need = float(k_sel) - jnp.sum(jnp.where(key > thr, 1.0, 0.0), axis=1, keepdims=True)
    tri = _strict_upper(LANES)
    carry = jnp.zeros((ts, 1), F32)
    sel_chunks = []
    for c in range(lp // LANES):
        sel_c, carry = _select_tile(key[:, c * LANES:(c + 1) * LANES], thr, need, carry, tri)
        sel_chunks.append(sel_c)
    keep = jnp.where(valid, jnp.concatenate(sel_chunks, axis=1), 0.0)
    keep_b = jnp.concatenate([keep] * H_B, axis=0) > 0.0

    qb = qb_ref[0] * (1.0 / math.sqrt(HEAD_DIM))
    qb_rows = _stack_heads(qb, H_B, HEAD_DIM)
    width_b = KV_B * HEAD_DIM
    own_kv = (_iota((rows, width_b), 1) // HEAD_DIM) == row_head // G_B
    qb_bd = jnp.where(own_kv, jnp.concatenate([qb_rows] * KV_B, axis=1), 0.0)
    s = scores(qb_bd.astype(BF16), kb_buf, kb_ref, bf_page, bf_new) + bias_b_scr[...]
    p = _softmax_rows(jnp.where(keep_b, s, NEG_INF))
    o = weighted_values(p, vb_buf, vb_ref)
    pieces = []
    for h in range(H_B):
        n = h // G_B
        pieces.append(o[h * ts:(h + 1) * ts, n * HEAD_DIM:(n + 1) * HEAD_DIM])
    yb_ref[0] = jnp.concatenate(pieces, axis=1)


def _sample_attn(new_rows, caches_t, page_table, rel_bias, n_seq, ts):
    n_pages = page_table.shape[1]
    page = caches_t[0].shape[2]
    past = n_pages * page
    assert past % MOBA_BLOCK == 0 and MOBA_BLOCK % page == 0 and page == LANES and ts <= SUBLANES
    assert past // MOBA_BLOCK <= LANES
    lp = past + TAIL_ROWS
    rows = H_A * ts
    bias_a_rows = jnp.repeat(rel_bias[:, :H_A].T, ts, axis=0)
    bias_b_rows = jnp.repeat(rel_bias[:, H_A:].T, ts, axis=0)
    full = lambda a: pl.BlockSpec(a.shape, lambda b, pt: (0, 0))
    seq_spec = lambda a: pl.BlockSpec((1, ts, a.shape[2]), lambda b, pt: (b, 0, 0))
    grid_spec = pltpu.PrefetchScalarGridSpec(
        num_scalar_prefetch=1,
        grid=(n_seq,),
        in_specs=[full(bias_a_rows), full(bias_b_rows)] + [seq_spec(a) for a in new_rows]
                 + [pl.BlockSpec(memory_space=pl.ANY)] * N_CACHES,
        out_specs=[pl.BlockSpec((1, ts, H_A * HEAD_DIM), lambda b, pt: (b, 0, 0)),
                   pl.BlockSpec((1, ts, H_B * HEAD_DIM), lambda b, pt: (b, 0, 0))],
        scratch_shapes=[pltpu.VMEM((2, n_pages, c.shape[1], page), F32) for c in caches_t]
                       + [pltpu.SemaphoreType.DMA((2, N_CACHES)),
                          pltpu.VMEM((rows, lp), F32), pltpu.VMEM((rows, lp), F32)],
    )
    return pl.pallas_call(
        functools.partial(_sample_attn_kernel, n_seq=n_seq, n_pages=n_pages, page=page, ts=ts),
        grid_spec=grid_spec,
        out_shape=[jax.ShapeDtypeStruct((n_seq, ts, H_A * HEAD_DIM), F32),
                   jax.ShapeDtypeStruct((n_seq, ts, H_B * HEAD_DIM), F32)],
        compiler_params=_params(("arbitrary",)),
        name="sample_attn",
    )(page_table, bias_a_rows, bias_b_rows, *new_rows, *caches_t)


def _attn_out_kernel(ya_ref, yb_ref, x_ref, w_ref, g_ref, b_ref, o_ref, *, alpha):
    wa = ya_ref.shape[1]
    y = (_dot(ya_ref[...].astype(BF16), w_ref[0:wa, :])
         + _dot(yb_ref[...].astype(BF16), w_ref[wa:, :]))
    o_ref[...] = _layer_norm(alpha * x_ref[...] + y, g_ref[...], b_ref[...])


def _attn_out(ya2, yb2, x2, w_out, g, b, alpha):
    n, d = x2.shape
    tm = min(ROW_TILE, n)
    row = lambda c: pl.BlockSpec((tm, c), lambda i: (i, 0))
    full = lambda a: pl.BlockSpec(a.shape, lambda i: (0, 0))
    return pl.pallas_call(
        functools.partial(_attn_out_kernel, alpha=alpha),
        grid=(n // tm,),
        in_specs=[row(ya2.shape[1]), row(yb2.shape[1]), row(d), full(w_out), full(g), full(b)],
        out_specs=row(d),
        out_shape=jax.ShapeDtypeStruct((n, d), F32),
        compiler_params=_params(("arbitrary",)),
        name="attn_out_ln",
    )(ya2, yb2, x2, w_out, g, b)


def _conv_core(x, u, u1, u2, gate_b, cw_ref, wout_ref, g_ref, b_ref, alpha):
    c = u2 * cw_ref[0:1, :] + u1 * cw_ref[1:2, :] + u * cw_ref[2:3, :]
    y = _dot((gate_b * c).astype(BF16), wout_ref[...])
    return _layer_norm(alpha * x + y, g_ref[...], b_ref[...])


def _conv_in(x, win_ref, dc):
    xb = x.astype(BF16)
    h = _dot(xb, win_ref[:, 0:dc])
    gate_b = _dot(xb, win_ref[:, dc:2 * dc])
    gate_c = _dot(xb, win_ref[:, 2 * dc:3 * dc])
    return gate_c * h, gate_b


def _conv_prompt_kernel(x_ref, win_ref, cw_ref, wout_ref, g_ref, b_ref, o_ref, st_ref, tail_scr,
                        *, alpha):
    tm, dc = x_ref.shape[0], wout_ref.shape[0]

    @pl.when(pl.program_id(1) == 0)
    def _():
        tail_scr[...] = jnp.zeros(tail_scr.shape, F32)

    x = x_ref[...]
    u, gate_b = _conv_in(x, win_ref, dc)
    ext = jnp.concatenate([tail_scr[...], u], axis=0)
    u1 = ext[SUBLANES - 1:SUBLANES - 1 + tm]
    u2 = ext[SUBLANES - 2:SUBLANES - 2 + tm]
    o_ref[...] = _conv_core(x, u, u1, u2, gate_b, cw_ref, wout_ref, g_ref, b_ref, alpha)
    tail_scr[...] = u[tm - SUBLANES:tm]
    st_ref[0] = u[tm - (CONV_W - 1):tm]


def _conv_prompt(x2, w_in, conv_w, w_out, g, b, alpha, batch, seq):
    n, d = x2.shape
    dc = w_out.shape[0]
    tm = min(ROW_TILE, seq)
    nt = seq // tm
    full = lambda a: pl.BlockSpec(a.shape, lambda bb, i: (0, 0))
    return pl.pallas_call(
        functools.partial(_conv_prompt_kernel, alpha=alpha),
        grid=(batch, nt),
        in_specs=[pl.BlockSpec((tm, d), lambda bb, i: (bb * nt + i, 0)),
                  full(w_in), full(conv_w), full(w_out), full(g), full(b)],
        out_specs=[pl.BlockSpec((tm, d), lambda bb, i: (bb * nt + i, 0)),
                   pl.BlockSpec((1, CONV_W - 1, dc), lambda bb, i: (bb, 0, 0))],
        out_shape=[jax.ShapeDtypeStruct((n, d), F32),
                   jax.ShapeDtypeStruct((batch, CONV_W - 1, dc), F32)],
        scratch_shapes=[pltpu.VMEM((SUBLANES, dc), F32)],
        compiler_params=_params(("arbitrary", "arbitrary")),
        name="conv_prompt_ln",
    )(x2, w_in, conv_w, w_out, g, b)


def _conv_sample_kernel(x_ref, p0_ref, p1_ref, win_ref, cw_ref, wout_ref, g_ref, b_ref,
                        o_ref, u_ref, *, alpha, ts):
    n, dc = x_ref.shape[0], wout_ref.shape[0]
    x = x_ref[...]
    u, gate_b = _conv_in(x, win_ref, dc)
    ext = jnp.concatenate([jnp.zeros((SUBLANES, dc), F32), u], axis=0)
    tpos = _iota((n, dc), 0) % ts
    u1 = jnp.where(tpos == 0, p1_ref[...], ext[SUBLANES - 1:SUBLANES - 1 + n])
    u2 = jnp.where(tpos == 0, p0_ref[...],
                   jnp.where(tpos == 1, p1_ref[...], ext[SUBLANES - 2:SUBLANES - 2 + n]))
    o_ref[...] = _conv_core(x, u, u1, u2, gate_b, cw_ref, wout_ref, g_ref, b_ref, alpha)
    u_ref[...] = u


def _conv_sample(x2, past0, past1, w_in, conv_w, w_out, g, b, alpha, ts):
    n, d = x2.shape
    dc = w_out.shape[0]
    tm = min(ROW_TILE, n)
    assert tm % ts == 0
    full = lambda a: pl.BlockSpec(a.shape, lambda i: (0, 0))
    row = lambda c: pl.BlockSpec((tm, c), lambda i: (i, 0))
    return pl.pallas_call(
        functools.partial(_conv_sample_kernel, alpha=alpha, ts=ts),
        grid=(n // tm,),
        in_specs=[row(d), row(dc), row(dc), full(w_in), full(conv_w), full(w_out), full(g), full(b)],
        out_specs=[row(d), row(dc)],
        out_shape=[jax.ShapeDtypeStruct((n, d), F32), jax.ShapeDtypeStruct((n, dc), F32)],
        compiler_params=_params(("arbitrary",)),
        name="conv_sample_ln",
    )(x2, past0, past1, w_in, conv_w, w_out, g, b)


DMA_UNROLL = 8
GROUP_LANE0 = 0
EXPERT_LANE0 = 32


def _first_argmax(v, lane_f):
    mx = jnp.max(v, axis=1, keepdims=True)
    first = jnp.min(jnp.where(v == mx, lane_f, float(LANES)), axis=1, keepdims=True)
    return mx, first


def _router_kernel(x_ref, w_ref, bias_ref, cin_ref, e0_ref, e1_ref, g0_ref, g1_ref, r0_ref, r1_ref,
                   cnt_ref, carry_scr):
    tm = x_ref.shape[0]

    @pl.when(pl.program_id(0) == 0)
    def _():
        carry_scr[...] = cin_ref[...]

    logits = _dot3(x_ref[...], w_ref[...]) + bias_ref[...]
    lane = _iota((tm, LANES), 1)
    lane_f = lane.astype(F32)
    gl = jnp.where((lane >= GROUP_LANE0) & (lane < GROUP_LANE0 + N_GROUPS), logits, NEG_INF)
    gmax, gfirst = _first_argmax(gl, lane_f)
    pg = jnp.exp(gl - gmax)
    g_val = jnp.max(pg / jnp.sum(pg, axis=1, keepdims=True), axis=1, keepdims=True)
    g_idx = gfirst - float(GROUP_LANE0)
    grp_of_lane = ((lane - EXPERT_LANE0) // EXPERTS_PER_GROUP).astype(F32)
    in_grp = (lane >= EXPERT_LANE0) & (lane < EXPERT_LANE0 + N_EXPERTS) & (grp_of_lane == g_idx)
    el = jnp.where(in_grp, logits, NEG_INF)
    emax, first0 = _first_argmax(el, lane_f)
    pe = jnp.exp(el - emax)
    pe = pe / jnp.sum(pe, axis=1, keepdims=True)
    hit0 = lane_f == first0
    v0 = jnp.max(jnp.where(hit0, pe, 0.0), axis=1, keepdims=True)
    _, first1 = _first_argmax(jnp.where(hit0, NEG_INF, el), lane_f)
    hit1 = lane_f == first1
    v1 = jnp.max(jnp.where(hit1, pe, 0.0), axis=1, keepdims=True)
    denom = v0 + v1
    g0_ref[...] = g_val * v0 / denom
    g1_ref[...] = g_val * v1 / denom
    e0_ref[...] = first0.astype(I32) - EXPERT_LANE0
    e1_ref[...] = first1.astype(I32) - EXPERT_LANE0
    onehot = jnp.where(hit0 | hit1, 1.0, 0.0)
    lower = jnp.where(_iota((tm, tm), 1) < _iota((tm, tm), 0), 1.0, 0.0).astype(BF16)
    before = _dot(lower, onehot.astype(BF16)) + carry_scr[...]
    r0_ref[...] = jnp.sum(jnp.where(hit0, before, 0.0), axis=1, keepdims=True).astype(I32)
    r1_ref[...] = jnp.sum(jnp.where(hit1, before, 0.0), axis=1, keepdims=True).astype(I32)
    carry_scr[...] = carry_scr[...] + jnp.sum(onehot, axis=0, keepdims=True)
    cnt_ref[...] = carry_scr[...]


def _router(x2, w_slab, b_slab, counts_in):
    n, d = x2.shape
    tm = min(ROW_TILE, n)
    full = lambda a: pl.BlockSpec(a.shape, lambda i: (0, 0))
    col = pl.BlockSpec((tm, 1), lambda i: (i, 0))
    return pl.pallas_call(
        _router_kernel,
        grid=(n // tm,),
        in_specs=[pl.BlockSpec((tm, d), lambda i: (i, 0)), full(w_slab), full(b_slab), full(counts_in)],
        out_specs=[col] * 6 + [pl.BlockSpec((1, LANES), lambda i: (0, 0))],
        out_shape=[jax.ShapeDtypeStruct((n, 1), I32), jax.ShapeDtypeStruct((n, 1), I32),
                   jax.ShapeDtypeStruct((n, 1), F32), jax.ShapeDtypeStruct((n, 1), F32),
                   jax.ShapeDtypeStruct((n, 1), I32), jax.ShapeDtypeStruct((n, 1), I32),
                   jax.ShapeDtypeStruct((1, LANES), F32)],
        scratch_shapes=[pltpu.VMEM((1, LANES), F32)],
        compiler_params=_params(("arbitrary",)),
        name="moe_router",
    )(x2, w_slab, b_slab, counts_in)


def _dispatch_kernel(e0_ref, e1_ref, r0_ref, r1_ref, start_ref, x_ref, xr_in, xr_out, sem):
    del xr_in
    tm = x_ref.shape[0]
    base = pl.program_id(0) * tm

    def copies(t):
        p0 = start_ref[e0_ref[base + t]] + r0_ref[base + t]
        p1 = start_ref[e1_ref[base + t]] + r1_ref[base + t]
        src = x_ref.at[pl.ds(t, 1)]
        return (pltpu.make_async_copy(src, xr_out.at[pl.ds(p0, 1)], sem),
                pltpu.make_async_copy(src, xr_out.at[pl.ds(p1, 1)], sem))

    def start(t, _):
        for cp in copies(t):
            cp.start()
        return 0

    def wait(t, _):
        for cp in copies(t):
            cp.wait()
        return 0

    lax.fori_loop(0, tm, start, 0, unroll=DMA_UNROLL)
    lax.fori_loop(0, tm, wait, 0, unroll=DMA_UNROLL)


def _dispatch(route, start_pad, x2, xr):
    n, d = x2.shape
    tm = min(ROW_TILE, n)
    grid_spec = pltpu.PrefetchScalarGridSpec(
        num_scalar_prefetch=5,
        grid=(n // tm,),
        in_specs=[pl.BlockSpec((tm, d), lambda i, *_: (i, 0)), pl.BlockSpec(memory_space=pl.ANY)],
        out_specs=pl.BlockSpec(memory_space=pl.ANY),
        scratch_shapes=[pltpu.SemaphoreType.DMA(())],
    )
    return pl.pallas_call(
        _dispatch_kernel,
        grid_spec=grid_spec,
        out_shape=jax.ShapeDtypeStruct(xr.shape, F32),
        input_output_aliases={6: 0},
        compiler_params=_params(("arbitrary",)),
        name="moe_dispatch",
    )(*route, start_pad, x2, xr)


def _ffn_kernel(be_ref, nu_ref, x_ref, wg_ref, wu_ref, wd_ref, o_ref, wg_s, wu_s, wd_s):
    i = pl.program_id(0)

    @pl.when(i < nu_ref[0])
    def _():
        @pl.when((i == 0) | (be_ref[i] != be_ref[jnp.maximum(i - 1, 0)]))
        def _():
            wg_s[...] = wg_ref[0, 0].astype(BF16)
            wu_s[...] = wu_ref[0, 0].astype(BF16)
            wd_s[...] = wd_ref[0, 0].astype(BF16)

        xb = x_ref[...].astype(BF16)
        gate = _dot(xb, wg_s[...])
        hidden = gate * jax.nn.sigmoid(gate) * _dot(xb, wu_s[...])
        o_ref[...] = _dot(hidden.astype(BF16), wd_s[...])

    @pl.when(i >= nu_ref[0])
    def _():
        o_ref[...] = jnp.zeros(o_ref.shape, F32)


def _grouped_ffn(block_expert, n_used, xr, w_gate, w_up, w_down, layer):
    rows, d = xr.shape
    de = w_gate.shape[3]
    nblk = rows // FFN_ROWS
    grid_spec = pltpu.PrefetchScalarGridSpec(
        num_scalar_prefetch=2,
        grid=(nblk,),
        in_specs=[pl.BlockSpec((FFN_ROWS, d), lambda i, be, nu: (jnp.minimum(i, nu[0] - 1), 0)),
                  pl.BlockSpec((1, 1, d, de), lambda i, be, nu: (layer, be[i], 0, 0)),
                  pl.BlockSpec((1, 1, d, de), lambda i, be, nu: (layer, be[i], 0, 0)),
                  pl.BlockSpec((1, 1, de, d), lambda i, be, nu: (layer, be[i], 0, 0))],
        out_specs=pl.BlockSpec((FFN_ROWS, d), lambda i, be, nu: (i, 0)),
        scratch_shapes=[pltpu.VMEM((d, de), BF16), pltpu.VMEM((d, de), BF16), pltpu.VMEM((de, d), BF16)],
    )
    return pl.pallas_call(
        _ffn_kernel,
        grid_spec=grid_spec,
        out_shape=jax.ShapeDtypeStruct((rows, d), F32),
        compiler_params=_params(("arbitrary",)),
        name="moe_ffn",
    )(block_expert, n_used, xr, w_gate, w_up, w_down)


def _combine_kernel(e0_ref, e1_ref, r0_ref, r1_ref, start_ref, x_ref, g0_ref, g1_ref, lg_ref, lb_ref,
                    yr_ref, o_ref, ybuf, sem, *, alpha):
    tm = x_ref.shape[0]
    base = pl.program_id(0) * tm

    def copies(t):
        p0 = start_ref[e0_ref[base + t]] + r0_ref[base + t]
        p1 = start_ref[e1_ref[base + t]] + r1_ref[base + t]
        return (pltpu.make_async_copy(yr_ref.at[pl.ds(p0, 1)], ybuf.at[0, pl.ds(t, 1)], sem),
                pltpu.make_async_copy(yr_ref.at[pl.ds(p1, 1)], ybuf.at[1, pl.ds(t, 1)], sem))

    def start(t, _):
        for cp in copies(t):
            cp.start()
        return 0

    def wait(t, _):
        for cp in copies(t):
            cp.wait()
        return 0

    lax.fori_loop(0, tm, start, 0, unroll=DMA_UNROLL)
    lax.fori_loop(0, tm, wait, 0, unroll=DMA_UNROLL)
    y = ybuf[0] * g0_ref[...] + ybuf[1] * g1_ref[...]
    o_ref[...] = _layer_norm(alpha * x_ref[...] + y, lg_ref[...], lb_ref[...])


def _combine(route, start_pad, x2, g0, g1, ln_g, ln_b, yr, alpha):
    n, d = x2.shape
    tm = min(ROW_TILE, n)
    col = pl.BlockSpec((tm, 1), lambda i, *_: (i, 0))
    vec = pl.BlockSpec((1, d), lambda i, *_: (0, 0))
    grid_spec = pltpu.PrefetchScalarGridSpec(
        num_scalar_prefetch=5,
        grid=(n // tm,),
        in_specs=[pl.BlockSpec((tm, d), lambda i, *_: (i, 0)), col, col, vec, vec,
                  pl.BlockSpec(memory_space=pl.ANY)],
        out_specs=pl.BlockSpec((tm, d), lambda i, *_: (i, 0)),
        scratch_shapes=[pltpu.VMEM((2, tm, d), F32), pltpu.SemaphoreType.DMA(())],
    )
    return pl.pallas_call(
        functools.partial(_combine_kernel, alpha=alpha),
        grid_spec=grid_spec,
        out_shape=jax.ShapeDtypeStruct((n, d), F32),
        compiler_params=_params(("arbitrary",)),
        name="moe_combine_ln",
    )(*route, start_pad, x2, g0, g1, ln_g, ln_b, yr)


def _hier_moe_ln(xs, w_rg, b_rg, w_re, b_re, w_gate, w_up, w_down, layer, ln_g, ln_b, alpha):
    d = xs[0].shape[1]
    w_slab = jnp.zeros((d, LANES), F32)
    w_slab = w_slab.at[:, GROUP_LANE0:GROUP_LANE0 + N_GROUPS].set(w_rg)
    w_slab = w_slab.at[:, EXPERT_LANE0:EXPERT_LANE0 + N_EXPERTS].set(w_re)
    b_slab = jnp.zeros((1, LANES), F32)
    b_slab = b_slab.at[0, GROUP_LANE0:GROUP_LANE0 + N_GROUPS].set(b_rg)
    b_slab = b_slab.at[0, EXPERT_LANE0:EXPERT_LANE0 + N_EXPERTS].set(b_re)

    counts = jnp.zeros((1, LANES), F32)
    routes, gates = [], []
    for x2 in xs:
        e0, e1, g0, g1, r0, r1, counts = _router(x2, w_slab, b_slab, counts)
        routes.append(tuple(a.reshape(-1) for a in (e0, e1, r0, r1)))
        gates.append((g0, g1))

    cnt = counts[0, EXPERT_LANE0:EXPERT_LANE0 + N_EXPERTS].astype(I32)
    padded = (cnt + FFN_ROWS - 1) // FFN_ROWS * FFN_ROWS
    end_pad = jnp.cumsum(padded)
    start_pad = (end_pad - padded).astype(I32)
    n_assign = 2 * sum(x2.shape[0] for x2 in xs)
    nblk = -(-n_assign // FFN_ROWS) + N_EXPERTS
    blk_start = jnp.arange(nblk, dtype=I32) * FFN_ROWS
    block_expert = jnp.minimum(
        jnp.sum((end_pad[None, :] <= blk_start[:, None]).astype(I32), axis=1), N_EXPERTS - 1)
    n_used = (end_pad[-1:] // FFN_ROWS).astype(I32)

    xr = jnp.zeros((nblk * FFN_ROWS, d), F32)
    for x2, route in zip(xs, routes):
        xr = _dispatch(route, start_pad, x2, xr)
    yr = _grouped_ffn(block_expert, n_used, xr, w_gate, w_up, w_down, layer)
    return [_combine(route, start_pad, x2, g0, g1, ln_g, ln_b, yr, alpha)
            for x2, route, (g0, g1) in zip(xs, routes, gates)]


def _stack(arrays):
    return arrays[0][None] if len(arrays) == 1 else jnp.stack(arrays)


def kernel(x_prompt, x_sample, cache_k_a, cache_v_a, cache_k_b, cache_v_b, cache_k_idx, state_conv, page_table, rel_bias, w_attn_in, w_attn_out, w_conv_in, conv_w, w_conv_out, w_router_group, b_router_group, w_router_expert, b_router_expert, w_exp_gate, w_exp_up, w_exp_down, ln_g, ln_b):
    depth = ln_g.shape[0]
    alpha = float((2 * depth) ** 0.25)
    bp, tp, d = x_prompt.shape
    bs, ts, _ = x_sample.shape
    assert ts >= CONV_W - 1
    xp = x_prompt.reshape(bp * tp, d)
    xs = x_sample.reshape(bs * ts, d)
    attn_p, attn_s, conv_p, conv_s = [], [], [], []
    kv_cols = (1, 2, 4, 5, 7)
    for layer in range(depth):
        i = layer // 2
        g_mix, b_mix = ln_g[layer, 0][None], ln_b[layer, 0][None]
        if layer % 2 == 0:
            w_in = jnp.pad(w_attn_in[i], ((0, 0), (0, _ATTN_IN_PAD - _ATTN_IN))).astype(BF16)
            w_out = w_attn_out[i].astype(BF16)
            qa, ka, qb, kb, qi, ki, kat, vat, kbt, vbt, kit, wit = _attn_inproj_prompt(
                xp, w_in, _transposed_proj_weight(w_attn_in[i]), bp, tp)
            proj_s = _attn_inproj(xs, w_in)
            ya_p = _moba_prompt8(qa, ka, vat, rel_bias, bp, tp)
            yb_p = _dsa_prompt_t(qb, qi, wit, kb, vbt, ki, rel_bias, bp, tp)
            caches_t = [jnp.moveaxis(c[i], 1, -1).reshape(c.shape[1], -1, c.shape[2])
                        for c in (cache_k_a, cache_v_a, cache_k_b, cache_v_b, cache_k_idx)]
            ya_s, yb_s = _sample_attn([a.reshape(bs, ts, -1) for a in proj_s], caches_t,
                                      page_table, rel_bias, bs, ts)
            xp = _attn_out(ya_p, yb_p, xp, w_out, g_mix, b_mix, alpha)
            xs = _attn_out(ya_s.reshape(bs * ts, -1), yb_s.reshape(bs * ts, -1), xs, w_out,
                           g_mix, b_mix, alpha)
            attn_p.append([kat, vat, kbt, vbt, kit])
            attn_s.append([proj_s[c] for c in kv_cols])
        else:
            w_in = w_conv_in[i].astype(BF16)
            w_out = w_conv_out[i].astype(BF16)
            xp, st_p = _conv_prompt(xp, w_in, conv_w[i], w_out, g_mix, b_mix, alpha, bp, tp)
            past = state_conv[i]
            xs, u_s = _conv_sample(xs, jnp.repeat(past[:, 0], ts, axis=0),
                                   jnp.repeat(past[:, 1], ts, axis=0),
                                   w_in, conv_w[i], w_out, g_mix, b_mix, alpha, ts)
            conv_p.append(st_p)
            conv_s.append(u_s.reshape(bs, ts, -1)[:, ts - (CONV_W - 1):])
        xp, xs = _hier_moe_ln([xp, xs], w_router_group[layer], b_router_group[layer],
                              w_router_expert[layer], b_router_expert[layer],
                              w_exp_gate, w_exp_up, w_exp_down, layer,
                              ln_g[layer, 1][None], ln_b[layer, 1][None], alpha)

    def rows(group, j, b, t, shape):
        return _stack([layer_rows[j].reshape((b, t) + shape) for layer_rows in group])

    def rows_t(group, j, b, t, shape):
        return _stack([jnp.moveaxis(layer_rows[j].reshape((b,) + shape + (t,)), -1, 1)
                       for layer_rows in group])

    head_shapes = ((H_A, HEAD_DIM), (H_A, HEAD_DIM), (KV_B, HEAD_DIM), (KV_B, HEAD_DIM), (D_IDX,))
    return ((xp.reshape(bp, tp, d), xs.reshape(bs, ts, d))
            + tuple(rows_t(attn_p, j, bp, tp, head_shapes[j]) for j in range(5))
            + (_stack(conv_p),)
            + tuple(rows(attn_s, j, bs, ts, head_shapes[j]) for j in range(5))
            + (_stack(conv_s),))
```

```python
import functools
import math

import numpy as np
import jax
import jax.numpy as jnp
from jax import lax
from jax.experimental import pallas as pl
from jax.experimental.pallas import tpu as pltpu

F32 = jnp.float32
BF16 = jnp.bfloat16
I32 = jnp.int32

HEAD_DIM = 64
H_A = 8
H_B = 8
KV_B = 2
G_B = H_B // KV_B
H_IDX = 8
D_IDX = 64
MOBA_BLOCK = 256
MOBA_TOPK = 3
DSA_TOPK = 256
NUM_BUCKETS = 32
MAX_DISTANCE = 128
CONV_W = 3
N_GROUPS = 4
EXPERTS_PER_GROUP = 8
N_EXPERTS = N_GROUPS * EXPERTS_PER_GROUP
LN_EPS = 1e-5
ATTN_COLS = (H_A * HEAD_DIM, H_A * HEAD_DIM, H_A * HEAD_DIM,
             H_B * HEAD_DIM, KV_B * HEAD_DIM, KV_B * HEAD_DIM,
             H_IDX * D_IDX, D_IDX, H_IDX)

LANES = 128
SUBLANES = 8
VMEM_LIMIT = 56 * 1024 * 1024

ATT_TILE = MOBA_BLOCK
ROW_TILE = 512
FFN_ROWS = 256
NEG_INF = float("-inf")


def _bucket_lows():
    n = np.arange(0, 4 * MAX_DISTANCE)
    max_exact = NUM_BUCKETS // 2
    nf = np.maximum(n, 1).astype(np.float32)
    large = max_exact + (np.log(nf / max_exact) / math.log(MAX_DISTANCE / max_exact)
                         * (NUM_BUCKETS - max_exact)).astype(np.int32)
    bucket = np.where(n < max_exact, n, np.minimum(large, NUM_BUCKETS - 1))
    assert np.all(np.diff(bucket) >= 0) and bucket[-1] == NUM_BUCKETS - 1
    lows = [int(np.argmax(bucket == k)) for k in range(NUM_BUCKETS)]
    assert all(bucket[lo] == k for k, lo in enumerate(lows))
    return tuple(lows)


BUCKET_LOWS = _bucket_lows()
FAR_DIST = BUCKET_LOWS[-1]


def _params(sem):
    return pltpu.CompilerParams(dimension_semantics=sem, vmem_limit_bytes=VMEM_LIMIT)


def _dot(a, b):
    return jnp.dot(a, b, preferred_element_type=F32)


def _dot_nt(a, b):
    return lax.dot_general(a, b, (((1,), (1,)), ((), ())), preferred_element_type=F32)


def _split_bf16(a):
    hi = a.astype(BF16)
    lo = (a - hi.astype(F32)).astype(BF16)
    return hi, lo


def _dot3(a, b, nt=False):
    ah, al = _split_bf16(a)
    bh, bl = _split_bf16(b)
    d = _dot_nt if nt else _dot
    return d(ah, bh) + (d(ah, bl) + d(al, bh))


def _iota(shape, dim):
    return lax.broadcasted_iota(I32, shape, dim)


def _layer_norm(z, g, b):
    mu = jnp.mean(z, axis=-1, keepdims=True)
    zc = z - mu
    var = jnp.mean(zc * zc, axis=-1, keepdims=True)
    return zc * lax.rsqrt(var + LN_EPS) * g + b


def _bias_by_distance(dist, bias_of_bucket):
    acc = jnp.zeros(dist.shape, F32) + bias_of_bucket(0)
    for k in range(1, NUM_BUCKETS):
        acc = jnp.where(dist >= BUCKET_LOWS[k], bias_of_bucket(k), acc)
    return acc


def _sort_key(x):
    bits = lax.bitcast_convert_type(x + 0.0, I32)
    return bits ^ (lax.shift_right_arithmetic(bits, 31) & 0x7FFFFFFF)


_ATTN_OFFS = tuple(int(v) for v in np.cumsum((0,) + ATTN_COLS[:-1]))
_ATTN_IN = sum(ATTN_COLS)
_ATTN_IN_PAD = -(-_ATTN_IN // LANES) * LANES
_TAIL_OFF = _ATTN_OFFS[7]


def _attn_inproj_kernel(x_ref, w_ref, qa, ka, va, qb, kb, vb, qi, ki, wi):
    xb = x_ref[...].astype(BF16)
    for o, off, wd in zip((qa, ka, va, qb, kb, vb, qi), _ATTN_OFFS, ATTN_COLS):
        o[...] = _dot(xb, w_ref[:, off:off + wd])
    tail = _dot(xb, w_ref[:, _TAIL_OFF:_ATTN_IN_PAD])
    ki[...] = tail[:, :D_IDX]
    wi[...] = tail[:, D_IDX:D_IDX + H_IDX]


def _attn_inproj(x2, w_pad):
    n, d = x2.shape
    tm = min(ROW_TILE, n)
    outs = tuple(jax.ShapeDtypeStruct((n, c), F32) for c in ATTN_COLS)
    return pl.pallas_call(
        _attn_inproj_kernel,
        grid=(n // tm,),
        in_specs=[pl.BlockSpec((tm, d), lambda i: (i, 0)),
                  pl.BlockSpec((d, _ATTN_IN_PAD), lambda i: (0, 0))],
        out_specs=tuple(pl.BlockSpec((tm, c), lambda i: (i, 0)) for c in ATTN_COLS),
        out_shape=outs,
        compiler_params=_params(("arbitrary",)),
        name="attn_inproj",
    )(x2, w_pad)


_T_GROUPS = (1, 2, 4, 5, 7, 8)
_T_ROWS = tuple(ATTN_COLS[g] for g in _T_GROUPS)
_T_OFFS = tuple(int(v) for v in np.cumsum((0,) + _T_ROWS[:-1]))
BF16_ROWS = 2 * SUBLANES
_T_TOTAL = -(-sum(_T_ROWS) // BF16_ROWS) * BF16_ROWS
_N_GROUPS_NORMAL = (0, 1, 3, 4, 6, 7)


def _attn_inproj_prompt_kernel(x_ref, w_ref, wt_ref, qa, ka, qb, kb, qi, ki,
                               kat, vat, kbt, vbt, kit, wit):
    xb = x_ref[...].astype(BF16)
    for o, g in zip((qa, ka, qb, kb, qi), _N_GROUPS_NORMAL[:5]):
        o[...] = _dot(xb, w_ref[:, _ATTN_OFFS[g]:_ATTN_OFFS[g] + ATTN_COLS[g]])
    ki[...] = _dot(xb, w_ref[:, _TAIL_OFF:_ATTN_IN_PAD])[:, :D_IDX]
    for o, off, rows in zip((kat, vat, kbt, vbt, kit, wit), _T_OFFS, _T_ROWS):
        padded = -(-rows // BF16_ROWS) * BF16_ROWS
        o[0] = _dot_nt(wt_ref[off:off + padded, :], xb)[:rows]


def _transposed_proj_weight(w_in):
    cols = [w_in[:, _ATTN_OFFS[g]:_ATTN_OFFS[g] + ATTN_COLS[g]] for g in _T_GROUPS]
    wt = jnp.concatenate(cols, axis=1).T
    return jnp.pad(wt, ((0, _T_TOTAL - wt.shape[0]), (0, 0))).astype(BF16)


def _attn_inproj_prompt(x2, w_pad, w_t, batch, seq):
    n, d = x2.shape
    tm = min(ROW_TILE, seq)
    nt = seq // tm
    normal = [ATTN_COLS[g] for g in _N_GROUPS_NORMAL]
    return pl.pallas_call(
        _attn_inproj_prompt_kernel,
        grid=(batch, nt),
        in_specs=[pl.BlockSpec((tm, d), lambda b, i: (b * nt + i, 0)),
                  pl.BlockSpec(w_pad.shape, lambda b, i: (0, 0)),
                  pl.BlockSpec(w_t.shape, lambda b, i: (0, 0))],
        out_specs=[pl.BlockSpec((tm, c), lambda b, i: (b * nt + i, 0)) for c in normal]
                  + [pl.BlockSpec((1, r, tm), lambda b, i: (b, 0, i)) for r in _T_ROWS],
        out_shape=[jax.ShapeDtypeStruct((n, c), F32) for c in normal]
                  + [jax.ShapeDtypeStruct((batch, r, seq), F32) for r in _T_ROWS],
        compiler_params=_params(("arbitrary", "arbitrary")),
        name="attn_inproj_prompt",
    )(x2, w_pad, w_t)


def _build_bias_tiles_t(bias_ref, scr, head0, n_heads):
    key = _iota((ATT_TILE, ATT_TILE), 0)
    qry = _iota((ATT_TILE, ATT_TILE), 1)
    for t in range(2):
        dist = qry - key + (ATT_TILE if t == 0 else 0)
        for h in range(n_heads):
            scr[h, t] = _bias_by_distance(dist, lambda k, h=h: bias_ref[k, head0 + h])


def _softmax_fold(state, s, vt, shift=None):
    m_old, l_old, acc_old = state
    tile_max = jnp.max(s, axis=0, keepdims=True)
    if shift is not None:
        tile_max = tile_max + shift
    m_new = jnp.maximum(m_old, tile_max)
    m_safe = jnp.where(m_new == NEG_INF, 0.0, m_new)
    alpha = jnp.exp(m_old - m_safe)
    p = jnp.exp(s - (m_safe if shift is None else m_safe - shift))
    pb = p.astype(BF16)
    if isinstance(vt, (list, tuple)):
        pv = jnp.concatenate([_dot(v, pb[:, c * ATT_TILE:(c + 1) * ATT_TILE])
                              for c, v in enumerate(vt)], axis=1)
    else:
        pv = _dot(vt, pb)
    return m_new, alpha * l_old + jnp.sum(p, axis=0, keepdims=True), alpha * acc_old + pv


def _store_states(scrs, states):
    for h, state in enumerate(states):
        for scr, value in zip(scrs, state):
            scr[h] = value


def _load_states(scrs, n_heads):
    return tuple(tuple(scr[h] for scr in scrs) for h in range(n_heads))


def _moba_prompt_kernel8(bias_ref, q_ref, k_ref, vt_ref, o_ref, kmean_scr, bias_scr, sel_scr,
                         m_scr, l_scr, acc_scr, *, nblk):
    b, i = pl.program_id(0), pl.program_id(1)

    @pl.when((b == 0) & (i == 0))
    def _():
        _build_bias_tiles_t(bias_ref, bias_scr, 0, H_A)

    @pl.when(i == 0)
    def _():
        for n in range(nblk):
            kmean_scr[n:n + 1, :] = jnp.mean(
                k_ref[n * MOBA_BLOCK:(n + 1) * MOBA_BLOCK, :], axis=0, keepdims=True)

    q = q_ref[...] * (1.0 / math.sqrt(HEAD_DIM))
    key = _iota((ATT_TILE, ATT_TILE), 0)
    qry = _iota((ATT_TILE, ATT_TILE), 1)
    blk = _iota((nblk, ATT_TILE), 0).astype(F32)
    heads = [slice(h * HEAD_DIM, (h + 1) * HEAD_DIM) for h in range(H_A)]
    q_h = []
    for h in range(H_A):
        qh = q[:, heads[h]]
        q_h.append(qh.astype(BF16))
        gate = _dot3(kmean_scr[:, heads[h]], qh, nt=True)
        gate = jnp.where(blk < i.astype(F32), gate, NEG_INF)
        selmask = jnp.zeros((nblk, ATT_TILE), F32)
        for _ in range(min(MOBA_TOPK, nblk)):
            mx = jnp.max(gate, axis=0, keepdims=True)
            first = jnp.min(jnp.where(gate == mx, blk, float(nblk)), axis=0, keepdims=True)
            hit = blk == first
            selmask = jnp.where(hit & (mx > NEG_INF), 1.0, selmask)
            gate = jnp.where(hit, NEG_INF, gate)
        sel_scr[h] = jnp.where(selmask > 0.0, 0.0, NEG_INF)

    def k_tile(ks, h):
        return k_ref[pl.ds(ks, ATT_TILE), heads[h]].astype(BF16)

    def vt_tile(ks, h):
        return vt_ref[0, heads[h], pl.ds(ks, ATT_TILE)].astype(BF16)

    scrs = (m_scr, l_scr, acc_scr)
    wide = H_A * ATT_TILE
    empty = (jnp.full((1, wide), NEG_INF, F32), jnp.zeros((1, wide), F32),
             jnp.zeros((HEAD_DIM, wide), F32))
    far_shift = jnp.concatenate(
        [jnp.full((1, ATT_TILE), bias_ref[NUM_BUCKETS - 1, h], F32) for h in range(H_A)], axis=1)

    def fold(state, ks, add, shift=None):
        s = jnp.concatenate([_dot_nt(k_tile(ks, h), q_h[h]) + add(h) for h in range(H_A)], axis=1)
        return _softmax_fold(state, s, [vt_tile(ks, h) for h in range(H_A)], shift=shift)

    start = pl.multiple_of(i * ATT_TILE, ATT_TILE)
    causal = jnp.where(key <= qry, 0.0, NEG_INF)
    _store_states(scrs, [fold(empty, start, lambda h: bias_scr[h, 1] + causal)])

    @pl.when(i >= 1)
    def _():
        ks = pl.multiple_of((i - 1) * ATT_TILE, ATT_TILE)
        _store_states(scrs, [fold(_load_states(scrs, 1)[0], ks,
                                  lambda h: bias_scr[h, 0] + sel_scr[h, pl.ds(i - 1, 1), :])])

    def far_body(j, state):
        ks = pl.multiple_of(j * ATT_TILE, ATT_TILE)
        return fold(state, ks, lambda h: sel_scr[h, pl.ds(j, 1), :], shift=far_shift)

    _, l, acc = lax.fori_loop(0, i - 1, far_body, _load_states(scrs, 1)[0])
    out_t = acc / l
    o_ref[...] = jnp.concatenate([out_t[:, h * ATT_TILE:(h + 1) * ATT_TILE] for h in range(H_A)],
                                 axis=0).T


def _moba_prompt8(q2, k2, vt3, rel_bias, batch, seq):
    assert ATT_TILE >= FAR_DIST
    nq = seq // ATT_TILE
    nblk = seq // MOBA_BLOCK
    width = q2.shape[1]
    return pl.pallas_call(
        functools.partial(_moba_prompt_kernel8, nblk=nblk),
        grid=(batch, nq),
        in_specs=[pl.BlockSpec(memory_space=pltpu.SMEM),
                  pl.BlockSpec((ATT_TILE, width), lambda b, i: (b * nq + i, 0)),
                  pl.BlockSpec((seq, width), lambda b, i: (b, 0)),
                  pl.BlockSpec((1, width, seq), lambda b, i: (b, 0, 0))],
        out_specs=pl.BlockSpec((ATT_TILE, width), lambda b, i: (b * nq + i, 0)),
        out_shape=jax.ShapeDtypeStruct((batch * seq, width), F32),
        scratch_shapes=[pltpu.VMEM((nblk, width), F32),
                        pltpu.VMEM((H_A, 2, ATT_TILE, ATT_TILE), F32),
                        pltpu.VMEM((H_A, nblk, ATT_TILE), F32),
                        pltpu.VMEM((1, 1, H_A * ATT_TILE), F32),
                        pltpu.VMEM((1, 1, H_A * ATT_TILE), F32),
                        pltpu.VMEM((1, HEAD_DIM, H_A * ATT_TILE), F32)],
        compiler_params=_params(("arbitrary", "arbitrary")),
        name="moba_prompt",
    )(rel_bias, q2, k2, vt3)


INT_MIN = -2 ** 31


def _strict_upper(n):
    return jnp.where(_iota((n, n), 0) < _iota((n, n), 1), 1.0, 0.0).astype(BF16)


def _select_tile(key, thr, need, carry, tri):
    eqf = jnp.where(key == thr, 1.0, 0.0)
    prefix = _dot(eqf.astype(BF16), tri) + carry
    self_ = jnp.where(key > thr, 1.0, jnp.where(prefix < need, eqf, 0.0))
    return self_, carry + jnp.sum(eqf, axis=1, keepdims=True)


RADIX_BITS = 3


def _kth_largest_key(count_ge, rows, k_sel):
    kk = float(k_sel)
    zero = jnp.zeros((rows, 1), I32)
    thr = jnp.where(count_ge(zero) >= kk, zero, jnp.full((rows, 1), INT_MIN, I32))

    def digit_step(thr, shift, n_digits):
        digit = jnp.zeros((rows, 1), I32)
        for m in range(1, n_digits):
            cand = thr | jnp.left_shift(jnp.int32(m), shift)
            digit = digit + jnp.where(count_ge(cand) >= kk, 1, 0)
        return thr | jnp.left_shift(digit, shift)

    n_steps, first_bits = divmod(31, RADIX_BITS)
    if first_bits:
        thr = digit_step(thr, jnp.int32(31 - first_bits), 1 << first_bits)
    return lax.fori_loop(
        0, n_steps,
        lambda t, thr: digit_step(thr, 31 - first_bits - RADIX_BITS * (t + 1), 1 << RADIX_BITS), thr)


def _strict_lower(n):
    return jnp.where(_iota((n, n), 1) < _iota((n, n), 0), 1.0, 0.0).astype(BF16)


def _dsa_prompt_t_kernel(bias_ref, qb_ref, qi_ref, wit_ref, kb_ref, vbt_ref, ki_ref, o_ref,
                         key_scr, bias_scr, m_scr, l_scr, acc_scr, ties_scr, *, k_sel):
    b, i = pl.program_id(0), pl.program_id(1)

    @pl.when((b == 0) & (i == 0))
    def _():
        _build_bias_tiles_t(bias_ref, bias_scr, H_A, H_B)

    key_ix = _iota((ATT_TILE, ATT_TILE), 0)
    qry_ix = _iota((ATT_TILE, ATT_TILE), 1)
    idx_scale = 1.0 / math.sqrt(D_IDX * H_IDX)

    qi = qi_ref[...]
    q_cat = []
    for h in range(H_IDX):
        hi, lo = _split_bf16(qi[:, h * D_IDX:(h + 1) * D_IDX])
        q_cat.append(jnp.concatenate([hi, lo, hi], axis=1))
    wit = wit_ref[0]
    w_row = [wit[h:h + 1, :] for h in range(H_IDX)]

    def score_body(j, _):
        ks = pl.multiple_of(j * ATT_TILE, ATT_TILE)
        hi, lo = _split_bf16(ki_ref[pl.ds(ks, ATT_TILE), :])
        k_cat = jnp.concatenate([hi, hi, lo], axis=1)
        acc = jnp.zeros((ATT_TILE, ATT_TILE), F32)
        for h in range(H_IDX):
            acc = acc + jnp.maximum(_dot_nt(k_cat, q_cat[h]), 0.0) * w_row[h]
        causal = (key_ix + (j - i) * ATT_TILE) <= qry_ix
        key_scr[j] = _sort_key(jnp.where(causal, acc * idx_scale, NEG_INF))
        return 0

    lax.fori_loop(0, i + 1, score_body, 0)

    def count_ge(cand):
        def body(j, cnt):
            return cnt + jnp.sum(jnp.where(key_scr[j] >= cand, 1.0, 0.0), axis=0, keepdims=True)
        return lax.fori_loop(0, i + 1, body, jnp.zeros((1, ATT_TILE), F32))

    kk = float(k_sel)
    zero = jnp.zeros((1, ATT_TILE), I32)
    thr0 = jnp.where(count_ge(zero) >= kk, zero, jnp.full((1, ATT_TILE), INT_MIN, I32))

    def bit_body(t, thr):
        cand = thr | lax.shift_left(jnp.int32(1), 30 - t)
        return jnp.where(count_ge(cand) >= kk, cand, thr)

    thr = lax.fori_loop(0, 31, bit_body, thr0)

    def above_and_ties_body(j, carry):
        above, ties = carry
        key = key_scr[j]
        ties_scr[j] = ties
        above = above + jnp.sum(jnp.where(key > thr, 1.0, 0.0), axis=0, keepdims=True)
        ties = ties + jnp.sum(jnp.where(key == thr, 1.0, 0.0), axis=0, keepdims=True)
        return above, ties

    zero_row = jnp.zeros((1, ATT_TILE), F32)
    above, _ = lax.fori_loop(0, i + 1, above_and_ties_body, (zero_row, zero_row))
    need = kk - above

    q = qb_ref[...] * (1.0 / math.sqrt(HEAD_DIM))
    q_grp = [jnp.concatenate([q[:, h * HEAD_DIM:(h + 1) * HEAD_DIM].astype(BF16)
                              for h in range(n * G_B, (n + 1) * G_B)], axis=0) for n in range(KV_B)]
    far_shift = jnp.concatenate([jnp.full((1, ATT_TILE), bias_ref[NUM_BUCKETS - 1, H_A + h], F32)
                                 for h in range(H_B)], axis=1)
    lower = _strict_lower(ATT_TILE)
    scrs = (m_scr, l_scr, acc_scr)
    wide = H_B * ATT_TILE

    def additive_mask(j):
        key = key_scr[j]
        eqf = jnp.where(key == thr, 1.0, 0.0)
        earlier_ties = _dot(lower, eqf.astype(BF16)) + ties_scr[j]
        member = jnp.where(key > thr, 1.0, jnp.where(earlier_ties < need, eqf, 0.0))
        causal = (key_ix + (j - i) * ATT_TILE) <= qry_ix
        return jnp.where(causal, jnp.where(member > 0.0, 0.0, NEG_INF), NEG_INF)

    def attend(j, bias_tile, state):
        ks = pl.multiple_of(j * ATT_TILE, ATT_TILE)
        negmask = additive_mask(j)
        scores, values = [], []
        for n in range(KV_B):
            cs = slice(n * HEAD_DIM, (n + 1) * HEAD_DIM)
            kt = kb_ref[pl.ds(ks, ATT_TILE), cs].astype(BF16)
            scores.append(_dot_nt(kt, q_grp[n]))
            values += [vbt_ref[0, cs, pl.ds(ks, ATT_TILE)].astype(BF16)] * G_B
        tiles = [bias_tile(h) for h in range(H_B)]
        if tiles[0] is None:
            add = jnp.concatenate([negmask] * H_B, axis=1)
            return _softmax_fold(state, jnp.concatenate(scores, axis=1) + add, values,
                                 shift=far_shift)
        add = jnp.concatenate([t + negmask for t in tiles], axis=1)
        return _softmax_fold(state, jnp.concatenate(scores, axis=1) + add, values)

    empty = (jnp.full((1, wide), NEG_INF, F32), jnp.zeros((1, wide), F32),
             jnp.zeros((HEAD_DIM, wide), F32))
    _store_states(scrs, [attend(i, lambda h: bias_scr[h, 1], empty)])

    @pl.when(i >= 1)
    def _():
        _store_states(scrs, [attend(i - 1, lambda h: bias_scr[h, 0], _load_states(scrs, 1)[0])])

    _, l, acc = lax.fori_loop(0, i - 1, lambda j, st: attend(j, lambda h: None, st),
                              _load_states(scrs, 1)[0])
    out_t = acc / l
    o_ref[...] = jnp.concatenate([out_t[:, h * ATT_TILE:(h + 1) * ATT_TILE] for h in range(H_B)],
                                 axis=0).T


def _dsa_prompt_t(qb2, qi2, wit3, kb2, vbt3, ki2, rel_bias, batch, seq):
    assert ATT_TILE >= FAR_DIST
    nq = seq // ATT_TILE
    k_sel = min(DSA_TOPK, seq // 4)
    qmap = lambda b, i: (b * nq + i, 0)
    kmap = lambda b, i: (b, 0)
    return pl.pallas_call(
        functools.partial(_dsa_prompt_t_kernel, k_sel=k_sel),
        grid=(batch, nq),
        in_specs=[pl.BlockSpec(memory_space=pltpu.SMEM),
                  pl.BlockSpec((ATT_TILE, H_B * HEAD_DIM), qmap),
                  pl.BlockSpec((ATT_TILE, H_IDX * D_IDX), qmap),
                  pl.BlockSpec((1, H_IDX, ATT_TILE), lambda b, i: (b, 0, i)),
                  pl.BlockSpec((seq, KV_B * HEAD_DIM), kmap),
                  pl.BlockSpec((1, KV_B * HEAD_DIM, seq), lambda b, i: (b, 0, 0)),
                  pl.BlockSpec((seq, D_IDX), kmap)],
        out_specs=pl.BlockSpec((ATT_TILE, H_B * HEAD_DIM), qmap),
        out_shape=jax.ShapeDtypeStruct((batch * seq, H_B * HEAD_DIM), F32),
        scratch_shapes=[pltpu.VMEM((nq, ATT_TILE, ATT_TILE), I32),
                        pltpu.VMEM((H_B, 2, ATT_TILE, ATT_TILE), F32),
                        pltpu.VMEM((1, 1, H_B * ATT_TILE), F32),
                        pltpu.VMEM((1, 1, H_B * ATT_TILE), F32),
                        pltpu.VMEM((1, HEAD_DIM, H_B * ATT_TILE), F32),
                        pltpu.VMEM((nq, 1, ATT_TILE), F32)],
        compiler_params=_params(("arbitrary", "arbitrary")),
        name="dsa_prompt",
    )(rel_bias, qb2, qi2, wit3, kb2, vbt3, ki2)


N_CACHES = 5
TAIL_ROWS = LANES


def _softmax_rows(s):
    m = jnp.max(s, axis=1, keepdims=True)
    p = jnp.exp(s - m)
    return p / jnp.sum(p, axis=1, keepdims=True)


def _stack_heads(x, n_heads, width):
    return jnp.concatenate([x[:, h * width:(h + 1) * width] for h in range(n_heads)], axis=0)


def _sample_attn_kernel(pt_ref, bias_a_ref, bias_b_ref, qa_ref, ka_ref, va_ref, qb_ref, kb_ref,
                        vb_ref, qi_ref, ki_ref, wi_ref, cka, cva, ckb, cvb, cki,
                        ya_ref, yb_ref, ka_buf, va_buf, kb_buf, vb_buf, ki_buf, sem,
                        bias_a_scr, bias_b_scr, *, n_seq, n_pages, page, ts):
    b = pl.program_id(0)
    slot = b % 2
    past = n_pages * page
    lp = past + TAIL_ROWS
    rows = H_A * ts
    caches = (cka, cva, ckb, cvb, cki)
    bufs = (ka_buf, va_buf, kb_buf, vb_buf, ki_buf)

    def page_copies(seq, slot_):
        cps = []
        for p in range(n_pages):
            pg = pt_ref[seq, p]
            for c in range(N_CACHES):
                cps.append(pltpu.make_async_copy(caches[c].at[pg], bufs[c].at[slot_, p],
                                                 sem.at[slot_, c]))
        return cps

    @pl.when(b == 0)
    def _():
        for cp in page_copies(0, 0):
            cp.start()
        dist = past + _iota((rows, lp), 0) % ts - _iota((rows, lp), 1)
        bias_a_scr[...] = _bias_by_distance(dist, lambda k: bias_a_ref[:, k:k + 1])
        bias_b_scr[...] = _bias_by_distance(dist, lambda k: bias_b_ref[:, k:k + 1])

    @pl.when(b + 1 < n_seq)
    def _():
        for cp in page_copies(b + 1, 1 - slot):
            cp.start()

    for cp in page_copies(b, slot):
        cp.wait()

    def tail(new_ref):
        new = new_ref[0]
        return jnp.concatenate([new, jnp.zeros((TAIL_ROWS - ts, new.shape[1]), F32)], axis=0)

    def scores(q, buf, new_ref, dot_page, dot_new):
        tiles = [dot_page(q, buf[slot, p]) for p in range(n_pages)]
        return jnp.concatenate(tiles + [dot_new(q, tail(new_ref))], axis=1)

    def weighted_values(p, buf, new_ref):
        pb = p.astype(BF16)
        o = _dot(pb[:, past:], tail(new_ref).astype(BF16))
        for pg in range(n_pages):
            o = o + _dot_nt(pb[:, pg * page:(pg + 1) * page], buf[slot, pg].astype(BF16))
        return o

    bf_page = lambda q, kt: _dot(q, kt.astype(BF16))
    bf_new = lambda q, k: _dot_nt(q, k.astype(BF16))

    colpos = _iota((ts, lp), 1)
    qpos = _iota((ts, lp), 0)
    valid = colpos <= past + qpos
    valid_rows = jnp.where(_iota((rows, lp), 1) <= past + _iota((rows, lp), 0) % ts, 1.0, 0.0)
    row_head = _iota((rows, 1), 0) // ts

    n_past_blk = past // MOBA_BLOCK
    pages_per_blk = MOBA_BLOCK // page
    qa = qa_ref[0] * (1.0 / math.sqrt(HEAD_DIM))
    width_a = H_A * HEAD_DIM
    q_rows = jnp.concatenate([qa] * H_A, axis=0)
    own_head_a = (_iota((rows, width_a), 1) // HEAD_DIM) == row_head
    q_bd = jnp.where(own_head_a, q_rows, 0.0)
    blk_of_lane = _iota((width_a, LANES), 1)
    kmean_t = jnp.zeros((width_a, LANES), F32)
    for n in range(n_past_blk):
        tot = ka_buf[slot, n * pages_per_blk]
        for pp in range(1, pages_per_blk):
            tot = tot + ka_buf[slot, n * pages_per_blk + pp]
        mean_n = jnp.sum(tot, axis=1, keepdims=True) * (1.0 / MOBA_BLOCK)
        kmean_t = jnp.where(blk_of_lane == n, mean_n, kmean_t)
    gate = _dot3(q_bd, kmean_t)
    blk_lane = _iota((rows, LANES), 1).astype(F32)
    gate = jnp.where(blk_lane < float(n_past_blk), gate, NEG_INF)
    selmask = jnp.zeros((rows, LANES), F32)
    for _ in range(min(MOBA_TOPK, n_past_blk)):
        mx = jnp.max(gate, axis=1, keepdims=True)
        first = jnp.min(jnp.where(gate == mx, blk_lane, float(LANES)), axis=1, keepdims=True)
        hit = blk_lane == first
        selmask = jnp.where(hit, 1.0, selmask)
        gate = jnp.where(hit, NEG_INF, gate)
    expand = jnp.where(_iota((LANES, lp), 1) // MOBA_BLOCK == _iota((LANES, lp), 0),
                       1.0, 0.0).astype(BF16)
    in_sel_blk = _dot(selmask.astype(BF16), expand)
    keep_a = jnp.where(_iota((rows, lp), 1) >= past, valid_rows, in_sel_blk) > 0.0
    s = scores(q_bd.astype(BF16), ka_buf, ka_ref, bf_page, bf_new) + bias_a_scr[...]
    p = _softmax_rows(jnp.where(keep_a, s, NEG_INF))
    o = jnp.where(own_head_a, weighted_values(p, va_buf, va_ref), 0.0)
    ya = o[0:ts]
    for h in range(1, H_A):
        ya = ya + o[h * ts:(h + 1) * ts]
    ya_ref[0] = ya

    k_sel = min(DSA_TOPK, (past + ts) // 4)
    qi_rows = _stack_heads(qi_ref[0], H_IDX, D_IDX)
    wi = wi_ref[0]
    w_rows = jnp.concatenate([wi[:, h:h + 1] for h in range(H_IDX)], axis=0)
    s_idx = scores(qi_rows, ki_buf, ki_ref, _dot3, functools.partial(_dot3, nt=True))
    contrib = jnp.maximum(s_idx, 0.0) * w_rows
    score = contrib[0:ts]
    for h in range(1, H_IDX):
        score = score + contrib[h * ts:(h + 1) * ts]
    score = jnp.where(valid, score * (1.0 / math.sqrt(D_IDX * H_IDX)), NEG_INF)
    key = _sort_key(score)

    def count_ge(cand):
        return jnp.sum(jnp.where(key >= cand, 1.0, 0.0), axis=1, keepdims=True)

    thr = _kth_largest_key(count_ge, ts, k_sel)
    need = float(k_sel) - jnp.sum(jnp.where(key > thr, 1.0, 0.0), axis=1, keepdims=True)
    tri = _strict_upper(LANES)
    carry = jnp.zeros((ts, 1), F32)
    sel_chunks = []
    for c in range(lp // LANES):
        sel_c, carry = _select_tile(key[:, c * LANES:(c + 1) * LANES], thr, need, carry, tri)
        sel_chunks.append(sel_c)
    keep = jnp.where(valid, jnp.concatenate(sel_chunks, axis=1), 0.0)
    keep_b = jnp.concatenate([keep] * H_B, axis=0) > 0.0

    qb = qb_ref[0] * (1.0 / math.sqrt(HEAD_DIM))
    qb_rows = _stack_heads(qb, H_B, HEAD_DIM)
    width_b = KV_B * HEAD_DIM
    own_kv = (_iota((rows, width_b), 1) // HEAD_DIM) == row_head // G_B
    qb_bd = jnp.where(own_kv, jnp.concatenate([qb_rows] * KV_B, axis=1), 0.0)
    s = scores(qb_bd.astype(BF16), kb_buf, kb_ref, bf_page, bf_new) + bias_b_scr[...]
    p = _softmax_rows(jnp.where(keep_b, s, NEG_INF))
    o = weighted_values(p, vb_buf, vb_ref)
    pieces = []
    for h in range(H_B):
        n = h // G_B
        pieces.append(o[h * ts:(h + 1) * ts, n * HEAD_DIM:(n + 1) * HEAD_DIM])
    yb_ref[0] = jnp.concatenate(pieces, axis=1)


def _sample_attn(new_rows, caches_t, page_table, rel_bias, n_seq, ts):
    n_pages = page_table.shape[1]
    page = caches_t[0].shape[2]
    past = n_pages * page
    assert past % MOBA_BLOCK == 0 and MOBA_BLOCK % page == 0 and page == LANES and ts <= SUBLANES
    assert past // MOBA_BLOCK <= LANES
    lp = past + TAIL_ROWS
    rows = H_A * ts
    bias_a_rows = jnp.repeat(rel_bias[:, :H_A].T, ts, axis=0)
    bias_b_rows = jnp.repeat(rel_bias[:, H_A:].T, ts, axis=0)
    full = lambda a: pl.BlockSpec(a.shape, lambda b, pt: (0, 0))
    seq_spec = lambda a: pl.BlockSpec((1, ts, a.shape[2]), lambda b, pt: (b, 0, 0))
    grid_spec = pltpu.PrefetchScalarGridSpec(
        num_scalar_prefetch=1,
        grid=(n_seq,),
        in_specs=[full(bias_a_rows), full(bias_b_rows)] + [seq_spec(a) for a in new_rows]
                 + [pl.BlockSpec(memory_space=pl.ANY)] * N_CACHES,
        out_specs=[pl.BlockSpec((1, ts, H_A * HEAD_DIM), lambda b, pt: (b, 0, 0)),
                   pl.BlockSpec((1, ts, H_B * HEAD_DIM), lambda b, pt: (b, 0, 0))],
        scratch_shapes=[pltpu.VMEM((2, n_pages, c.shape[1], page), F32) for c in caches_t]
                       + [pltpu.SemaphoreType.DMA((2, N_CACHES)),
                          pltpu.VMEM((rows, lp), F32), pltpu.VMEM((rows, lp), F32)],
    )
    return pl.pallas_call(
        functools.partial(_sample_attn_kernel, n_seq=n_seq, n_pages=n_pages, page=page, ts=ts),
        grid_spec=grid_spec,
        out_shape=[jax.ShapeDtypeStruct((n_seq, ts, H_A * HEAD_DIM), F32),
                   jax.ShapeDtypeStruct((n_seq, ts, H_B * HEAD_DIM), F32)],
        compiler_params=_params(("arbitrary",)),
        name="sample_attn",
    )(page_table, bias_a_rows, bias_b_rows, *new_rows, *caches_t)


def _attn_out_kernel(ya_ref, yb_ref, x_ref, w_ref, g_ref, b_ref, o_ref, *, alpha):
    wa = ya_ref.shape[1]
    y = (_dot(ya_ref[...].astype(BF16), w_ref[0:wa, :])
         + _dot(yb_ref[...].astype(BF16), w_ref[wa:, :]))
    o_ref[...] = _layer_norm(alpha * x_ref[...] + y, g_ref[...], b_ref[...])


def _attn_out(ya2, yb2, x2, w_out, g, b, alpha):
    n, d = x2.shape
    tm = min(ROW_TILE, n)
    row = lambda c: pl.BlockSpec((tm, c), lambda i: (i, 0))
    full = lambda a: pl.BlockSpec(a.shape, lambda i: (0, 0))
    return pl.pallas_call(
        functools.partial(_attn_out_kernel, alpha=alpha),
        grid=(n // tm,),
        in_specs=[row(ya2.shape[1]), row(yb2.shape[1]), row(d), full(w_out), full(g), full(b)],
        out_specs=row(d),
        out_shape=jax.ShapeDtypeStruct((n, d), F32),
        compiler_params=_params(("arbitrary",)),
        name="attn_out_ln",
    )(ya2, yb2, x2, w_out, g, b)


def _conv_core(x, u, u1, u2, gate_b, cw_ref, wout_ref, g_ref, b_ref, alpha):
    c = u2 * cw_ref[0:1, :] + u1 * cw_ref[1:2, :] + u * cw_ref[2:3, :]
    y = _dot((gate_b * c).astype(BF16), wout_ref[...])
    return _layer_norm(alpha * x + y, g_ref[...], b_ref[...])


def _conv_in(x, win_ref, dc):
    xb = x.astype(BF16)
    h = _dot(xb, win_ref[:, 0:dc])
    gate_b = _dot(xb, win_ref[:, dc:2 * dc])
    gate_c = _dot(xb, win_ref[:, 2 * dc:3 * dc])
    return gate_c * h, gate_b


def _conv_prompt_kernel(x_ref, win_ref, cw_ref, wout_ref, g_ref, b_ref, o_ref, st_ref, tail_scr,
                        *, alpha):
    tm, dc = x_ref.shape[0], wout_ref.shape[0]

    @pl.when(pl.program_id(1) == 0)
    def _():
        tail_scr[...] = jnp.zeros(tail_scr.shape, F32)

    x = x_ref[...]
    u, gate_b = _conv_in(x, win_ref, dc)
    ext = jnp.concatenate([tail_scr[...], u], axis=0)
    u1 = ext[SUBLANES - 1:SUBLANES - 1 + tm]
    u2 = ext[SUBLANES - 2:SUBLANES - 2 + tm]
    o_ref[...] = _conv_core(x, u, u1, u2, gate_b, cw_ref, wout_ref, g_ref, b_ref, alpha)
    tail_scr[...] = u[tm - SUBLANES:tm]
    st_ref[0] = u[tm - (CONV_W - 1):tm]


def _conv_prompt(x2, w_in, conv_w, w_out, g, b, alpha, batch, seq):
    n, d = x2.shape
    dc = w_out.shape[0]
    tm = min(ROW_TILE, seq)
    nt = seq // tm
    full = lambda a: pl.BlockSpec(a.shape, lambda bb, i: (0, 0))
    return pl.pallas_call(
        functools.partial(_conv_prompt_kernel, alpha=alpha),
        grid=(batch, nt),
        in_specs=[pl.BlockSpec((tm, d), lambda bb, i: (bb * nt + i, 0)),
                  full(w_in), full(conv_w), full(w_out), full(g), full(b)],
        out_specs=[pl.BlockSpec((tm, d), lambda bb, i: (bb * nt + i, 0)),
                   pl.BlockSpec((1, CONV_W - 1, dc), lambda bb, i: (bb, 0, 0))],
        out_shape=[jax.ShapeDtypeStruct((n, d), F32),
                   jax.ShapeDtypeStruct((batch, CONV_W - 1, dc), F32)],
        scratch_shapes=[pltpu.VMEM((SUBLANES, dc), F32)],
        compiler_params=_params(("arbitrary", "arbitrary")),
        name="conv_prompt_ln",
    )(x2, w_in, conv_w, w_out, g, b)


def _conv_sample_kernel(x_ref, p0_ref, p1_ref, win_ref, cw_ref, wout_ref, g_ref, b_ref,
                        o_ref, u_ref, *, alpha, ts):
    n, dc = x_ref.shape[0], wout_ref.shape[0]
    x = x_ref[...]
    u, gate_b = _conv_in(x, win_ref, dc)
    ext = jnp.concatenate([jnp.zeros((SUBLANES, dc), F32), u], axis=0)
    tpos = _iota((n, dc), 0) % ts
    u1 = jnp.where(tpos == 0, p1_ref[...], ext[SUBLANES - 1:SUBLANES - 1 + n])
    u2 = jnp.where(tpos == 0, p0_ref[...],
                   jnp.where(tpos == 1, p1_ref[...], ext[SUBLANES - 2:SUBLANES - 2 + n]))
    o_ref[...] = _conv_core(x, u, u1, u2, gate_b, cw_ref, wout_ref, g_ref, b_ref, alpha)
    u_ref[...] = u


def _conv_sample(x2, past0, past1, w_in, conv_w, w_out, g, b, alpha, ts):
    n, d = x2.shape
    dc = w_out.shape[0]
    tm = min(ROW_TILE, n)
    assert tm % ts == 0
    full = lambda a: pl.BlockSpec(a.shape, lambda i: (0, 0))
    row = lambda c: pl.BlockSpec((tm, c), lambda i: (i, 0))
    return pl.pallas_call(
        functools.partial(_conv_sample_kernel, alpha=alpha, ts=ts),
        grid=(n // tm,),
        in_specs=[row(d), row(dc), row(dc), full(w_in), full(conv_w), full(w_out), full(g), full(b)],
        out_specs=[row(d), row(dc)],
        out_shape=[jax.ShapeDtypeStruct((n, d), F32), jax.ShapeDtypeStruct((n, dc), F32)],
        compiler_params=_params(("arbitrary",)),
        name="conv_sample_ln",
    )(x2, past0, past1, w_in, conv_w, w_out, g, b)


DMA_UNROLL = 8
GROUP_LANE0 = 0
EXPERT_LANE0 = 32


def _first_argmax(v, lane_f):
    mx = jnp.max(v, axis=1, keepdims=True)
    first = jnp.min(jnp.where(v == mx, lane_f, float(LANES)), axis=1, keepdims=True)
    return mx, first


def _router_kernel(x_ref, w_ref, bias_ref, cin_ref, e0_ref, e1_ref, g0_ref, g1_ref, r0_ref, r1_ref,
                   cnt_ref, carry_scr):
    tm = x_ref.shape[0]

    @pl.when(pl.program_id(0) == 0)
    def _():
        carry_scr[...] = cin_ref[...]

    logits = _dot3(x_ref[...], w_ref[...]) + bias_ref[...]
    lane = _iota((tm, LANES), 1)
    lane_f = lane.astype(F32)
    gl = jnp.where((lane >= GROUP_LANE0) & (lane < GROUP_LANE0 + N_GROUPS), logits, NEG_INF)
    gmax, gfirst = _first_argmax(gl, lane_f)
    pg = jnp.exp(gl - gmax)
    g_val = jnp.max(pg / jnp.sum(pg, axis=1, keepdims=True), axis=1, keepdims=True)
    g_idx = gfirst - float(GROUP_LANE0)
    grp_of_lane = ((lane - EXPERT_LANE0) // EXPERTS_PER_GROUP).astype(F32)
    in_grp = (lane >= EXPERT_LANE0) & (lane < EXPERT_LANE0 + N_EXPERTS) & (grp_of_lane == g_idx)
    el = jnp.where(in_grp, logits, NEG_INF)
    emax, first0 = _first_argmax(el, lane_f)
    pe = jnp.exp(el - emax)
    pe = pe / jnp.sum(pe, axis=1, keepdims=True)
    hit0 = lane_f == first0
    v0 = jnp.max(jnp.where(hit0, pe, 0.0), axis=1, keepdims=True)
    _, first1 = _first_argmax(jnp.where(hit0, NEG_INF, el), lane_f)
    hit1 = lane_f == first1
    v1 = jnp.max(jnp.where(hit1, pe, 0.0), axis=1, keepdims=True)
    denom = v0 + v1
    g0_ref[...] = g_val * v0 / denom
    g1_ref[...] = g_val * v1 / denom
    e0_ref[...] = first0.astype(I32) - EXPERT_LANE0
    e1_ref[...] = first1.astype(I32) - EXPERT_LANE0
    onehot = jnp.where(hit0 | hit1, 1.0, 0.0)
    lower = jnp.where(_iota((tm, tm), 1) < _iota((tm, tm), 0), 1.0, 0.0).astype(BF16)
    before = _dot(lower, onehot.astype(BF16)) + carry_scr[...]
    r0_ref[...] = jnp.sum(jnp.where(hit0, before, 0.0), axis=1, keepdims=True).astype(I32)
    r1_ref[...] = jnp.sum(jnp.where(hit1, before, 0.0), axis=1, keepdims=True).astype(I32)
    carry_scr[...] = carry_scr[...] + jnp.sum(onehot, axis=0, keepdims=True)
    cnt_ref[...] = carry_scr[...]


def _router(x2, w_slab, b_slab, counts_in):
    n, d = x2.shape
    tm = min(ROW_TILE, n)
    full = lambda a: pl.BlockSpec(a.shape, lambda i: (0, 0))
    col = pl.BlockSpec((tm, 1), lambda i: (i, 0))
    return pl.pallas_call(
        _router_kernel,
        grid=(n // tm,),
        in_specs=[pl.BlockSpec((tm, d), lambda i: (i, 0)), full(w_slab), full(b_slab), full(counts_in)],
        out_specs=[col] * 6 + [pl.BlockSpec((1, LANES), lambda i: (0, 0))],
        out_shape=[jax.ShapeDtypeStruct((n, 1), I32), jax.ShapeDtypeStruct((n, 1), I32),
                   jax.ShapeDtypeStruct((n, 1), F32), jax.ShapeDtypeStruct((n, 1), F32),
                   jax.ShapeDtypeStruct((n, 1), I32), jax.ShapeDtypeStruct((n, 1), I32),
                   jax.ShapeDtypeStruct((1, LANES), F32)],
        scratch_shapes=[pltpu.VMEM((1, LANES), F32)],
        compiler_params=_params(("arbitrary",)),
        name="moe_router",
    )(x2, w_slab, b_slab, counts_in)


def _dispatch_kernel(e0_ref, e1_ref, r0_ref, r1_ref, start_ref, x_ref, xr_in, xr_out, sem):
    del xr_in
    tm = x_ref.shape[0]
    base = pl.program_id(0) * tm

    def copies(t):
        p0 = start_ref[e0_ref[base + t]] + r0_ref[base + t]
        p1 = start_ref[e1_ref[base + t]] + r1_ref[base + t]
        src = x_ref.at[pl.ds(t, 1)]
        return (pltpu.make_async_copy(src, xr_out.at[pl.ds(p0, 1)], sem),
                pltpu.make_async_copy(src, xr_out.at[pl.ds(p1, 1)], sem))

    def start(t, _):
        for cp in copies(t):
            cp.start()
        return 0

    def wait(t, _):
        for cp in copies(t):
            cp.wait()
        return 0

    lax.fori_loop(0, tm, start, 0, unroll=DMA_UNROLL)
    lax.fori_loop(0, tm, wait, 0, unroll=DMA_UNROLL)


def _dispatch(route, start_pad, x2, xr):
    n, d = x2.shape
    tm = min(ROW_TILE, n)
    grid_spec = pltpu.PrefetchScalarGridSpec(
        num_scalar_prefetch=5,
        grid=(n // tm,),
        in_specs=[pl.BlockSpec((tm, d), lambda i, *_: (i, 0)), pl.BlockSpec(memory_space=pl.ANY)],
        out_specs=pl.BlockSpec(memory_space=pl.ANY),
        scratch_shapes=[pltpu.SemaphoreType.DMA(())],
    )
    return pl.pallas_call(
        _dispatch_kernel,
        grid_spec=grid_spec,
        out_shape=jax.ShapeDtypeStruct(xr.shape, F32),
        input_output_aliases={6: 0},
        compiler_params=_params(("arbitrary",)),
        name="moe_dispatch",
    )(*route, start_pad, x2, xr)


def _ffn_kernel(be_ref, nu_ref, x_ref, wg_ref, wu_ref, wd_ref, o_ref, wg_s, wu_s, wd_s):
    i = pl.program_id(0)

    @pl.when(i < nu_ref[0])
    def _():
        @pl.when((i == 0) | (be_ref[i] != be_ref[jnp.maximum(i - 1, 0)]))
        def _():
            wg_s[...] = wg_ref[0, 0].astype(BF16)
            wu_s[...] = wu_ref[0, 0].astype(BF16)
            wd_s[...] = wd_ref[0, 0].astype(BF16)

        xb = x_ref[...].astype(BF16)
        gate = _dot(xb, wg_s[...])
        hidden = gate * jax.nn.sigmoid(gate) * _dot(xb, wu_s[...])
        o_ref[...] = _dot(hidden.astype(BF16), wd_s[...])

    @pl.when(i >= nu_ref[0])
    def _():
        o_ref[...] = jnp.zeros(o_ref.shape, F32)


def _grouped_ffn(block_expert, n_used, xr, w_gate, w_up, w_down, layer):
    rows, d = xr.shape
    de = w_gate.shape[3]
    nblk = rows // FFN_ROWS
    grid_spec = pltpu.PrefetchScalarGridSpec(
        num_scalar_prefetch=2,
        grid=(nblk,),
        in_specs=[pl.BlockSpec((FFN_ROWS, d), lambda i, be, nu: (jnp.minimum(i, nu[0] - 1), 0)),
                  pl.BlockSpec((1, 1, d, de), lambda i, be, nu: (layer, be[i], 0, 0)),
                  pl.BlockSpec((1, 1, d, de), lambda i, be, nu: (layer, be[i], 0, 0)),
                  pl.BlockSpec((1, 1, de, d), lambda i, be, nu: (layer, be[i], 0, 0))],
        out_specs=pl.BlockSpec((FFN_ROWS, d), lambda i, be, nu: (i, 0)),
        scratch_shapes=[pltpu.VMEM((d, de), BF16), pltpu.VMEM((d, de), BF16), pltpu.VMEM((de, d), BF16)],
    )
    return pl.pallas_call(
        _ffn_kernel,
        grid_spec=grid_spec,
        out_shape=jax.ShapeDtypeStruct((rows, d), F32),
        compiler_params=_params(("arbitrary",)),
        name="moe_ffn",
    )(block_expert, n_used, xr, w_gate, w_up, w_down)


def _combine_kernel(e0_ref, e1_ref, r0_ref, r1_ref, start_ref, x_ref, g0_ref, g1_ref, lg_ref, lb_ref,
                    yr_ref, o_ref, ybuf, sem, *, alpha):
    tm = x_ref.shape[0]
    base = pl.program_id(0) * tm

    def copies(t):
        p0 = start_ref[e0_ref[base + t]] + r0_ref[base + t]
        p1 = start_ref[e1_ref[base + t]] + r1_ref[base + t]
        return (pltpu.make_async_copy(yr_ref.at[pl.ds(p0, 1)], ybuf.at[0, pl.ds(t, 1)], sem),
                pltpu.make_async_copy(yr_ref.at[pl.ds(p1, 1)], ybuf.at[1, pl.ds(t, 1)], sem))

    def start(t, _):
        for cp in copies(t):
            cp.start()
        return 0

    def wait(t, _):
        for cp in copies(t):
            cp.wait()
        return 0

    lax.fori_loop(0, tm, start, 0, unroll=DMA_UNROLL)
    lax.fori_loop(0, tm, wait, 0, unroll=DMA_UNROLL)
    y = ybuf[0] * g0_ref[...] + ybuf[1] * g1_ref[...]
    o_ref[...] = _layer_norm(alpha * x_ref[...] + y, lg_ref[...], lb_ref[...])


def _combine(route, start_pad, x2, g0, g1, ln_g, ln_b, yr, alpha):
    n, d = x2.shape
    tm = min(ROW_TILE, n)
    col = pl.BlockSpec((tm, 1), lambda i, *_: (i, 0))
    vec = pl.BlockSpec((1, d), lambda i, *_: (0, 0))
    grid_spec = pltpu.PrefetchScalarGridSpec(
        num_scalar_prefetch=5,
        grid=(n // tm,),
        in_specs=[pl.BlockSpec((tm, d), lambda i, *_: (i, 0)), col, col, vec, vec,
                  pl.BlockSpec(memory_space=pl.ANY)],
        out_specs=pl.BlockSpec((tm, d), lambda i, *_: (i, 0)),
        scratch_shapes=[pltpu.VMEM((2, tm, d), F32), pltpu.SemaphoreType.DMA(())],
    )
    return pl.pallas_call(
        functools.partial(_combine_kernel, alpha=alpha),
        grid_spec=grid_spec,
        out_shape=jax.ShapeDtypeStruct((n, d), F32),
        compiler_params=_params(("arbitrary",)),
        name="moe_combine_ln",
    )(*route, start_pad, x2, g0, g1, ln_g, ln_b, yr)


def _hier_moe_ln(xs, w_rg, b_rg, w_re, b_re, w_gate, w_up, w_down, layer, ln_g, ln_b, alpha):
    d = xs[0].shape[1]
    w_slab = jnp.zeros((d, LANES), F32)
    w_slab = w_slab.at[:, GROUP_LANE0:GROUP_LANE0 + N_GROUPS].set(w_rg)
    w_slab = w_slab.at[:, EXPERT_LANE0:EXPERT_LANE0 + N_EXPERTS].set(w_re)
    b_slab = jnp.zeros((1, LANES), F32)
    b_slab = b_slab.at[0, GROUP_LANE0:GROUP_LANE0 + N_GROUPS].set(b_rg)
    b_slab = b_slab.at[0, EXPERT_LANE0:EXPERT_LANE0 + N_EXPERTS].set(b_re)

    counts = jnp.zeros((1, LANES), F32)
    routes, gates = [], []
    for x2 in xs:
        e0, e1, g0, g1, r0, r1, counts = _router(x2, w_slab, b_slab, counts)
        routes.append(tuple(a.reshape(-1) for a in (e0, e1, r0, r1)))
        gates.append((g0, g1))

    cnt = counts[0, EXPERT_LANE0:EXPERT_LANE0 + N_EXPERTS].astype(I32)
    padded = (cnt + FFN_ROWS - 1) // FFN_ROWS * FFN_ROWS
    end_pad = jnp.cumsum(padded)
    start_pad = (end_pad - padded).astype(I32)
    n_assign = 2 * sum(x2.shape[0] for x2 in xs)
    nblk = -(-n_assign // FFN_ROWS) + N_EXPERTS
    blk_start = jnp.arange(nblk, dtype=I32) * FFN_ROWS
    block_expert = jnp.minimum(
        jnp.sum((end_pad[None, :] <= blk_start[:, None]).astype(I32), axis=1), N_EXPERTS - 1)
    n_used = (end_pad[-1:] // FFN_ROWS).astype(I32)

    xr = jnp.zeros((nblk * FFN_ROWS, d), F32)
    for x2, route in zip(xs, routes):
        xr = _dispatch(route, start_pad, x2, xr)
    yr = _grouped_ffn(block_expert, n_used, xr, w_gate, w_up, w_down, layer)
    return [_combine(route, start_pad, x2, g0, g1, ln_g, ln_b, yr, alpha)
            for x2, route, (g0, g1) in zip(xs, routes, gates)]


def _stack(arrays):
    return arrays[0][None] if len(arrays) == 1 else jnp.stack(arrays)


def kernel(x_prompt, x_sample, cache_k_a, cache_v_a, cache_k_b, cache_v_b, cache_k_idx, state_conv, page_table, rel_bias, w_attn_in, w_attn_out, w_conv_in, conv_w, w_conv_out, w_router_group, b_router_group, w_router_expert, b_router_expert, w_exp_gate, w_exp_up, w_exp_down, ln_g, ln_b):
    depth = ln_g.shape[0]
    alpha = float((2 * depth) ** 0.25)
    bp, tp, d = x_prompt.shape
    bs, ts, _ = x_sample.shape
    assert ts >= CONV_W - 1
    xp = x_prompt.reshape(bp * tp, d)
    xs = x_sample.reshape(bs * ts, d)
    attn_p, attn_s, conv_p, conv_s = [], [], [], []
    kv_cols = (1, 2, 4, 5, 7)
    for layer in range(depth):
        i = layer // 2
        g_mix, b_mix = ln_g[layer, 0][None], ln_b[layer, 0][None]
        if layer % 2 == 0:
            w_in = jnp.pad(w_attn_in[i], ((0, 0), (0, _ATTN_IN_PAD - _ATTN_IN))).astype(BF16)
            w_out = w_attn_out[i].astype(BF16)
            qa, ka, qb, kb, qi, ki, kat, vat, kbt, vbt, kit, wit = _attn_inproj_prompt(
                xp, w_in, _transposed_proj_weight(w_attn_in[i]), bp, tp)
            proj_s = _attn_inproj(xs, w_in)
            ya_p = _moba_prompt8(qa, ka, vat, rel_bias, bp, tp)
            yb_p = _dsa_prompt_t(qb, qi, wit, kb, vbt, ki, rel_bias, bp, tp)
            caches_t = [jnp.moveaxis(c[i], 1, -1).reshape(c.shape[1], -1, c.shape[2])
                        for c in (cache_k_a, cache_v_a, cache_k_b, cache_v_b, cache_k_idx)]
            ya_s, yb_s = _sample_attn([a.reshape(bs, ts, -1) for a in proj_s], caches_t,
                                      page_table, rel_bias, bs, ts)
            xp = _attn_out(ya_p, yb_p, xp, w_out, g_mix, b_mix, alpha)
            xs = _attn_out(ya_s.reshape(bs * ts, -1), yb_s.reshape(bs * ts, -1), xs, w_out,
                           g_mix, b_mix, alpha)
            attn_p.append([kat, vat, kbt, vbt, kit])
            attn_s.append([proj_s[c] for c in kv_cols])
        else:
            w_in = w_conv_in[i].astype(BF16)
            w_out = w_conv_out[i].astype(BF16)
            xp, st_p = _conv_prompt(xp, w_in, conv_w[i], w_out, g_mix, b_mix, alpha, bp, tp)
            past = state_conv[i]
            xs, u_s = _conv_sample(xs, jnp.repeat(past[:, 0], ts, axis=0),
                                   jnp.repeat(past[:, 1], ts, axis=0),
                                   w_in, conv_w[i], w_out, g_mix, b_mix, alpha, ts)
            conv_p.append(st_p)
            conv_s.append(u_s.reshape(bs, ts, -1)[:, ts - (CONV_W - 1):])
        xp, xs = _hier_moe_ln([xp, xs], w_router_group[layer], b_router_group[layer],
                              w_router_expert[layer], b_router_expert[layer],
                              w_exp_gate, w_exp_up, w_exp_down, layer,
                              ln_g[layer, 1][None], ln_b[layer, 1][None], alpha)

    def rows(group, j, b, t, shape):
        return _stack([layer_rows[j].reshape((b, t) + shape) for layer_rows in group])

    def rows_t(group, j, b, t, shape):
        return _stack([jnp.moveaxis(layer_rows[j].reshape((b,) + shape + (t,)), -1, 1)
                       for layer_rows in group])

    head_shapes = ((H_A, HEAD_DIM), (H_A, HEAD_DIM), (KV_B, HEAD_DIM), (KV_B, HEAD_DIM), (D_IDX,))
    return ((xp.reshape(bp, tp, d), xs.reshape(bs, ts, d))
            + tuple(rows_t(attn_p, j, bp, tp, head_shapes[j]) for j in range(5))
            + (_stack(conv_p),)
            + tuple(rows(attn_s, j, bs, ts, head_shapes[j]) for j in range(5))
            + (_stack(conv_s),))
```

```python
import functools
import math

import numpy as np
import jax
import jax.numpy as jnp
from jax import lax
from jax.experimental import pallas as pl
from jax.experimental.pallas import tpu as pltpu

F32 = jnp.float32
BF16 = jnp.bfloat16
I32 = jnp.int32

HEAD_DIM = 64
H_A = 8
H_B = 8
KV_B = 2
G_B = H_B // KV_B
H_IDX = 8
D_IDX = 64
MOBA_BLOCK = 256
MOBA_TOPK = 3
DSA_TOPK = 256
NUM_BUCKETS = 32
MAX_DISTANCE = 128
CONV_W = 3
N_GROUPS = 4
EXPERTS_PER_GROUP = 8
N_EXPERTS = N_GROUPS * EXPERTS_PER_GROUP
LN_EPS = 1e-5
ATTN_COLS = (H_A * HEAD_DIM, H_A * HEAD_DIM, H_A * HEAD_DIM,
             H_B * HEAD_DIM, KV_B * HEAD_DIM, KV_B * HEAD_DIM,
             H_IDX * D_IDX, D_IDX, H_IDX)

LANES = 128
SUBLANES = 8
VMEM_LIMIT = 56 * 1024 * 1024

ATT_TILE = MOBA_BLOCK
ROW_TILE = 512
FFN_ROWS = 256
NEG_INF = float("-inf")


def _bucket_lows():
    n = np.arange(0, 4 * MAX_DISTANCE)
    max_exact = NUM_BUCKETS // 2
    nf = np.maximum(n, 1).astype(np.float32)
    large = max_exact + (np.log(nf / max_exact) / math.log(MAX_DISTANCE / max_exact)
                         * (NUM_BUCKETS - max_exact)).astype(np.int32)
    bucket = np.where(n < max_exact, n, np.minimum(large, NUM_BUCKETS - 1))
    assert np.all(np.diff(bucket) >= 0) and bucket[-1] == NUM_BUCKETS - 1
    lows = [int(np.argmax(bucket == k)) for k in range(NUM_BUCKETS)]
    assert all(bucket[lo] == k for k, lo in enumerate(lows))
    return tuple(lows)


BUCKET_LOWS = _bucket_lows()
FAR_DIST = BUCKET_LOWS[-1]


def _params(sem):
    return pltpu.CompilerParams(dimension_semantics=sem, vmem_limit_bytes=VMEM_LIMIT)


def _dot(a, b):
    return jnp.dot(a, b, preferred_element_type=F32)


def _dot_nt(a, b):
    return lax.dot_general(a, b, (((1,), (1,)), ((), ())), preferred_element_type=F32)


def _split_bf16(a):
    hi = a.astype(BF16)
    lo = (a - hi.astype(F32)).astype(BF16)
    return hi, lo


def _dot3(a, b, nt=False):
    ah, al = _split_bf16(a)
    bh, bl = _split_bf16(b)
    d = _dot_nt if nt else _dot
    return d(ah, bh) + (d(ah, bl) + d(al, bh))


def _iota(shape, dim):
    return lax.broadcasted_iota(I32, shape, dim)


def _layer_norm(z, g, b):
    mu = jnp.mean(z, axis=-1, keepdims=True)
    zc = z - mu
    var = jnp.mean(zc * zc, axis=-1, keepdims=True)
    return zc * lax.rsqrt(var + LN_EPS) * g + b


def _bias_by_distance(dist, bias_of_bucket):
    acc = jnp.zeros(dist.shape, F32) + bias_of_bucket(0)
    for k in range(1, NUM_BUCKETS):
        acc = jnp.where(dist >= BUCKET_LOWS[k], bias_of_bucket(k), acc)
    return acc


def _sort_key(x):
    bits = lax.bitcast_convert_type(x + 0.0, I32)
    return bits ^ (lax.shift_right_arithmetic(bits, 31) & 0x7FFFFFFF)


_ATTN_OFFS = tuple(int(v) for v in np.cumsum((0,) + ATTN_COLS[:-1]))
_ATTN_IN = sum(ATTN_COLS)
_ATTN_IN_PAD = -(-_ATTN_IN // LANES) * LANES
_TAIL_OFF = _ATTN_OFFS[7]


def _attn_inproj_kernel(x_ref, w_ref, qa, ka, va, qb, kb, vb, qi, ki, wi):
    xb = x_ref[...].astype(BF16)
    for o, off, wd in zip((qa, ka, va, qb, kb, vb, qi), _ATTN_OFFS, ATTN_COLS):
        o[...] = _dot(xb, w_ref[:, off:off + wd])
    tail = _dot(xb, w_ref[:, _TAIL_OFF:_ATTN_IN_PAD])
    ki[...] = tail[:, :D_IDX]
    wi[...] = tail[:, D_IDX:D_IDX + H_IDX]


def _attn_inproj(x2, w_pad):
    n, d = x2.shape
    tm = min(ROW_TILE, n)
    outs = tuple(jax.ShapeDtypeStruct((n, c), F32) for c in ATTN_COLS)
    return pl.pallas_call(
        _attn_inproj_kernel,
        grid=(n // tm,),
        in_specs=[pl.BlockSpec((tm, d), lambda i: (i, 0)),
                  pl.BlockSpec((d, _ATTN_IN_PAD), lambda i: (0, 0))],
        out_specs=tuple(pl.BlockSpec((tm, c), lambda i: (i, 0)) for c in ATTN_COLS),
        out_shape=outs,
        compiler_params=_params(("arbitrary",)),
        name="attn_inproj",
    )(x2, w_pad)


_T_GROUPS = (1, 2, 4, 5, 7, 8)
_T_ROWS = tuple(ATTN_COLS[g] for g in _T_GROUPS)
_T_OFFS = tuple(int(v) for v in np.cumsum((0,) + _T_ROWS[:-1]))
BF16_ROWS = 2 * SUBLANES
_T_TOTAL = -(-sum(_T_ROWS) // BF16_ROWS) * BF16_ROWS
_N_GROUPS_NORMAL = (0, 1, 3, 4, 6, 7)


def _attn_inproj_prompt_kernel(x_ref, w_ref, wt_ref, qa, ka, qb, kb, qi, ki,
                               kat, vat, kbt, vbt, kit, wit):
    xb = x_ref[...].astype(BF16)
    for o, g in zip((qa, ka, qb, kb, qi), _N_GROUPS_NORMAL[:5]):
        o[...] = _dot(xb, w_ref[:, _ATTN_OFFS[g]:_ATTN_OFFS[g] + ATTN_COLS[g]])
    ki[...] = _dot(xb, w_ref[:, _TAIL_OFF:_ATTN_IN_PAD])[:, :D_IDX]
    for o, off, rows in zip((kat, vat, kbt, vbt, kit, wit), _T_OFFS, _T_ROWS):
        padded = -(-rows // BF16_ROWS) * BF16_ROWS
        o[0] = _dot_nt(wt_ref[off:off + padded, :], xb)[:rows]


def _transposed_proj_weight(w_in):
    cols = [w_in[:, _ATTN_OFFS[g]:_ATTN_OFFS[g] + ATTN_COLS[g]] for g in _T_GROUPS]
    wt = jnp.concatenate(cols, axis=1).T
    return jnp.pad(wt, ((0, _T_TOTAL - wt.shape[0]), (0, 0))).astype(BF16)


def _attn_inproj_prompt(x2, w_pad, w_t, batch, seq):
    n, d = x2.shape
    tm = min(ROW_TILE, seq)
    nt = seq // tm
    normal = [ATTN_COLS[g] for g in _N_GROUPS_NORMAL]
    return pl.pallas_call(
        _attn_inproj_prompt_kernel,
        grid=(batch, nt),
        in_specs=[pl.BlockSpec((tm, d), lambda b, i: (b * nt + i, 0)),
                  pl.BlockSpec(w_pad.shape, lambda b, i: (0, 0)),
                  pl.BlockSpec(w_t.shape, lambda b, i: (0, 0))],
        out_specs=[pl.BlockSpec((tm, c), lambda b, i: (b * nt + i, 0)) for c in normal]
                  + [pl.BlockSpec((1, r, tm), lambda b, i: (b, 0, i)) for r in _T_ROWS],
        out_shape=[jax.ShapeDtypeStruct((n, c), F32) for c in normal]
                  + [jax.ShapeDtypeStruct((batch, r, seq), F32) for r in _T_ROWS],
        compiler_params=_params(("arbitrary", "arbitrary")),
        name="attn_inproj_prompt",
    )(x2, w_pad, w_t)


def _build_bias_tiles_t(bias_ref, scr, head0, n_heads):
    key = _iota((ATT_TILE, ATT_TILE), 0)
    qry = _iota((ATT_TILE, ATT_TILE), 1)
    for t in range(2):
        dist = qry - key + (ATT_TILE if t == 0 else 0)
        for h in range(n_heads):
            scr[h, t] = _bias_by_distance(dist, lambda k, h=h: bias_ref[k, head0 + h])


def _softmax_fold(state, s, vt, shift=None):
    m_old, l_old, acc_old = state
    tile_max = jnp.max(s, axis=0, keepdims=True)
    if shift is not None:
        tile_max = tile_max + shift
    m_new = jnp.maximum(m_old, tile_max)
    m_safe = jnp.where(m_new == NEG_INF, 0.0, m_new)
    alpha = jnp.exp(m_old - m_safe)
    p = jnp.exp(s - (m_safe if shift is None else m_safe - shift))
    pb = p.astype(BF16)
    if isinstance(vt, (list, tuple)):
        pv = jnp.concatenate([_dot(v, pb[:, c * ATT_TILE:(c + 1) * ATT_TILE])
                              for c, v in enumerate(vt)], axis=1)
    else:
        pv = _dot(vt, pb)
    return m_new, alpha * l_old + jnp.sum(p, axis=0, keepdims=True), alpha * acc_old + pv


def _store_states(scrs, states):
    for h, state in enumerate(states):
        for scr, value in zip(scrs, state):
            scr[h] = value


def _load_states(scrs, n_heads):
    return tuple(tuple(scr[h] for scr in scrs) for h in range(n_heads))


def _moba_prompt_kernel8(bias_ref, q_ref, k_ref, vt_ref, o_ref, kmean_scr, bias_scr, sel_scr,
                         m_scr, l_scr, acc_scr, *, nblk):
    b, i = pl.program_id(0), pl.program_id(1)

    @pl.when((b == 0) & (i == 0))
    def _():
        _build_bias_tiles_t(bias_ref, bias_scr, 0, H_A)

    @pl.when(i == 0)
    def _():
        for n in range(nblk):
            kmean_scr[n:n + 1, :] = jnp.mean(
                k_ref[n * MOBA_BLOCK:(n + 1) * MOBA_BLOCK, :], axis=0, keepdims=True)

    q = q_ref[...] * (1.0 / math.sqrt(HEAD_DIM))
    key = _iota((ATT_TILE, ATT_TILE), 0)
    qry = _iota((ATT_TILE, ATT_TILE), 1)
    blk = _iota((nblk, ATT_TILE), 0).astype(F32)
    heads = [slice(h * HEAD_DIM, (h + 1) * HEAD_DIM) for h in range(H_A)]
    q_h = []
    for h in range(H_A):
        qh = q[:, heads[h]]
        q_h.append(qh.astype(BF16))
        gate = _dot3(kmean_scr[:, heads[h]], qh, nt=True)
        gate = jnp.where(blk < i.astype(F32), gate, NEG_INF)
        selmask = jnp.zeros((nblk, ATT_TILE), F32)
        for _ in range(min(MOBA_TOPK, nblk)):
            mx = jnp.max(gate, axis=0, keepdims=True)
            first = jnp.min(jnp.where(gate == mx, blk, float(nblk)), axis=0, keepdims=True)
            hit = blk == first
            selmask = jnp.where(hit & (mx > NEG_INF), 1.0, selmask)
            gate = jnp.where(hit, NEG_INF, gate)
        sel_scr[h] = jnp.where(selmask > 0.0, 0.0, NEG_INF)

    def k_tile(ks, h):
        return k_ref[pl.ds(ks, ATT_TILE), heads[h]].astype(BF16)

    def vt_tile(ks, h):
        return vt_ref[0, heads[h], pl.ds(ks, ATT_TILE)].astype(BF16)

    scrs = (m_scr, l_scr, acc_scr)
    wide = H_A * ATT_TILE
    empty = (jnp.full((1, wide), NEG_INF, F32), jnp.zeros((1, wide), F32),
             jnp.zeros((HEAD_DIM, wide), F32))
    far_shift = jnp.concatenate(
        [jnp.full((1, ATT_TILE), bias_ref[NUM_BUCKETS - 1, h], F32) for h in range(H_A)], axis=1)

    def fold(state, ks, add, shift=None):
        s = jnp.concatenate([_dot_nt(k_tile(ks, h), q_h[h]) + add(h) for h in range(H_A)], axis=1)
        return _softmax_fold(state, s, [vt_tile(ks, h) for h in range(H_A)], shift=shift)

    start = pl.multiple_of(i * ATT_TILE, ATT_TILE)
    causal = jnp.where(key <= qry, 0.0, NEG_INF)
    _store_states(scrs, [fold(empty, start, lambda h: bias_scr[h, 1] + causal)])

    @pl.when(i >= 1)
    def _():
        ks = pl.multiple_of((i - 1) * ATT_TILE, ATT_TILE)
        _store_states(scrs, [fold(_load_states(scrs, 1)[0], ks,
                                  lambda h: bias_scr[h, 0] + sel_scr[h, pl.ds(i - 1, 1), :])])

    def far_body(j, state):
        ks = pl.multiple_of(j * ATT_TILE, ATT_TILE)
        return fold(state, ks, lambda h: sel_scr[h, pl.ds(j, 1), :], shift=far_shift)

    _, l, acc = lax.fori_loop(0, i - 1, far_body, _load_states(scrs, 1)[0])
    out_t = acc / l
    o_ref[...] = jnp.concatenate([out_t[:, h * ATT_TILE:(h + 1) * ATT_TILE] for h in range(H_A)],
                                 axis=0).T


def _moba_prompt8(q2, k2, vt3, rel_bias, batch, seq):
    assert ATT_TILE >= FAR_DIST
    nq = seq // ATT_TILE
    nblk = seq // MOBA_BLOCK
    width = q2.shape[1]
    return pl.pallas_call(
        functools.partial(_moba_prompt_kernel8, nblk=nblk),
        grid=(batch, nq),
        in_specs=[pl.BlockSpec(memory_space=pltpu.SMEM),
                  pl.BlockSpec((ATT_TILE, width), lambda b, i: (b * nq + i, 0)),
                  pl.BlockSpec((seq, width), lambda b, i: (b, 0)),
                  pl.BlockSpec((1, width, seq), lambda b, i: (b, 0, 0))],
        out_specs=pl.BlockSpec((ATT_TILE, width), lambda b, i: (b * nq + i, 0)),
        out_shape=jax.ShapeDtypeStruct((batch * seq, width), F32),
        scratch_shapes=[pltpu.VMEM((nblk, width), F32),
                        pltpu.VMEM((H_A, 2, ATT_TILE, ATT_TILE), F32),
                        pltpu.VMEM((H_A, nblk, ATT_TILE), F32),
                        pltpu.VMEM((1, 1, H_A * ATT_TILE), F32),
                        pltpu.VMEM((1, 1, H_A * ATT_TILE), F32),
                        pltpu.VMEM((1, HEAD_DIM, H_A * ATT_TILE), F32)],
        compiler_params=_params(("arbitrary", "arbitrary")),
        name="moba_prompt",
    )(rel_bias, q2, k2, vt3)


INT_MIN = -2 ** 31


def _strict_upper(n):
    return jnp.where(_iota((n, n), 0) < _iota((n, n), 1), 1.0, 0.0).astype(BF16)


def _select_tile(key, thr, need, carry, tri):
    eqf = jnp.where(key == thr, 1.0, 0.0)
    prefix = _dot(eqf.astype(BF16), tri) + carry
    self_ = jnp.where(key > thr, 1.0, jnp.where(prefix < need, eqf, 0.0))
    return self_, carry + jnp.sum(eqf, axis=1, keepdims=True)


RADIX_BITS = 3


def _kth_largest_key(count_ge, rows, k_sel):
    kk = float(k_sel)
    zero = jnp.zeros((rows, 1), I32)
    thr = jnp.where(count_ge(zero) >= kk, zero, jnp.full((rows, 1), INT_MIN, I32))

    def digit_step(thr, shift, n_digits):
        digit = jnp.zeros((rows, 1), I32)
        for m in range(1, n_digits):
            cand = thr | jnp.left_shift(jnp.int32(m), shift)
            digit = digit + jnp.where(count_ge(cand) >= kk, 1, 0)
        return thr | jnp.left_shift(digit, shift)

    n_steps, first_bits = divmod(31, RADIX_BITS)
    if first_bits:
        thr = digit_step(thr, jnp.int32(31 - first_bits), 1 << first_bits)
    return lax.fori_loop(
        0, n_steps,
        lambda t, thr: digit_step(thr, 31 - first_bits - RADIX_BITS * (t + 1), 1 << RADIX_BITS), thr)


def _strict_lower(n):
    return jnp.where(_iota((n, n), 1) < _iota((n, n), 0), 1.0, 0.0).astype(BF16)


def _dsa_prompt_t_kernel(bias_ref, qb_ref, qi_ref, wit_ref, kb_ref, vbt_ref, ki_ref, o_ref,
                         key_scr, bias_scr, m_scr, l_scr, acc_scr, ties_scr, *, k_sel):
    b, i = pl.program_id(0), pl.program_id(1)

    @pl.when((b == 0) & (i == 0))
    def _():
        _build_bias_tiles_t(bias_ref, bias_scr, H_A, H_B)

    key_ix = _iota((ATT_TILE, ATT_TILE), 0)
    qry_ix = _iota((ATT_TILE, ATT_TILE), 1)
    idx_scale = 1.0 / math.sqrt(D_IDX * H_IDX)

    qi = qi_ref[...]
    q_cat = []
    for h in range(H_IDX):
        hi, lo = _split_bf16(qi[:, h * D_IDX:(h + 1) * D_IDX])
        q_cat.append(jnp.concatenate([hi, lo, hi], axis=1))
    wit = wit_ref[0]
    w_row = [wit[h:h + 1, :] for h in range(H_IDX)]

    def score_body(j, _):
        ks = pl.multiple_of(j * ATT_TILE, ATT_TILE)
        hi, lo = _split_bf16(ki_ref[pl.ds(ks, ATT_TILE), :])
        k_cat = jnp.concatenate([hi, hi, lo], axis=1)
        acc = jnp.zeros((ATT_TILE, ATT_TILE), F32)
        for h in range(H_IDX):
            acc = acc + jnp.maximum(_dot_nt(k_cat, q_cat[h]), 0.0) * w_row[h]
        causal = (key_ix + (j - i) * ATT_TILE) <= qry_ix
        key_scr[j] = _sort_key(jnp.where(causal, acc * idx_scale, NEG_INF))
        return 0

    lax.fori_loop(0, i + 1, score_body, 0)

    def count_ge(cand):
        def body(j, part):
            hit = jnp.where(key_scr[j] >= cand, 1.0, 0.0)
            return part + jnp.sum(hit.reshape(ATT_TILE // SUBLANES, SUBLANES, ATT_TILE), axis=0)
        part = lax.fori_loop(0, i + 1, body, jnp.zeros((SUBLANES, ATT_TILE), F32))
        return jnp.sum(part, axis=0, keepdims=True)

    kk = float(k_sel)
    zero = jnp.zeros((1, ATT_TILE), I32)
    thr0 = jnp.where(count_ge(zero) >= kk, zero, jnp.full((1, ATT_TILE), INT_MIN, I32))

    def bit_body(t, thr):
        cand = thr | lax.shift_left(jnp.int32(1), 30 - t)
        return jnp.where(count_ge(cand) >= kk, cand, thr)

    thr = lax.fori_loop(0, 31, bit_body, thr0)

    def above_and_ties_body(j, carry):
        above, ties = carry
        key = key_scr[j]
        ties_scr[j] = ties
        above = above + jnp.sum(jnp.where(key > thr, 1.0, 0.0), axis=0, keepdims=True)
        ties = ties + jnp.sum(jnp.where(key == thr, 1.0, 0.0), axis=0, keepdims=True)
        return above, ties

    zero_row = jnp.zeros((1, ATT_TILE), F32)
    above, _ = lax.fori_loop(0, i + 1, above_and_ties_body, (zero_row, zero_row))
    need = kk - above

    q = qb_ref[...] * (1.0 / math.sqrt(HEAD_DIM))
    q_grp = [jnp.concatenate([q[:, h * HEAD_DIM:(h + 1) * HEAD_DIM].astype(BF16)
                              for h in range(n * G_B, (n + 1) * G_B)], axis=0) for n in range(KV_B)]
    far_shift = jnp.concatenate([jnp.full((1, ATT_TILE), bias_ref[NUM_BUCKETS - 1, H_A + h], F32)
                                 for h in range(H_B)], axis=1)
    lower = _strict_lower(ATT_TILE)
    scrs = (m_scr, l_scr, acc_scr)
    wide = H_B * ATT_TILE

    def additive_mask(j):
        key = key_scr[j]
        eqf = jnp.where(key == thr, 1.0, 0.0)
        earlier_ties = _dot(lower, eqf.astype(BF16)) + ties_scr[j]
        member = jnp.where(key > thr, 1.0, jnp.where(earlier_ties < need, eqf, 0.0))
        causal = (key_ix + (j - i) * ATT_TILE) <= qry_ix
        return jnp.where(causal, jnp.where(member > 0.0, 0.0, NEG_INF), NEG_INF)

    def attend(j, bias_tile, state):
        ks = pl.multiple_of(j * ATT_TILE, ATT_TILE)
        negmask = additive_mask(j)
        scores, values = [], []
        for n in range(KV_B):
            cs = slice(n * HEAD_DIM, (n + 1) * HEAD_DIM)
            kt = kb_ref[pl.ds(ks, ATT_TILE), cs].astype(BF16)
            scores.append(_dot_nt(kt, q_grp[n]))
            values += [vbt_ref[0, cs, pl.ds(ks, ATT_TILE)].astype(BF16)] * G_B
        tiles = [bias_tile(h) for h in range(H_B)]
        if tiles[0] is None:
            add = jnp.concatenate([negmask] * H_B, axis=1)
            return _softmax_fold(state, jnp.concatenate(scores, axis=1) + add, values,
                                 shift=far_shift)
        add = jnp.concatenate([t + negmask for t in tiles], axis=1)
        return _softmax_fold(state, jnp.concatenate(scores, axis=1) + add, values)

    empty = (jnp.full((1, wide), NEG_INF, F32), jnp.zeros((1, wide), F32),
             jnp.zeros((HEAD_DIM, wide), F32))
    _store_states(scrs, [attend(i, lambda h: bias_scr[h, 1], empty)])

    @pl.when(i >= 1)
    def _():
        _store_states(scrs, [attend(i - 1, lambda h: bias_scr[h, 0], _load_states(scrs, 1)[0])])

    _, l, acc = lax.fori_loop(0, i - 1, lambda j, st: attend(j, lambda h: None, st),
                              _load_states(scrs, 1)[0])
    out_t = acc / l
    o_ref[...] = jnp.concatenate([out_t[:, h * ATT_TILE:(h + 1) * ATT_TILE] for h in range(H_B)],
                                 axis=0).T


def _dsa_prompt_t(qb2, qi2, wit3, kb2, vbt3, ki2, rel_bias, batch, seq):
    assert ATT_TILE >= FAR_DIST
    nq = seq // ATT_TILE
    k_sel = min(DSA_TOPK, seq // 4)
    qmap = lambda b, i: (b * nq + i, 0)
    kmap = lambda b, i: (b, 0)
    return pl.pallas_call(
        functools.partial(_dsa_prompt_t_kernel, k_sel=k_sel),
        grid=(batch, nq),
        in_specs=[pl.BlockSpec(memory_space=pltpu.SMEM),
                  pl.BlockSpec((ATT_TILE, H_B * HEAD_DIM), qmap),
                  pl.BlockSpec((ATT_TILE, H_IDX * D_IDX), qmap),
                  pl.BlockSpec((1, H_IDX, ATT_TILE), lambda b, i: (b, 0, i)),
                  pl.BlockSpec((seq, KV_B * HEAD_DIM), kmap),
                  pl.BlockSpec((1, KV_B * HEAD_DIM, seq), lambda b, i: (b, 0, 0)),
                  pl.BlockSpec((seq, D_IDX), kmap)],
        out_specs=pl.BlockSpec((ATT_TILE, H_B * HEAD_DIM), qmap),
        out_shape=jax.ShapeDtypeStruct((batch * seq, H_B * HEAD_DIM), F32),
        scratch_shapes=[pltpu.VMEM((nq, ATT_TILE, ATT_TILE), I32),
                        pltpu.VMEM((H_B, 2, ATT_TILE, ATT_TILE), F32),
                        pltpu.VMEM((1, 1, H_B * ATT_TILE), F32),
                        pltpu.VMEM((1, 1, H_B * ATT_TILE), F32),
                        pltpu.VMEM((1, HEAD_DIM, H_B * ATT_TILE), F32),
                        pltpu.VMEM((nq, 1, ATT_TILE), F32)],
        compiler_params=_params(("arbitrary", "arbitrary")),
        name="dsa_prompt",
    )(rel_bias, qb2, qi2, wit3, kb2, vbt3, ki2)


N_CACHES = 5
TAIL_ROWS = LANES


def _softmax_rows(s):
    m = jnp.max(s, axis=1, keepdims=True)
    p = jnp.exp(s - m)
    return p / jnp.sum(p, axis=1, keepdims=True)


def _stack_heads(x, n_heads, width):
    return jnp.concatenate([x[:, h * width:(h + 1) * width] for h in range(n_heads)], axis=0)


def _sample_attn_kernel(pt_ref, bias_a_ref, bias_b_ref, qa_ref, ka_ref, va_ref, qb_ref, kb_ref,
                        vb_ref, qi_ref, ki_ref, wi_ref, cka, cva, ckb, cvb, cki,
                        ya_ref, yb_ref, ka_buf, va_buf, kb_buf, vb_buf, ki_buf, sem,
                        bias_a_scr, bias_b_scr, *, n_seq, n_pages, page, ts):
    b = pl.program_id(0)
    slot = b % 2
    past = n_pages * page
    lp = past + TAIL_ROWS
    rows = H_A * ts
    caches = (cka, cva, ckb, cvb, cki)
    bufs = (ka_buf, va_buf, kb_buf, vb_buf, ki_buf)

    def page_copies(seq, slot_):
        cps = []
        for p in range(n_pages):
            pg = pt_ref[seq, p]
            for c in range(N_CACHES):
                cps.append(pltpu.make_async_copy(caches[c].at[pg], bufs[c].at[slot_, p],
                                                 sem.at[slot_, c]))
        return cps

    @pl.when(b == 0)
    def _():
        for cp in page_copies(0, 0):
            cp.start()
        dist = past + _iota((rows, lp), 0) % ts - _iota((rows, lp), 1)
        bias_a_scr[...] = _bias_by_distance(dist, lambda k: bias_a_ref[:, k:k + 1])
        bias_b_scr[...] = _bias_by_distance(dist, lambda k: bias_b_ref[:, k:k + 1])

    @pl.when(b + 1 < n_seq)
    def _():
        for cp in page_copies(b + 1, 1 - slot):
            cp.start()

    for cp in page_copies(b, slot):
        cp.wait()

    def tail(new_ref):
        new = new_ref[0]
        return jnp.concatenate([new, jnp.zeros((TAIL_ROWS - ts, new.shape[1]), F32)], axis=0)

    def scores(q, buf, new_ref, dot_page, dot_new):
        tiles = [dot_page(q, buf[slot, p]) for p in range(n_pages)]
        return jnp.concatenate(tiles + [dot_new(q, tail(new_ref))], axis=1)

    def weighted_values(p, buf, new_ref):
        pb = p.astype(BF16)
        o = _dot(pb[:, past:], tail(new_ref).astype(BF16))
        for pg in range(n_pages):
            o = o + _dot_nt(pb[:, pg * page:(pg + 1) * page], buf[slot, pg].astype(BF16))
        return o

    bf_page = lambda q, kt: _dot(q, kt.astype(BF16))
    bf_new = lambda q, k: _dot_nt(q, k.astype(BF16))

    colpos = _iota((ts, lp), 1)
    qpos = _iota((ts, lp), 0)
    valid = colpos <= past + qpos
    valid_rows = jnp.where(_iota((rows, lp), 1) <= past + _iota((rows, lp), 0) % ts, 1.0, 0.0)
    row_head = _iota((rows, 1), 0) // ts

    n_past_blk = past // MOBA_BLOCK
    pages_per_blk = MOBA_BLOCK // page
    qa = qa_ref[0] * (1.0 / math.sqrt(HEAD_DIM))
    width_a = H_A * HEAD_DIM
    q_rows = jnp.concatenate([qa] * H_A, axis=0)
    own_head_a = (_iota((rows, width_a), 1) // HEAD_DIM) == row_head
    q_bd = jnp.where(own_head_a, q_rows, 0.0)
    blk_of_lane = _iota((width_a, LANES), 1)
    kmean_t = jnp.zeros((width_a, LANES), F32)
    for n in range(n_past_blk):
        tot = ka_buf[slot, n * pages_per_blk]
        for pp in range(1, pages_per_blk):
            tot = tot + ka_buf[slot, n * pages_per_blk + pp]
        mean_n = jnp.sum(tot, axis=1, keepdims=True) * (1.0 / MOBA_BLOCK)
        kmean_t = jnp.where(blk_of_lane == n, mean_n, kmean_t)
    gate = _dot3(q_bd, kmean_t)
    blk_lane = _iota((rows, LANES), 1).astype(F32)
    gate = jnp.where(blk_lane < float(n_past_blk), gate, NEG_INF)
    selmask = jnp.zeros((rows, LANES), F32)
    for _ in range(min(MOBA_TOPK, n_past_blk)):
        mx = jnp.max(gate, axis=1, keepdims=True)
        first = jnp.min(jnp.where(gate == mx, blk_lane, float(LANES)), axis=1, keepdims=True)
        hit = blk_lane == first
        selmask = jnp.where(hit, 1.0, selmask)
        gate = jnp.where(hit, NEG_INF, gate)
    expand = jnp.where(_iota((LANES, lp), 1) // MOBA_BLOCK == _iota((LANES, lp), 0),
                       1.0, 0.0).astype(BF16)
    in_sel_blk = _dot(selmask.astype(BF16), expand)
    keep_a = jnp.where(_iota((rows, lp), 1) >= past, valid_rows, in_sel_blk) > 0.0
    s = scores(q_bd.astype(BF16), ka_buf, ka_ref, bf_page, bf_new) + bias_a_scr[...]
    p = _softmax_rows(jnp.where(keep_a, s, NEG_INF))
    o = jnp.where(own_head_a, weighted_values(p, va_buf, va_ref), 0.0)
    ya = o[0:ts]
    for h in range(1, H_A):
        ya = ya + o[h * ts:(h + 1) * ts]
    ya_ref[0] = ya

    k_sel = min(DSA_TOPK, (past + ts) // 4)
    qi_rows = _stack_heads(qi_ref[0], H_IDX, D_IDX)
    wi = wi_ref[0]
    w_rows = jnp.concatenate([wi[:, h:h + 1] for h in range(H_IDX)], axis=0)
    s_idx = scores(qi_rows, ki_buf, ki_ref, _dot3, functools.partial(_dot3, nt=True))
    contrib = jnp.maximum(s_idx, 0.0) * w_rows
    score = contrib[0:ts]
    for h in range(1, H_IDX):
        score = score + contrib[h * ts:(h + 1) * ts]
    score = jnp.where(valid, score * (1.0 / math.sqrt(D_IDX * H_IDX)), NEG_INF)
    key = _sort_key(score)

    def count_ge(cand):
        return jnp.sum(jnp.where(key >= cand, 1.0, 0.0), axis=1, keepdims=True)

    thr = _kth_largest_key(count_ge, ts, k_sel)
    need = float(k_sel) - jnp.sum(jnp.where(key > thr, 1.0, 0.0), axis=1, keepdims=True)
    tri = _strict_upper(LANES)
    carry = jnp.zeros((ts, 1), F32)
    sel_chunks = []
    for c in range(lp // LANES):
        sel_c, carry = _select_tile(key[:, c * LANES:(c + 1) * LANES], thr, need, carry, tri)
        sel_chunks.append(sel_c)
    keep = jnp.where(valid, jnp.concatenate(sel_chunks, axis=1), 0.0)
    keep_b = jnp.concatenate([keep] * H_B, axis=0) > 0.0

    qb = qb_ref[0] * (1.0 / math.sqrt(HEAD_DIM))
    qb_rows = _stack_heads(qb, H_B, HEAD_DIM)
    width_b = KV_B * HEAD_DIM
    own_kv = (_iota((rows, width_b), 1) // HEAD_DIM) == row_head // G_B
    qb_bd = jnp.where(own_kv, jnp.concatenate([qb_rows] * KV_B, axis=1), 0.0)
    s = scores(qb_bd.astype(BF16), kb_buf, kb_ref, bf_page, bf_new) + bias_b_scr[...]
    p = _softmax_rows(jnp.where(keep_b, s, NEG_INF))
    o = weighted_values(p, vb_buf, vb_ref)
    pieces = []
    for h in range(H_B):
        n = h // G_B
        pieces.append(o[h * ts:(h + 1) * ts, n * HEAD_DIM:(n + 1) * HEAD_DIM])
    yb_ref[0] = jnp.concatenate(pieces, axis=1)


def _sample_attn(new_rows, caches_t, page_table, rel_bias, n_seq, ts):
    n_pages = page_table.shape[1]
    page = caches_t[0].shape[2]
    past = n_pages * page
    assert past % MOBA_BLOCK == 0 and MOBA_BLOCK % page == 0 and page == LANES and ts <= SUBLANES
    assert past // MOBA_BLOCK <= LANES
    lp = past + TAIL_ROWS
    rows = H_A * ts
    bias_a_rows = jnp.repeat(rel_bias[:, :H_A].T, ts, axis=0)
    bias_b_rows = jnp.repeat(rel_bias[:, H_A:].T, ts, axis=0)
    full = lambda a: pl.BlockSpec(a.shape, lambda b, pt: (0, 0))
    seq_spec = lambda a: pl.BlockSpec((1, ts, a.shape[2]), lambda b, pt: (b, 0, 0))
    grid_spec = pltpu.PrefetchScalarGridSpec(
        num_scalar_prefetch=1,
        grid=(n_seq,),
        in_specs=[full(bias_a_rows), full(bias_b_rows)] + [seq_spec(a) for a in new_rows]
                 + [pl.BlockSpec(memory_space=pl.ANY)] * N_CACHES,
        out_specs=[pl.BlockSpec((1, ts, H_A * HEAD_DIM), lambda b, pt: (b, 0, 0)),
                   pl.BlockSpec((1, ts, H_B * HEAD_DIM), lambda b, pt: (b, 0, 0))],
        scratch_shapes=[pltpu.VMEM((2, n_pages, c.shape[1], page), F32) for c in caches_t]
                       + [pltpu.SemaphoreType.DMA((2, N_CACHES)),
                          pltpu.VMEM((rows, lp), F32), pltpu.VMEM((rows, lp), F32)],
    )
    return pl.pallas_call(
        functools.partial(_sample_attn_kernel, n_seq=n_seq, n_pages=n_pages, page=page, ts=ts),
        grid_spec=grid_spec,
        out_shape=[jax.ShapeDtypeStruct((n_seq, ts, H_A * HEAD_DIM), F32),
                   jax.ShapeDtypeStruct((n_seq, ts, H_B * HEAD_DIM), F32)],
        compiler_params=_params(("arbitrary",)),
        name="sample_attn",
    )(page_table, bias_a_rows, bias_b_rows, *new_rows, *caches_t)


def _attn_out_kernel(ya_ref, yb_ref, x_ref, w_ref, g_ref, b_ref, o_ref, *, alpha):
    wa = ya_ref.shape[1]
    y = (_dot(ya_ref[...].astype(BF16), w_ref[0:wa, :])
         + _dot(yb_ref[...].astype(BF16), w_ref[wa:, :]))
    o_ref[...] = _layer_norm(alpha * x_ref[...] + y, g_ref[...], b_ref[...])


def _attn_out(ya2, yb2, x2, w_out, g, b, alpha):
    n, d = x2.shape
    tm = min(ROW_TILE, n)
    row = lambda c: pl.BlockSpec((tm, c), lambda i: (i, 0))
    full = lambda a: pl.BlockSpec(a.shape, lambda i: (0, 0))
    return pl.pallas_call(
        functools.partial(_attn_out_kernel, alpha=alpha),
        grid=(n // tm,),
        in_specs=[row(ya2.shape[1]), row(yb2.shape[1]), row(d), full(w_out), full(g), full(b)],
        out_specs=row(d),
        out_shape=jax.ShapeDtypeStruct((n, d), F32),
        compiler_params=_params(("arbitrary",)),
        name="attn_out_ln",
    )(ya2, yb2, x2, w_out, g, b)


def _conv_core(x, u, u1, u2, gate_b, cw_ref, wout_ref, g_ref, b_ref, alpha):
    c = u2 * cw_ref[0:1, :] + u1 * cw_ref[1:2, :] + u * cw_ref[2:3, :]
    y = _dot((gate_b * c).astype(BF16), wout_ref[...])
    return _layer_norm(alpha * x + y, g_ref[...], b_ref[...])


def _conv_in(x, win_ref, dc):
    xb = x.astype(BF16)
    h = _dot(xb, win_ref[:, 0:dc])
    gate_b = _dot(xb, win_ref[:, dc:2 * dc])
    gate_c = _dot(xb, win_ref[:, 2 * dc:3 * dc])
    return gate_c * h, gate_b


def _conv_prompt_kernel(x_ref, win_ref, cw_ref, wout_ref, g_ref, b_ref, o_ref, st_ref, tail_scr,
                        *, alpha):
    tm, dc = x_ref.shape[0], wout_ref.shape[0]

    @pl.when(pl.program_id(1) == 0)
    def _():
        tail_scr[...] = jnp.zeros(tail_scr.shape, F32)

    x = x_ref[...]
    u, gate_b = _conv_in(x, win_ref, dc)
    ext = jnp.concatenate([tail_scr[...], u], axis=0)
    u1 = ext[SUBLANES - 1:SUBLANES - 1 + tm]
    u2 = ext[SUBLANES - 2:SUBLANES - 2 + tm]
    o_ref[...] = _conv_core(x, u, u1, u2, gate_b, cw_ref, wout_ref, g_ref, b_ref, alpha)
    tail_scr[...] = u[tm - SUBLANES:tm]
    st_ref[0] = u[tm - (CONV_W - 1):tm]


def _conv_prompt(x2, w_in, conv_w, w_out, g, b, alpha, batch, seq):
    n, d = x2.shape
    dc = w_out.shape[0]
    tm = min(ROW_TILE, seq)
    nt = seq // tm
    full = lambda a: pl.BlockSpec(a.shape, lambda bb, i: (0, 0))
    return pl.pallas_call(
        functools.partial(_conv_prompt_kernel, alpha=alpha),
        grid=(batch, nt),
        in_specs=[pl.BlockSpec((tm, d), lambda bb, i: (bb * nt + i, 0)),
                  full(w_in), full(conv_w), full(w_out), full(g), full(b)],
        out_specs=[pl.BlockSpec((tm, d), lambda bb, i: (bb * nt + i, 0)),
                   pl.BlockSpec((1, CONV_W - 1, dc), lambda bb, i: (bb, 0, 0))],
        out_shape=[jax.ShapeDtypeStruct((n, d), F32),
                   jax.ShapeDtypeStruct((batch, CONV_W - 1, dc), F32)],
        scratch_shapes=[pltpu.VMEM((SUBLANES, dc), F32)],
        compiler_params=_params(("arbitrary", "arbitrary")),
        name="conv_prompt_ln",
    )(x2, w_in, conv_w, w_out, g, b)


def _conv_sample_kernel(x_ref, p0_ref, p1_ref, win_ref, cw_ref, wout_ref, g_ref, b_ref,
                        o_ref, u_ref, *, alpha, ts):
    n, dc = x_ref.shape[0], wout_ref.shape[0]
    x = x_ref[...]
    u, gate_b = _conv_in(x, win_ref, dc)
    ext = jnp.concatenate([jnp.zeros((SUBLANES, dc), F32), u], axis=0)
    tpos = _iota((n, dc), 0) % ts
    u1 = jnp.where(tpos == 0, p1_ref[...], ext[SUBLANES - 1:SUBLANES - 1 + n])
    u2 = jnp.where(tpos == 0, p0_ref[...],
                   jnp.where(tpos == 1, p1_ref[...], ext[SUBLANES - 2:SUBLANES - 2 + n]))
    o_ref[...] = _conv_core(x, u, u1, u2, gate_b, cw_ref, wout_ref, g_ref, b_ref, alpha)
    u_ref[...] = u


def _conv_sample(x2, past0, past1, w_in, conv_w, w_out, g, b, alpha, ts):
    n, d = x2.shape
    dc = w_out.shape[0]
    tm = min(ROW_TILE, n)
    assert tm % ts == 0
    full = lambda a: pl.BlockSpec(a.shape, lambda i: (0, 0))
    row = lambda c: pl.BlockSpec((tm, c), lambda i: (i, 0))
    return pl.pallas_call(
        functools.partial(_conv_sample_kernel, alpha=alpha, ts=ts),
        grid=(n // tm,),
        in_specs=[row(d), row(dc), row(dc), full(w_in), full(conv_w), full(w_out), full(g), full(b)],
        out_specs=[row(d), row(dc)],
        out_shape=[jax.ShapeDtypeStruct((n, d), F32), jax.ShapeDtypeStruct((n, dc), F32)],
        compiler_params=_params(("arbitrary",)),
        name="conv_sample_ln",
    )(x2, past0, past1, w_in, conv_w, w_out, g, b)


DMA_UNROLL = 8
GROUP_LANE0 = 0
EXPERT_LANE0 = 32


def _first_argmax(v, lane_f):
    mx = jnp.max(v, axis=1, keepdims=True)
    first = jnp.min(jnp.where(v == mx, lane_f, float(LANES)), axis=1, keepdims=True)
    return mx, first


def _router_kernel(x_ref, w_ref, bias_ref, cin_ref, e0_ref, e1_ref, g0_ref, g1_ref, r0_ref, r1_ref,
                   cnt_ref, carry_scr):
    tm = x_ref.shape[0]

    @pl.when(pl.program_id(0) == 0)
    def _():
        carry_scr[...] = cin_ref[...]

    logits = _dot3(x_ref[...], w_ref[...]) + bias_ref[...]
    lane = _iota((tm, LANES), 1)
    lane_f = lane.astype(F32)
    gl = jnp.where((lane >= GROUP_LANE0) & (lane < GROUP_LANE0 + N_GROUPS), logits, NEG_INF)
    gmax, gfirst = _first_argmax(gl, lane_f)
    pg = jnp.exp(gl - gmax)
    g_val = jnp.max(pg / jnp.sum(pg, axis=1, keepdims=True), axis=1, keepdims=True)
    g_idx = gfirst - float(GROUP_LANE0)
    grp_of_lane = ((lane - EXPERT_LANE0) // EXPERTS_PER_GROUP).astype(F32)
    in_grp = (lane >= EXPERT_LANE0) & (lane < EXPERT_LANE0 + N_EXPERTS) & (grp_of_lane == g_idx)
    el = jnp.where(in_grp, logits, NEG_INF)
    emax, first0 = _first_argmax(el, lane_f)
    pe = jnp.exp(el - emax)
    pe = pe / jnp.sum(pe, axis=1, keepdims=True)
    hit0 = lane_f == first0
    v0 = jnp.max(jnp.where(hit0, pe, 0.0), axis=1, keepdims=True)
    _, first1 = _first_argmax(jnp.where(hit0, NEG_INF, el), lane_f)
    hit1 = lane_f == first1
    v1 = jnp.max(jnp.where(hit1, pe, 0.0), axis=1, keepdims=True)
    denom = v0 + v1
    g0_ref[...] = g_val * v0 / denom
    g1_ref[...] = g_val * v1 / denom
    e0_ref[...] = first0.astype(I32) - EXPERT_LANE0
    e1_ref[...] = first1.astype(I32) - EXPERT_LANE0
    onehot = jnp.where(hit0 | hit1, 1.0, 0.0)
    lower = jnp.where(_iota((tm, tm), 1) < _iota((tm, tm), 0), 1.0, 0.0).astype(BF16)
    before = _dot(lower, onehot.astype(BF16)) + carry_scr[...]
    r0_ref[...] = jnp.sum(jnp.where(hit0, before, 0.0), axis=1, keepdims=True).astype(I32)
    r1_ref[...] = jnp.sum(jnp.where(hit1, before, 0.0), axis=1, keepdims=True).astype(I32)
    carry_scr[...] = carry_scr[...] + jnp.sum(onehot, axis=0, keepdims=True)
    cnt_ref[...] = carry_scr[...]


def _router(x2, w_slab, b_slab, counts_in):
    n, d = x2.shape
    tm = min(ROW_TILE, n)
    full = lambda a: pl.BlockSpec(a.shape, lambda i: (0, 0))
    col = pl.BlockSpec((tm, 1), lambda i: (i, 0))
    return pl.pallas_call(
        _router_kernel,
        grid=(n // tm,),
        in_specs=[pl.BlockSpec((tm, d), lambda i: (i, 0)), full(w_slab), full(b_slab), full(counts_in)],
        out_specs=[col] * 6 + [pl.BlockSpec((1, LANES), lambda i: (0, 0))],
        out_shape=[jax.ShapeDtypeStruct((n, 1), I32), jax.ShapeDtypeStruct((n, 1), I32),
                   jax.ShapeDtypeStruct((n, 1), F32), jax.ShapeDtypeStruct((n, 1), F32),
                   jax.ShapeDtypeStruct((n, 1), I32), jax.ShapeDtypeStruct((n, 1), I32),
                   jax.ShapeDtypeStruct((1, LANES), F32)],
        scratch_shapes=[pltpu.VMEM((1, LANES), F32)],
        compiler_params=_params(("arbitrary",)),
        name="moe_router",
    )(x2, w_slab, b_slab, counts_in)


def _dispatch_kernel(e0_ref, e1_ref, r0_ref, r1_ref, start_ref, x_ref, xr_in, xr_out, sem):
    del xr_in
    tm = x_ref.shape[0]
    base = pl.program_id(0) * tm

    def copies(t):
        p0 = start_ref[e0_ref[base + t]] + r0_ref[base + t]
        p1 = start_ref[e1_ref[base + t]] + r1_ref[base + t]
        src = x_ref.at[pl.ds(t, 1)]
        return (pltpu.make_async_copy(src, xr_out.at[pl.ds(p0, 1)], sem),
                pltpu.make_async_copy(src, xr_out.at[pl.ds(p1, 1)], sem))

    def start(t, _):
        for cp in copies(t):
            cp.start()
        return 0

    def wait(t, _):
        for cp in copies(t):
            cp.wait()
        return 0

    lax.fori_loop(0, tm, start, 0, unroll=DMA_UNROLL)
    lax.fori_loop(0, tm, wait, 0, unroll=DMA_UNROLL)


def _dispatch(route, start_pad, x2, xr):
    n, d = x2.shape
    tm = min(ROW_TILE, n)
    grid_spec = pltpu.PrefetchScalarGridSpec(
        num_scalar_prefetch=5,
        grid=(n // tm,),
        in_specs=[pl.BlockSpec((tm, d), lambda i, *_: (i, 0)), pl.BlockSpec(memory_space=pl.ANY)],
        out_specs=pl.BlockSpec(memory_space=pl.ANY),
        scratch_shapes=[pltpu.SemaphoreType.DMA(())],
    )
    return pl.pallas_call(
        _dispatch_kernel,
        grid_spec=grid_spec,
        out_shape=jax.ShapeDtypeStruct(xr.shape, F32),
        input_output_aliases={6: 0},
        compiler_params=_params(("arbitrary",)),
        name="moe_dispatch",
    )(*route, start_pad, x2, xr)


def _ffn_kernel(be_ref, nu_ref, x_ref, wg_ref, wu_ref, wd_ref, o_ref, wg_s, wu_s, wd_s):
    i = pl.program_id(0)

    @pl.when(i < nu_ref[0])
    def _():
        @pl.when((i == 0) | (be_ref[i] != be_ref[jnp.maximum(i - 1, 0)]))
        def _():
            wg_s[...] = wg_ref[0, 0].astype(BF16)
            wu_s[...] = wu_ref[0, 0].astype(BF16)
            wd_s[...] = wd_ref[0, 0].astype(BF16)

        xb = x_ref[...].astype(BF16)
        gate = _dot(xb, wg_s[...])
        hidden = gate * jax.nn.sigmoid(gate) * _dot(xb, wu_s[...])
        o_ref[...] = _dot(hidden.astype(BF16), wd_s[...])

    @pl.when(i >= nu_ref[0])
    def _():
        o_ref[...] = jnp.zeros(o_ref.shape, F32)


def _grouped_ffn(block_expert, n_used, xr, w_gate, w_up, w_down, layer):
    rows, d = xr.shape
    de = w_gate.shape[3]
    nblk = rows // FFN_ROWS
    grid_spec = pltpu.PrefetchScalarGridSpec(
        num_scalar_prefetch=2,
        grid=(nblk,),
        in_specs=[pl.BlockSpec((FFN_ROWS, d), lambda i, be, nu: (jnp.minimum(i, nu[0] - 1), 0)),
                  pl.BlockSpec((1, 1, d, de), lambda i, be, nu: (layer, be[i], 0, 0)),
                  pl.BlockSpec((1, 1, d, de), lambda i, be, nu: (layer, be[i], 0, 0)),
                  pl.BlockSpec((1, 1, de, d), lambda i, be, nu: (layer, be[i], 0, 0))],
        out_specs=pl.BlockSpec((FFN_ROWS, d), lambda i, be, nu: (i, 0)),
        scratch_shapes=[pltpu.VMEM((d, de), BF16), pltpu.VMEM((d, de), BF16), pltpu.VMEM((de, d), BF16)],
    )
    return pl.pallas_call(
        _ffn_kernel,
        grid_spec=grid_spec,
        out_shape=jax.ShapeDtypeStruct((rows, d), F32),
        compiler_params=_params(("arbitrary",)),
        name="moe_ffn",
    )(block_expert, n_used, xr, w_gate, w_up, w_down)


def _combine_kernel(e0_ref, e1_ref, r0_ref, r1_ref, start_ref, x_ref, g0_ref, g1_ref, lg_ref, lb_ref,
                    yr_ref, o_ref, ybuf, sem, *, alpha):
    tm = x_ref.shape[0]
    base = pl.program_id(0) * tm

    def copies(t):
        p0 = start_ref[e0_ref[base + t]] + r0_ref[base + t]
        p1 = start_ref[e1_ref[base + t]] + r1_ref[base + t]
        return (pltpu.make_async_copy(yr_ref.at[pl.ds(p0, 1)], ybuf.at[0, pl.ds(t, 1)], sem),
                pltpu.make_async_copy(yr_ref.at[pl.ds(p1, 1)], ybuf.at[1, pl.ds(t, 1)], sem))

    def start(t, _):
        for cp in copies(t):
            cp.start()
        return 0

    def wait(t, _):
        for cp in copies(t):
            cp.wait()
        return 0

    lax.fori_loop(0, tm, start, 0, unroll=DMA_UNROLL)
    lax.fori_loop(0, tm, wait, 0, unroll=DMA_UNROLL)
    y = ybuf[0] * g0_ref[...] + ybuf[1] * g1_ref[...]
    o_ref[...] = _layer_norm(alpha * x_ref[...] + y, lg_ref[...], lb_ref[...])


def _combine(route, start_pad, x2, g0, g1, ln_g, ln_b, yr, alpha):
    n, d = x2.shape
    tm = min(ROW_TILE, n)
    col = pl.BlockSpec((tm, 1), lambda i, *_: (i, 0))
    vec = pl.BlockSpec((1, d), lambda i, *_: (0, 0))
    grid_spec = pltpu.PrefetchScalarGridSpec(
        num_scalar_prefetch=5,
        grid=(n // tm,),
        in_specs=[pl.BlockSpec((tm, d), lambda i, *_: (i, 0)), col, col, vec, vec,
                  pl.BlockSpec(memory_space=pl.ANY)],
        out_specs=pl.BlockSpec((tm, d), lambda i, *_: (i, 0)),
        scratch_shapes=[pltpu.VMEM((2, tm, d), F32), pltpu.SemaphoreType.DMA(())],
    )
    return pl.pallas_call(
        functools.partial(_combine_kernel, alpha=alpha),
        grid_spec=grid_spec,
        out_shape=jax.ShapeDtypeStruct((n, d), F32),
        compiler_params=_params(("arbitrary",)),
        name="moe_combine_ln",
    )(*route, start_pad, x2, g0, g1, ln_g, ln_b, yr)


def _hier_moe_ln(xs, w_rg, b_rg, w_re, b_re, w_gate, w_up, w_down, layer, ln_g, ln_b, alpha,
                 sorted_rows=None):
    d = xs[0].shape[1]
    w_slab = jnp.zeros((d, LANES), F32)
    w_slab = w_slab.at[:, GROUP_LANE0:GROUP_LANE0 + N_GROUPS].set(w_rg)
    w_slab = w_slab.at[:, EXPERT_LANE0:EXPERT_LANE0 + N_EXPERTS].set(w_re)
    b_slab = jnp.zeros((1, LANES), F32)
    b_slab = b_slab.at[0, GROUP_LANE0:GROUP_LANE0 + N_GROUPS].set(b_rg)
    b_slab = b_slab.at[0, EXPERT_LANE0:EXPERT_LANE0 + N_EXPERTS].set(b_re)

    counts = jnp.zeros((1, LANES), F32)
    routes, gates = [], []
    for x2 in xs:
        e0, e1, g0, g1, r0, r1, counts = _router(x2, w_slab, b_slab, counts)
        routes.append(tuple(a.reshape(-1) for a in (e0, e1, r0, r1)))
        gates.append((g0, g1))

    cnt = counts[0, EXPERT_LANE0:EXPERT_LANE0 + N_EXPERTS].astype(I32)
    padded = (cnt + FFN_ROWS - 1) // FFN_ROWS * FFN_ROWS
    end_pad = jnp.cumsum(padded)
    start_pad = (end_pad - padded).astype(I32)
    n_assign = 2 * sum(x2.shape[0] for x2 in xs)
    nblk = -(-n_assign // FFN_ROWS) + N_EXPERTS
    blk_start = jnp.arange(nblk, dtype=I32) * FFN_ROWS
    block_expert = jnp.minimum(
        jnp.sum((end_pad[None, :] <= blk_start[:, None]).astype(I32), axis=1), N_EXPERTS - 1)
    n_used = (end_pad[-1:] // FFN_ROWS).astype(I32)

    xr = jnp.zeros((nblk * FFN_ROWS, d), F32) if sorted_rows is None else sorted_rows
    for x2, route in zip(xs, routes):
        xr = _dispatch(route, start_pad, x2, xr)
    yr = _grouped_ffn(block_expert, n_used, xr, w_gate, w_up, w_down, layer)
    return [_combine(route, start_pad, x2, g0, g1, ln_g, ln_b, yr, alpha)
            for x2, route, (g0, g1) in zip(xs, routes, gates)], xr


def _stack(arrays):
    return arrays[0][None] if len(arrays) == 1 else jnp.stack(arrays)


def kernel(x_prompt, x_sample, cache_k_a, cache_v_a, cache_k_b, cache_v_b, cache_k_idx, state_conv, page_table, rel_bias, w_attn_in, w_attn_out, w_conv_in, conv_w, w_conv_out, w_router_group, b_router_group, w_router_expert, b_router_expert, w_exp_gate, w_exp_up, w_exp_down, ln_g, ln_b):
    depth = ln_g.shape[0]
    alpha = float((2 * depth) ** 0.25)
    bp, tp, d = x_prompt.shape
    bs, ts, _ = x_sample.shape
    assert ts >= CONV_W - 1
    xp = x_prompt.reshape(bp * tp, d)
    xs = x_sample.reshape(bs * ts, d)
    attn_p, attn_s, conv_p, conv_s = [], [], [], []
    kv_cols = (1, 2, 4, 5, 7)
    sorted_rows = None
    for layer in range(depth):
        i = layer // 2
        g_mix, b_mix = ln_g[layer, 0][None], ln_b[layer, 0][None]
        if layer % 2 == 0:
            w_in = jnp.pad(w_attn_in[i], ((0, 0), (0, _ATTN_IN_PAD - _ATTN_IN))).astype(BF16)
            w_out = w_attn_out[i].astype(BF16)
            qa, ka, qb, kb, qi, ki, kat, vat, kbt, vbt, kit, wit = _attn_inproj_prompt(
                xp, w_in, _transposed_proj_weight(w_attn_in[i]), bp, tp)
            proj_s = _attn_inproj(xs, w_in)
            ya_p = _moba_prompt8(qa, ka, vat, rel_bias, bp, tp)
            yb_p = _dsa_prompt_t(qb, qi, wit, kb, vbt, ki, rel_bias, bp, tp)
            caches_t = [jnp.moveaxis(c[i], 1, -1).reshape(c.shape[1], -1, c.shape[2])
                        for c in (cache_k_a, cache_v_a, cache_k_b, cache_v_b, cache_k_idx)]
            ya_s, yb_s = _sample_attn([a.reshape(bs, ts, -1) for a in proj_s], caches_t,
                                      page_table, rel_bias, bs, ts)
            xp = _attn_out(ya_p, yb_p, xp, w_out, g_mix, b_mix, alpha)
            xs = _attn_out(ya_s.reshape(bs * ts, -1), yb_s.reshape(bs * ts, -1), xs, w_out,
                           g_mix, b_mix, alpha)
            attn_p.append([kat, vat, kbt, vbt, kit])
            attn_s.append([proj_s[c] for c in kv_cols])
        else:
            w_in = w_conv_in[i].astype(BF16)
            w_out = w_conv_out[i].astype(BF16)
            xp, st_p = _conv_prompt(xp, w_in, conv_w[i], w_out, g_mix, b_mix, alpha, bp, tp)
            past = state_conv[i]
            xs, u_s = _conv_sample(xs, jnp.repeat(past[:, 0], ts, axis=0),
                                   jnp.repeat(past[:, 1], ts, axis=0),
                                   w_in, conv_w[i], w_out, g_mix, b_mix, alpha, ts)
            conv_p.append(st_p)
            conv_s.append(u_s.reshape(bs, ts, -1)[:, ts - (CONV_W - 1):])
        (xp, xs), sorted_rows = _hier_moe_ln(
            [xp, xs], w_router_group[layer], b_router_group[layer],
            w_router_expert[layer], b_router_expert[layer],
            w_exp_gate, w_exp_up, w_exp_down, layer,
            ln_g[layer, 1][None], ln_b[layer, 1][None], alpha, sorted_rows)

    def rows(group, j, b, t, shape):
        return _stack([layer_rows[j].reshape((b, t) + shape) for layer_rows in group])

    def rows_t(group, j, b, t, shape):
        return _stack([jnp.moveaxis(layer_rows[j].reshape((b,) + shape + (t,)), -1, 1)
                       for layer_rows in group])

    head_shapes = ((H_A, HEAD_DIM), (H_A, HEAD_DIM), (KV_B, HEAD_DIM), (KV_B, HEAD_DIM), (D_IDX,))
    return ((xp.reshape(bp, tp, d), xs.reshape(bs, ts, d))
            + tuple(rows_t(attn_p, j, bp, tp, head_shapes[j]) for j in range(5))
            + (_stack(conv_p),)
            + tuple(rows(attn_s, j, bs, ts, head_shapes[j]) for j in range(5))
            + (_stack(conv_s),))
```
